```python
import jax, jax.numpy as jnp
from jax import lax
import numpy as np

D_MODEL = 1024
BATCH = 2
SEQ = 8192
DEPTH = 2

HEAD_DIM = 64
MOBA_HEADS = 8
MOBA_BLOCK = 256
MOBA_TOPK = 3
MOBA_QCHUNK = 128
MOBA_W = MOBA_HEADS * HEAD_DIM
DIL_GROUPS = ((128, 1), (512, 4), (2048, 16))
DIL_HEADS_PER_GROUP = 4
DIL_HEADS = DIL_HEADS_PER_GROUP * len(DIL_GROUPS)
DIL_W = DIL_HEADS * HEAD_DIM
DIL_OUT_W = DIL_HEADS_PER_GROUP * HEAD_DIM
CONV_WIDTH = 512
CONV_K = 3
N_BRANCH = 3
SPLIT_SIZES = (MOBA_W, MOBA_W, MOBA_W, DIL_W, DIL_W, DIL_W,
               CONV_WIDTH, CONV_WIDTH, CONV_WIDTH, N_BRANCH * D_MODEL)
IN_COLS = sum(SPLIT_SIZES)
D_FF = 2816
FFN_CONV_K = 3
DN_ALPHA = (2 * DEPTH) ** 0.25
DN_BETA = (8 * DEPTH) ** -0.25
LN_EPS = 1e-5

kernel_name = "hybrid_moba_dilated_shortconv_deepnorm"


def layer_norm(x, g, b):
    xf = x.astype(jnp.float32)
    mu = jnp.mean(xf, axis=-1, keepdims=True)
    var = jnp.mean(jnp.square(xf - mu), axis=-1, keepdims=True)
    return ((xf - mu) * lax.rsqrt(var + LN_EPS)).astype(x.dtype) * g + b


def causal_dwconv(u, w):
    k = w.shape[0]
    s = u.shape[1]
    up = jnp.pad(u, ((0, 0), (k - 1, 0), (0, 0)))
    y = up[:, k - 1:k - 1 + s] * w[0]
    for j in range(1, k):
        y = y + up[:, k - 1 - j:k - 1 - j + s] * w[j]
    return y


def moba_attention(q, k, v):
    b, s, h, dh = q.shape
    nb = -(-s // MOBA_BLOCK)
    sp = nb * MOBA_BLOCK
    pad = ((0, 0), (0, sp - s), (0, 0), (0, 0))
    q, k, v = (jnp.pad(t, pad).transpose(0, 2, 1, 3) for t in (q, k, v))
    kb = k.reshape(b, h, nb, MOBA_BLOCK, dh)
    vb = v.reshape(b, h, nb, MOBA_BLOCK, dh)
    scale = HEAD_DIM ** -0.5
    kmean = jnp.mean(kb, axis=3)
    gate = jnp.einsum('bhsd,bhnd->bhsn', q, kmean).astype(jnp.float32)
    q_blk = jnp.arange(sp) // MOBA_BLOCK
    past = jnp.arange(nb)[None, :] < q_blk[:, None]
    gate = jnp.where(past, gate, -jnp.inf)
    n_sel = max(1, min(MOBA_TOPK, nb - 1))
    _, sel = lax.top_k(gate, n_sel)
    sel_valid = sel < q_blk[:, None]

    nc = sp // MOBA_QCHUNK
    qc_len = MOBA_QCHUNK

    def to_chunks(t):
        t = t.reshape(b, h, nc, qc_len, *t.shape[3:])
        return jnp.moveaxis(t, 2, 0)

    bi = jnp.arange(b)[:, None, None, None]
    hi = jnp.arange(h)[None, :, None, None]
    n_g = n_sel * MOBA_BLOCK

    def chunk_attn(args):
        qc, selc, validc, c = args
        q_pos = c * qc_len + jnp.arange(qc_len)
        blk = (c * qc_len) // MOBA_BLOCK
        kg = kb[bi, hi, selc]
        vg = vb[bi, hi, selc]
        ko = lax.dynamic_index_in_dim(kb, blk, axis=2, keepdims=False)
        vo = lax.dynamic_index_in_dim(vb, blk, axis=2, keepdims=False)
        s_sel = jnp.einsum('bhqd,bhqknd->bhqkn', qc, kg).astype(jnp.float32) * scale
        s_sel = jnp.where(validc[..., None], s_sel, -jnp.inf).reshape(b, h, qc_len, n_g)
        s_own = jnp.einsum('bhqd,bhnd->bhqn', qc, ko).astype(jnp.float32) * scale
        k_pos = blk * MOBA_BLOCK + jnp.arange(MOBA_BLOCK)
        s_own = jnp.where(k_pos[None, :] <= q_pos[:, None], s_own, -jnp.inf)
        p = jax.nn.softmax(jnp.concatenate([s_sel, s_own], axis=-1), axis=-1).astype(v.dtype)
        p_sel = p[..., :n_g].reshape(b, h, qc_len, n_sel, MOBA_BLOCK)
        return (jnp.einsum('bhqkn,bhqknd->bhqd', p_sel, vg)
                + jnp.einsum('bhqn,bhnd->bhqd', p[..., n_g:], vo))

    out = lax.map(chunk_attn, (to_chunks(q), to_chunks(sel), to_chunks(sel_valid),
                               jnp.arange(nc)))
    out = jnp.moveaxis(out, 0, 2).reshape(b, h, sp, dh)[:, :, :s]
    return out.transpose(0, 2, 1, 3).reshape(b, s, h * dh)


def dilated_group_attention(q, k, v, window, dilation):
    b, s, h, dh = q.shape
    span = window // dilation
    blk = span
    l = s // dilation
    nb = -(-l // blk)
    lp = nb * blk
    scale = HEAD_DIM ** -0.5

    def split(t):
        t = t.reshape(b, l, dilation, h, dh).transpose(0, 2, 3, 1, 4)
        return jnp.pad(t, ((0, 0), (0, 0), (0, 0), (0, lp - l), (0, 0)))

    def band(t):
        tb = t.reshape(b, dilation, h, nb, blk, dh)
        prev = jnp.pad(tb, ((0, 0), (0, 0), (0, 0), (1, 0), (0, 0), (0, 0)))[:, :, :, :-1]
        return jnp.concatenate([prev, tb], axis=4)

    qs, ks, vs = split(q), split(k), split(v)
    qb = qs.reshape(b, dilation, h, nb, blk, dh)
    kb, vb = band(ks), band(vs)
    sc = jnp.einsum('brhnqd,brhnkd->brhnqk', qb, kb).astype(jnp.float32) * scale
    q_pos = jnp.arange(nb)[:, None, None] * blk + jnp.arange(blk)[None, :, None]
    k_pos = jnp.arange(nb)[:, None, None] * blk - blk + jnp.arange(2 * blk)[None, None, :]
    dist = q_pos - k_pos
    mask = (dist >= 0) & (dist <= span) & (k_pos >= 0)
    sc = jnp.where(mask, sc, -jnp.inf)
    lse = jax.nn.logsumexp(sc, axis=-1)
    p = jnp.exp(sc - lse[..., None]).astype(v.dtype)
    o = jnp.einsum('brhnqk,brhnkd->brhnqd', p, vb)

    def merge(t):
        t = t.reshape(b, dilation, h, lp, *t.shape[5:])[:, :, :, :l]
        t = jnp.moveaxis(t, 3, 1)
        return t.reshape(b, s, h, *t.shape[4:])

    return merge(o), merge(lse)


def dilated_mixture(q, k, v):
    b, s = q.shape[:2]
    outs, lses = [], []
    for g, (window, dilation) in enumerate(DIL_GROUPS):
        hs = slice(g * DIL_HEADS_PER_GROUP, (g + 1) * DIL_HEADS_PER_GROUP)
        o, lse = dilated_group_attention(q[:, :, hs], k[:, :, hs], v[:, :, hs], window, dilation)
        outs.append(o)
        lses.append(lse)
    wts = jax.nn.softmax(jnp.stack(lses, axis=0), axis=0).astype(q.dtype)
    out = jnp.einsum('gbsh,gbshd->bshd', wts, jnp.stack(outs, axis=0))
    return out.reshape(b, s, DIL_OUT_W)


def mixer_sublayer(x, w_in, w_short_conv, w_moba_proj, w_dil_proj, w_conv_proj, w_mix_out):
    b, s, _ = x.shape
    split_points = tuple(int(i) for i in np.cumsum(SPLIT_SIZES)[:-1])
    qa, ka, va, qd, kd, vd, b_gate, c_gate, hc, gate_logits = jnp.split(
        x @ w_in, split_points, axis=-1)
    heads = lambda t, n: t.reshape(b, s, n, HEAD_DIM)
    y_moba = moba_attention(heads(qa, MOBA_HEADS), heads(ka, MOBA_HEADS), heads(va, MOBA_HEADS))
    y_conv = b_gate * causal_dwconv(c_gate * hc, w_short_conv)
    y_dil = dilated_mixture(heads(qd, DIL_HEADS), heads(kd, DIL_HEADS), heads(vd, DIL_HEADS))
    gates = jax.nn.sigmoid(gate_logits).reshape(b, s, N_BRANCH, D_MODEL)
    merged = (gates[:, :, 0] * (y_moba @ w_moba_proj)
              + gates[:, :, 1] * (y_conv @ w_conv_proj)
              + gates[:, :, 2] * (y_dil @ w_dil_proj))
    return merged @ w_mix_out


def conv_ffn_sublayer(x, w_up, w_ffn_conv, b_ffn_conv, w_down):
    u = causal_dwconv(x @ w_up, w_ffn_conv) + b_ffn_conv
    gate, val = jnp.split(u, 2, axis=-1)
    return (jax.nn.silu(gate) * val) @ w_down


def setup_inputs(seed: int = 0) -> dict:
    key = jax.random.key(seed)
    ks = jax.random.split(key, 16)
    nrm = lambda k, shape, sc: jax.random.normal(k, shape, jnp.float32) * sc
    L = DEPTH
    return {
        "x": nrm(ks[0], (BATCH, SEQ, D_MODEL), 1.0),
        "w_in": nrm(ks[1], (L, D_MODEL, IN_COLS), D_MODEL ** -0.5),
        "w_short_conv": nrm(ks[2], (L, CONV_K, CONV_WIDTH), CONV_K ** -0.5),
        "w_moba_proj": nrm(ks[3], (L, MOBA_W, D_MODEL), MOBA_W ** -0.5),
        "w_dil_proj": nrm(ks[4], (L, DIL_OUT_W, D_MODEL), DIL_OUT_W ** -0.5),
        "w_conv_proj": nrm(ks[5], (L, CONV_WIDTH, D_MODEL), CONV_WIDTH ** -0.5),
        "w_mix_out": nrm(ks[6], (L, D_MODEL, D_MODEL), DN_BETA * D_MODEL ** -0.5),
        "ln1_g": 1.0 + nrm(ks[7], (L, D_MODEL), 0.02),
        "ln1_b": nrm(ks[8], (L, D_MODEL), 0.02),
        "w_up": nrm(ks[9], (L, D_MODEL, 2 * D_FF), D_MODEL ** -0.5),
        "w_ffn_conv": nrm(ks[10], (L, FFN_CONV_K, 2 * D_FF), FFN_CONV_K ** -0.5),
        "b_ffn_conv": nrm(ks[11], (L, 2 * D_FF), 0.02),
        "w_down": nrm(ks[12], (L, D_FF, D_MODEL), DN_BETA * D_FF ** -0.5),
        "ln2_g": 1.0 + nrm(ks[13], (L, D_MODEL), 0.02),
        "ln2_b": nrm(ks[14], (L, D_MODEL), 0.02),
    }


def reference(x, w_in, w_short_conv, w_moba_proj, w_dil_proj, w_conv_proj, w_mix_out,
              ln1_g, ln1_b, w_up, w_ffn_conv, b_ffn_conv, w_down, ln2_g, ln2_b):
    for l in range(DEPTH):
        mix = mixer_sublayer(x, w_in[l], w_short_conv[l], w_moba_proj[l], w_dil_proj[l],
                             w_conv_proj[l], w_mix_out[l])
        x = layer_norm(DN_ALPHA * x + mix, ln1_g[l], ln1_b[l])
        ffn = conv_ffn_sublayer(x, w_up[l], w_ffn_conv[l], b_ffn_conv[l], w_down[l])
        x = layer_norm(DN_ALPHA * x + ffn, ln2_g[l], ln2_b[l])
    return x
```

```python
import functools

import jax
import jax.numpy as jnp
import numpy as np
from jax import lax
from jax.experimental import pallas as pl
from jax.experimental.pallas import tpu as pltpu

D_MODEL = 1024
HEAD_DIM = 64
MOBA_HEADS = 8
MOBA_BLOCK = 256
MOBA_TOPK = 3
MOBA_W = MOBA_HEADS * HEAD_DIM
DIL_GROUPS = ((128, 1), (512, 4), (2048, 16))
DIL_HEADS_PER_GROUP = 4
DIL_W = 3 * DIL_HEADS_PER_GROUP * HEAD_DIM
DIL_OUT_W = DIL_HEADS_PER_GROUP * HEAD_DIM
CONV_WIDTH = 512
N_BRANCH = 3
IN_COLS = 3 * MOBA_W + 3 * DIL_W + 3 * CONV_WIDTH + N_BRANCH * D_MODEL
D_FF = 2816
LN_EPS = 1e-5

OFF_MOBA = 0
OFF_CONV = 3 * MOBA_W
OFF_GATE = OFF_CONV + 3 * CONV_WIDTH
OFF_DIL = OFF_GATE + N_BRANCH * D_MODEL

LANES = 128
NEG = -1e30
VMEM_LIMIT = 56 * 1024 * 1024

PROJ_TM = 1024
PROJ_TN = 1408
DIL_TQ = 512
DIL_SPAN = 128
MERGE_TM = 512
FFN_TM = 512
FFN_HALO = 16
FFN_CHUNK = 256
FFN_NCHUNK = D_FF // FFN_CHUNK


def _cparams(sem):
    return pltpu.CompilerParams(dimension_semantics=sem, vmem_limit_bytes=VMEM_LIMIT)


def _proj_kernel(x_ref, w_ref, o_ref, xb_ref):
    @pl.when(pl.program_id(1) == 0)
    def _():
        xb_ref[...] = x_ref[...].astype(jnp.bfloat16)

    o_ref[...] = jnp.dot(xb_ref[...], w_ref[...],
                         preferred_element_type=jnp.float32).astype(o_ref.dtype)


def _in_proj(x2, w):
    n, d = x2.shape
    cols = w.shape[1]
    return pl.pallas_call(
        _proj_kernel,
        grid=(n // PROJ_TM, cols // PROJ_TN),
        in_specs=[pl.BlockSpec((PROJ_TM, d), lambda i, j: (i, 0)),
                  pl.BlockSpec((d, PROJ_TN), lambda i, j: (0, j))],
        out_specs=pl.BlockSpec((PROJ_TM, PROJ_TN), lambda i, j: (i, j)),
        out_shape=jax.ShapeDtypeStruct((n, cols), jnp.bfloat16),
        scratch_shapes=[pltpu.VMEM((PROJ_TM, d), jnp.bfloat16)],
        compiler_params=_cparams(("arbitrary", "arbitrary")),
        name="in_proj",
    )(x2, w)


def _moba_kernel(q_ref, k_ref, v_ref, o_ref, kaug_ref, vaug_ref, km_ref):
    s = q_ref.shape[0]
    nb = s // MOBA_BLOCK
    tq = MOBA_BLOCK
    lane_s = lax.broadcasted_iota(jnp.int32, (s, LANES), 1)
    row_blk = lax.broadcasted_iota(jnp.int32, (s, LANES), 0) // MOBA_BLOCK
    k2 = k_ref[...].astype(jnp.float32)
    v2 = v_ref[...].astype(jnp.float32)
    bf = jnp.bfloat16
    kaug_ref[0] = jnp.where(lane_s < HEAD_DIM, k2,
                            jnp.where(lane_s - HEAD_DIM == row_blk, 1.0, 0.0)).astype(bf)
    kaug_ref[1] = jnp.where(lane_s >= HEAD_DIM, k2,
                            jnp.where(lane_s == row_blk, 1.0, 0.0)).astype(bf)
    vaug_ref[0] = jnp.where(lane_s < HEAD_DIM, v2, 1.0).astype(bf)
    vaug_ref[1] = jnp.where(lane_s >= HEAD_DIM, v2, 1.0).astype(bf)

    ksum = jnp.sum(k2.reshape(nb, MOBA_BLOCK, LANES), axis=1)
    kmean = ksum * (1.0 / MOBA_BLOCK)
    lane_b = lax.broadcasted_iota(jnp.int32, (nb, LANES), 1)
    km0 = jnp.where(lane_b < HEAD_DIM, kmean, 0.0)
    km1 = jnp.where(lane_b >= HEAD_DIM, kmean, 0.0)
    zpad = jnp.zeros((LANES - HEAD_DIM - nb, LANES), jnp.float32)
    km_ref[0] = jnp.concatenate(
        [jnp.zeros((HEAD_DIM, LANES), jnp.float32), km0, zpad], axis=0).astype(jnp.bfloat16)
    km_ref[1] = jnp.concatenate(
        [km1, jnp.zeros((LANES - nb, LANES), jnp.float32)], axis=0).astype(jnp.bfloat16)

    lane_t = lax.broadcasted_iota(jnp.int32, (tq, LANES), 1)
    lane_tf = lane_t.astype(jnp.float32)
    r_i = lax.broadcasted_iota(jnp.int32, (tq, tq), 0)
    c_i = lax.broadcasted_iota(jnp.int32, (tq, tq), 1)
    nt = (((1,), (1,)), ((), ()))

    for h in range(2):
        off = HEAD_DIM if h == 0 else 0
        q_lanes = (lane_t < HEAD_DIM) if h == 0 else (lane_t >= HEAD_DIM)
        blk = lane_t - off
        in_blk = (blk >= 0) & (blk < nb)
        kmh = km_ref[h]

        def q_tile(j, carry, h=h, q_lanes=q_lanes, blk=blk, in_blk=in_blk, kmh=kmh):
            r0 = pl.multiple_of(j * tq, tq)
            q2 = q_ref[pl.ds(r0, tq), :]
            gate = lax.dot_general(q2, kmh, nt, preferred_element_type=jnp.float32)
            valid = in_blk & (blk < j)
            g = jnp.where(valid, gate, NEG)
            sel = jnp.zeros((tq, LANES), jnp.bool_)
            for _ in range(MOBA_TOPK):
                m = jnp.max(g, axis=1, keepdims=True)
                idx = jnp.min(jnp.where(g == m, lane_tf, 1e9), axis=1, keepdims=True)
                pick = lane_tf == idx
                sel = sel | pick
                g = jnp.where(pick, NEG, g)
            keep = (sel & valid) | (blk == j)
            pen = jnp.where(in_blk & jnp.logical_not(keep), NEG, 0.0)
            qa = jnp.where(q_lanes, q2.astype(jnp.float32), pen).astype(jnp.bfloat16)

            sc = lax.dot_general(qa, kaug_ref[h, pl.ds(r0, tq), :], nt,
                                 preferred_element_type=jnp.float32)
            sc = jnp.where(c_i <= r_i, sc, NEG)
            m0 = jnp.max(sc, axis=1, keepdims=True)
            p = jnp.exp(sc - m0)
            acc0 = jnp.dot(p.astype(jnp.bfloat16), vaug_ref[h, pl.ds(r0, tq), :],
                           preferred_element_type=jnp.float32)

            def kv_step(n, mc):
                m_old, acc = mc
                k0 = pl.multiple_of(n * tq, tq)
                sc = lax.dot_general(qa, kaug_ref[h, pl.ds(k0, tq), :], nt,
                                     preferred_element_type=jnp.float32)
                m_new = jnp.maximum(m_old, jnp.max(sc, axis=1, keepdims=True))
                alpha = jnp.exp(m_old - m_new)
                p = jnp.exp(sc - m_new)
                acc = acc * alpha + jnp.dot(p.astype(jnp.bfloat16),
                                            vaug_ref[h, pl.ds(k0, tq), :],
                                            preferred_element_type=jnp.float32)
                return m_new, acc

            _, acc = lax.fori_loop(0, j, kv_step, (m0, acc0))
            den = pltpu.roll(acc, HEAD_DIM, axis=1)
            out = (acc / den).astype(o_ref.dtype)
            if h == 0:
                o_ref[pl.ds(r0, tq), 0:HEAD_DIM] = out[:, 0:HEAD_DIM]
            else:
                o_ref[pl.ds(r0, tq), HEAD_DIM:LANES] = out[:, HEAD_DIM:LANES]
            return carry

        lax.fori_loop(0, nb, q_tile, 0)


def _moba(p3):
    b, s, _ = p3.shape
    n_pairs = MOBA_HEADS // 2
    blk = (None, s, LANES)
    return pl.pallas_call(
        _moba_kernel,
        grid=(b, n_pairs),
        in_specs=[pl.BlockSpec(blk, lambda i, j: (i, 0, j)),
                  pl.BlockSpec(blk, lambda i, j: (i, 0, n_pairs + j)),
                  pl.BlockSpec(blk, lambda i, j: (i, 0, 2 * n_pairs + j))],
        out_specs=pl.BlockSpec(blk, lambda i, j: (i, 0, j)),
        out_shape=jax.ShapeDtypeStruct((b, s, MOBA_W), jnp.bfloat16),
        scratch_shapes=[pltpu.VMEM((2, s, LANES), jnp.bfloat16),
                        pltpu.VMEM((2, s, LANES), jnp.bfloat16),
                        pltpu.VMEM((2, LANES, LANES), jnp.bfloat16)],
        compiler_params=_cparams(("arbitrary", "arbitrary")),
        name="moba_attn",
    )(p3, p3, p3)


def _dil_kernel(q_ref, k_ref, v_ref, o_ref, lse_ref):
    tq = q_ref.shape[0]
    sub = DIL_SPAN
    kw = 2 * sub
    base_tile = pl.program_id(2) * tq
    lane = lax.broadcasted_iota(jnp.int32, (sub, LANES), 1)
    first = lane < HEAD_DIM
    r_i = lax.broadcasted_iota(jnp.int32, (sub, kw), 0)
    c_i = lax.broadcasted_iota(jnp.int32, (sub, kw), 1)
    nt = (((1,), (1,)), ((), ()))

    def sub_block(sb, carry):
        q0 = pl.multiple_of(sb * sub, sub)
        base = base_tile + q0
        kstart = pl.multiple_of(jnp.maximum(base - sub, 0), sub)
        dist = (base + r_i) - (kstart + c_i)
        mask = (dist >= 0) & (dist <= DIL_SPAN)
        for pr in range(DIL_HEADS_PER_GROUP // 2):
            cs = slice(pr * LANES, (pr + 1) * LANES)
            q2 = q_ref[pl.ds(q0, sub), cs].astype(jnp.float32)
            k2 = k_ref[pl.ds(kstart, kw), cs]
            v2 = v_ref[pl.ds(kstart, kw), cs]
            outs, lses = [], []
            for h in range(2):
                qh = jnp.where(first if h == 0 else jnp.logical_not(first), q2, 0.0)
                qh = qh.astype(jnp.bfloat16)
                sc = lax.dot_general(qh, k2, nt, preferred_element_type=jnp.float32)
                sc = jnp.where(mask, sc, NEG)
                m = jnp.max(sc, axis=1, keepdims=True)
                p = jnp.exp(sc - m)
                l = jnp.sum(p, axis=1, keepdims=True)
                o = jnp.dot(p.astype(jnp.bfloat16), v2, preferred_element_type=jnp.float32)
                outs.append(o / l)
                lses.append(jnp.broadcast_to(m + jnp.log(l), (sub, LANES)))
            o_ref[pl.ds(q0, sub), cs] = jnp.where(first, outs[0], outs[1]).astype(o_ref.dtype)
            lse_ref[pl.ds(q0, sub), cs] = jnp.where(first, lses[0], lses[1])
        return carry

    lax.fori_loop(0, tq // sub, sub_block, 0)


def _dil_group(p3, g, dil):
    b, s, cols = p3.shape
    l = s // dil
    pv = p3.reshape(b, l, dil * cols)
    w = DIL_OUT_W
    cb = cols // w
    qb = OFF_DIL // w + g
    kb = (OFF_DIL + DIL_W) // w + g
    vb = (OFF_DIL + 2 * DIL_W) // w + g
    tq = min(DIL_TQ, l)
    o, lse = pl.pallas_call(
        _dil_kernel,
        grid=(b, dil, l // tq),
        in_specs=[pl.BlockSpec((None, tq, w), lambda i, r, t: (i, t, r * cb + qb)),
                  pl.BlockSpec((None, l, w), lambda i, r, t: (i, 0, r * cb + kb)),
                  pl.BlockSpec((None, l, w), lambda i, r, t: (i, 0, r * cb + vb))],
        out_specs=[pl.BlockSpec((None, tq, w), lambda i, r, t: (i, t, r)),
                   pl.BlockSpec((None, tq, w), lambda i, r, t: (i, t, r))],
        out_shape=[jax.ShapeDtypeStruct((b, l, dil * w), jnp.bfloat16),
                   jax.ShapeDtypeStruct((b, l, dil * w), jnp.float32)],
        compiler_params=_cparams(("arbitrary", "arbitrary", "arbitrary")),
        name=f"dil_attn_g{g}",
    )(pv, pv, pv)
    return o.reshape(b, s, w), lse.reshape(b, s, w)


def _layer_norm(y, g, b):
    mu = jnp.mean(y, axis=-1, keepdims=True)
    yc = y - mu
    var = jnp.mean(yc * yc, axis=-1, keepdims=True)
    return yc * lax.rsqrt(var + LN_EPS) * g + b


def _merge_kernel(alpha, x_ref, ym_ref, bg_ref, cg_ref, hc_ref, cgp_ref, hcp_ref,
                  g0_ref, g1_ref, g2_ref, o1_ref, o2_ref, o3_ref, l1_ref, l2_ref, l3_ref,
                  wm_ref, wc_ref, wd_ref, wo_ref, wsc_ref, lng_ref, lnb_ref, out_ref):
    f32 = jnp.float32
    tm = x_ref.shape[0]
    ch = cg_ref[...].astype(f32) * hc_ref[...].astype(f32)
    chp = cgp_ref[...].astype(f32) * hcp_ref[...].astype(f32)
    chp = jnp.where(pl.program_id(1) == 0, 0.0, chp)
    hp = chp.shape[0]
    prev1 = chp[hp - 1:hp, :]
    prev2 = chp[hp - 2:hp - 1, :]
    row = lax.broadcasted_iota(jnp.int32, (tm, CONV_WIDTH), 0)
    ch1 = jnp.where(row == 0, prev1, pltpu.roll(ch, 1, axis=0))
    ch2 = pltpu.roll(ch, 2, axis=0)
    ch2 = jnp.where(row == 0, prev2, jnp.where(row == 1, prev1, ch2))
    wsc = wsc_ref[...]
    conv = ch * wsc[0:1, :] + ch1 * wsc[1:2, :] + ch2 * wsc[2:3, :]
    y_conv = (bg_ref[...].astype(f32) * conv).astype(jnp.bfloat16)

    l1, l2, l3 = l1_ref[...], l2_ref[...], l3_ref[...]
    lm = jnp.maximum(jnp.maximum(l1, l2), l3)
    e1, e2, e3 = jnp.exp(l1 - lm), jnp.exp(l2 - lm), jnp.exp(l3 - lm)
    den = e1 + e2 + e3
    y_dil = ((e1 / den) * o1_ref[...].astype(f32) + (e2 / den) * o2_ref[...].astype(f32)
             + (e3 / den) * o3_ref[...].astype(f32)).astype(jnp.bfloat16)

    pm = jnp.dot(ym_ref[...], wm_ref[...], preferred_element_type=f32)
    pc = jnp.dot(y_conv, wc_ref[...], preferred_element_type=f32)
    pd = jnp.dot(y_dil, wd_ref[...], preferred_element_type=f32)
    merged = (jax.nn.sigmoid(g0_ref[...].astype(f32)) * pm
              + jax.nn.sigmoid(g1_ref[...].astype(f32)) * pc
              + jax.nn.sigmoid(g2_ref[...].astype(f32)) * pd)
    mix = jnp.dot(merged.astype(jnp.bfloat16), wo_ref[...], preferred_element_type=f32)
    y = alpha * x_ref[...] + mix
    out_ref[...] = _layer_norm(y, lng_ref[...], lnb_ref[...])


def _merge(alpha, x3, ym, p3, dil_outs, wm, wc, wd, wo, wsc, lng, lnb):
    b, s, d = x3.shape
    tm = MERGE_TM
    hp = 16
    cw = CONV_WIDTH
    cblk = OFF_CONV // cw
    gblk = OFF_GATE // d
    row = lambda w: pl.BlockSpec((None, tm, w), lambda i, t: (i, t, 0))
    pcol = lambda w, c: pl.BlockSpec((None, tm, w), lambda i, t, c=c: (i, t, c))
    prev = lambda c: pl.BlockSpec(
        (None, hp, cw), lambda i, t, c=c: (i, jnp.maximum(t * (tm // hp) - 1, 0), c))
    full = lambda a: pl.BlockSpec(a.shape, lambda i, t: (0,) * a.ndim)
    (o1, l1), (o2, l2), (o3, l3) = dil_outs
    in_specs = [row(d), row(MOBA_W),
                pcol(cw, cblk), pcol(cw, cblk + 1), pcol(cw, cblk + 2),
                prev(cblk + 1), prev(cblk + 2),
                pcol(d, gblk), pcol(d, gblk + 1), pcol(d, gblk + 2),
                row(DIL_OUT_W), row(DIL_OUT_W), row(DIL_OUT_W),
                row(DIL_OUT_W), row(DIL_OUT_W), row(DIL_OUT_W),
                full(wm), full(wc), full(wd), full(wo), full(wsc), full(lng), full(lnb)]
    return pl.pallas_call(
        functools.partial(_merge_kernel, alpha),
        grid=(b, s // tm),
        in_specs=in_specs,
        out_specs=row(d),
        out_shape=jax.ShapeDtypeStruct((b, s, d), jnp.float32),
        compiler_params=_cparams(("arbitrary", "arbitrary")),
        name="merge",
    )(x3, ym, p3, p3, p3, p3, p3, p3, p3, p3, o1, o2, o3, l1, l2, l3,
      wm, wc, wd, wo, wsc, lng, lnb)


def _ffn_kernel(alpha, x_ref, xp_ref, wup_ref, wdn_ref, wcv_ref, bcv_ref, lng_ref, lnb_ref,
                out_ref, xs_ref, acc_ref):
    f32 = jnp.float32
    tm = x_ref.shape[0]
    halo = xp_ref.shape[0]
    xp = jnp.where(pl.program_id(1) == 0, 0.0, xp_ref[...])
    xs_ref[0:halo, :] = xp.astype(jnp.bfloat16)
    xs_ref[halo:halo + tm, :] = x_ref[...].astype(jnp.bfloat16)
    acc_ref[...] = jnp.zeros_like(acc_ref)

    def conv(u, w):
        y = u * w[0:1, :] + pltpu.roll(u, 1, axis=0) * w[1:2, :] + pltpu.roll(u, 2, axis=0) * w[2:3, :]
        return y[halo:, :]

    def chunk(c, carry):
        xs = xs_ref[...]
        ug = jnp.dot(xs, wup_ref[c], preferred_element_type=f32)
        uv = jnp.dot(xs, wup_ref[FFN_NCHUNK + c], preferred_element_type=f32)
        yg = conv(ug, wcv_ref[c]) + bcv_ref[c]
        yv = conv(uv, wcv_ref[FFN_NCHUNK + c]) + bcv_ref[FFN_NCHUNK + c]
        act = (yg * jax.nn.sigmoid(yg) * yv).astype(jnp.bfloat16)
        acc_ref[...] += jnp.dot(act, wdn_ref[c], preferred_element_type=f32)
        return carry

    lax.fori_loop(0, FFN_NCHUNK, chunk, 0)
    y = alpha * x_ref[...] + acc_ref[...]
    out_ref[...] = _layer_norm(y, lng_ref[...], lnb_ref[...])


def _ffn(alpha, x3, wup, wdn, wcv, bcv, lng, lnb):
    b, s, d = x3.shape
    tm = FFN_TM
    halo = FFN_HALO
    full = lambda a: pl.BlockSpec(a.shape, lambda i, t: (0,) * a.ndim)
    return pl.pallas_call(
        functools.partial(_ffn_kernel, alpha),
        grid=(b, s // tm),
        in_specs=[pl.BlockSpec((None, tm, d), lambda i, t: (i, t, 0)),
                  pl.BlockSpec((None, halo, d),
                               lambda i, t: (i, jnp.maximum(t * (tm // halo) - 1, 0), 0)),
                  full(wup), full(wdn), full(wcv), full(bcv), full(lng), full(lnb)],
        out_specs=pl.BlockSpec((None, tm, d), lambda i, t: (i, t, 0)),
        out_shape=jax.ShapeDtypeStruct((b, s, d), jnp.float32),
        scratch_shapes=[pltpu.VMEM((tm + halo, d), jnp.bfloat16),
                        pltpu.VMEM((tm, d), jnp.float32)],
        compiler_params=_cparams(("arbitrary", "arbitrary")),
        name="conv_ffn",
    )(x3, x3, wup, wdn, wcv, bcv, lng, lnb)


def _permute_w_in(w):
    sizes = (MOBA_W, MOBA_W, MOBA_W, DIL_W, DIL_W, DIL_W,
             CONV_WIDTH, CONV_WIDTH, CONV_WIDTH, N_BRANCH * D_MODEL)
    pts = np.cumsum((0,) + sizes)
    parts = [w[:, int(pts[i]):int(pts[i + 1])] for i in range(len(sizes))]
    qa, ka, va, qd, kd, vd, bg, cg, hc, gl = parts
    scale = HEAD_DIM ** -0.5
    return jnp.concatenate([qa * scale, ka, va, bg, cg, hc, gl, qd * scale, kd, vd], axis=1)


def kernel(x, w_in, w_short_conv, w_moba_proj, w_dil_proj, w_conv_proj, w_mix_out,
           ln1_g, ln1_b, w_up, w_ffn_conv, b_ffn_conv, w_down, ln2_g, ln2_b):
    b, s, d = x.shape
    depth = w_in.shape[0]
    alpha = float((2 * depth) ** 0.25)
    bf = jnp.bfloat16
    for l in range(depth):
        w_in_l = _permute_w_in(w_in[l]).astype(bf)
        p = _in_proj(x.reshape(b * s, d), w_in_l).reshape(b, s, IN_COLS)
        ym = _moba(p)
        dil_outs = [_dil_group(p, g, dil) for g, (_, dil) in enumerate(DIL_GROUPS)]
        x = _merge(alpha, x, ym, p, dil_outs,
                   w_moba_proj[l].astype(bf), w_conv_proj[l].astype(bf),
                   w_dil_proj[l].astype(bf), w_mix_out[l].astype(bf),
                   w_short_conv[l], ln1_g[l].reshape(1, d), ln1_b[l].reshape(1, d))
        wup = w_up[l].astype(bf).reshape(d, 2 * FFN_NCHUNK, FFN_CHUNK).transpose(1, 0, 2)
        wdn = w_down[l].astype(bf).reshape(FFN_NCHUNK, FFN_CHUNK, d)
        wcv = w_ffn_conv[l].reshape(3, 2 * FFN_NCHUNK, FFN_CHUNK).transpose(1, 0, 2)
        bcv = b_ffn_conv[l].reshape(2 * FFN_NCHUNK, 1, FFN_CHUNK)
        x = _ffn(alpha, x, wup, wdn, wcv, bcv, ln2_g[l].reshape(1, d), ln2_b[l].reshape(1, d))
    return x
```

```python
import functools

import jax
import jax.numpy as jnp
import numpy as np
from jax import lax
from jax.experimental import pallas as pl
from jax.experimental.pallas import tpu as pltpu

D_MODEL = 1024
HEAD_DIM = 64
MOBA_HEADS = 8
MOBA_BLOCK = 256
MOBA_TOPK = 3
MOBA_W = MOBA_HEADS * HEAD_DIM
DIL_GROUPS = ((128, 1), (512, 4), (2048, 16))
DIL_HEADS_PER_GROUP = 4
DIL_W = 3 * DIL_HEADS_PER_GROUP * HEAD_DIM
DIL_OUT_W = DIL_HEADS_PER_GROUP * HEAD_DIM
CONV_WIDTH = 512
N_BRANCH = 3
D_FF = 2816
LN_EPS = 1e-5

OFF_CONV = 3 * MOBA_W
OFF_GATE = OFF_CONV + 3 * CONV_WIDTH
MAIN_COLS = OFF_GATE + N_BRANCH * D_MODEL

LANES = 128
NEG = -1e30
VMEM_LIMIT = 56 * 1024 * 1024

PROJ_TM = 1024
PROJ_TN = 1536
DILP_TM = 512
MOBA_KTILE = 2 * MOBA_BLOCK
DIL_TQ = 512
DIL_SPAN = 128
MERGE_TM = 512
FFN_TM = 512
FFN_HALO = 16
FFN_CHUNK = 256
FFN_NCHUNK = D_FF // FFN_CHUNK


def _cparams(sem):
    return pltpu.CompilerParams(dimension_semantics=sem, vmem_limit_bytes=VMEM_LIMIT)


def _proj_kernel(x_ref, w_ref, o_ref, xb_ref):
    @pl.when(pl.program_id(1) == 0)
    def _():
        xb_ref[...] = x_ref[...].astype(jnp.bfloat16)

    o_ref[...] = jnp.dot(xb_ref[...], w_ref[...],
                         preferred_element_type=jnp.float32).astype(o_ref.dtype)


def _in_proj(x2, w):
    n, d = x2.shape
    cols = w.shape[1]
    return pl.pallas_call(
        _proj_kernel,
        grid=(n // PROJ_TM, cols // PROJ_TN),
        in_specs=[pl.BlockSpec((PROJ_TM, d), lambda i, j: (i, 0)),
                  pl.BlockSpec((d, PROJ_TN), lambda i, j: (0, j))],
        out_specs=pl.BlockSpec((PROJ_TM, PROJ_TN), lambda i, j: (i, j)),
        out_shape=jax.ShapeDtypeStruct((n, cols), jnp.bfloat16),
        scratch_shapes=[pltpu.VMEM((PROJ_TM, d), jnp.bfloat16)],
        compiler_params=_cparams(("arbitrary", "arbitrary")),
        name="in_proj",
    )(x2, w)


def _dil_proj_kernel(x_ref, w_ref, o0_ref, o1_ref, o2_ref, res_ref):
    tm = x_ref.shape[0]
    res = jnp.dot(x_ref[...].astype(jnp.bfloat16), w_ref[...],
                  preferred_element_type=jnp.float32)
    for cb in range(res_ref.shape[0]):
        res_ref[cb] = res[:, cb * LANES:(cb + 1) * LANES]
    slabs = DIL_OUT_W // LANES
    for g, o_ref in enumerate((o0_ref, o1_ref, o2_ref)):
        d = DIL_GROUPS[g][1]
        rows = tm // d
        for r in range(d):
            for j in range(3):
                for sl in range(slabs):
                    src_cb = (j * DIL_W + g * DIL_OUT_W) // LANES + sl
                    dst_c = (r * 3 + j) * DIL_OUT_W + sl * LANES
                    if d == 1:
                        src = res_ref[src_cb]
                    else:
                        src = res_ref[src_cb, pl.ds(r, rows, stride=d), :]
                    o_ref[:, dst_c:dst_c + LANES] = src.astype(o_ref.dtype)


def _dil_proj(x2, w):
    n, d = x2.shape
    tm = DILP_TM
    dils = [dil for _, dil in DIL_GROUPS]
    return pl.pallas_call(
        _dil_proj_kernel,
        grid=(n // tm,),
        in_specs=[pl.BlockSpec((tm, d), lambda i: (i, 0)),
                  pl.BlockSpec(w.shape, lambda i: (0, 0))],
        out_specs=[pl.BlockSpec((tm // dil, dil * 3 * DIL_OUT_W), lambda i: (i, 0))
                   for dil in dils],
        out_shape=[jax.ShapeDtypeStruct((n // dil, dil * 3 * DIL_OUT_W), jnp.bfloat16)
                   for dil in dils],
        scratch_shapes=[pltpu.VMEM((3 * DIL_W // LANES, tm, LANES), jnp.float32)],
        compiler_params=_cparams(("arbitrary",)),
        name="dil_proj",
    )(x2, w)


def _moba_kernel(q_ref, k_ref, v_ref, o_ref, kaug_ref, vaug_ref, km_ref):
    s = q_ref.shape[0]
    nb = s // MOBA_BLOCK
    tq = MOBA_BLOCK
    kt = MOBA_KTILE
    f32 = jnp.float32
    bf = jnp.bfloat16
    lane_s = lax.broadcasted_iota(jnp.int32, (s, LANES), 1)
    row_blk = lax.broadcasted_iota(jnp.int32, (s, LANES), 0) // MOBA_BLOCK
    k2 = k_ref[...].astype(f32)
    v2 = v_ref[...].astype(f32)
    kaug_ref[0] = jnp.where(lane_s < HEAD_DIM, k2,
                            jnp.where(lane_s - HEAD_DIM == row_blk, 1.0, 0.0)).astype(bf)
    kaug_ref[1] = jnp.where(lane_s >= HEAD_DIM, k2,
                            jnp.where(lane_s == row_blk, 1.0, 0.0)).astype(bf)
    vaug_ref[0] = jnp.where(lane_s < HEAD_DIM, v2, 1.0).astype(bf)
    vaug_ref[1] = jnp.where(lane_s >= HEAD_DIM, v2, 1.0).astype(bf)

    ksum = jnp.sum(k2.reshape(nb, MOBA_BLOCK, LANES), axis=1)
    kmean = ksum * (1.0 / MOBA_BLOCK)
    lane_b = lax.broadcasted_iota(jnp.int32, (nb, LANES), 1)
    km0 = jnp.where(lane_b < HEAD_DIM, kmean, 0.0)
    km1 = jnp.where(lane_b >= HEAD_DIM, kmean, 0.0)
    zpad = jnp.zeros((LANES - HEAD_DIM - nb, LANES), f32)
    km_ref[0] = jnp.concatenate(
        [jnp.zeros((HEAD_DIM, LANES), f32), km0, zpad], axis=0).astype(bf)
    km_ref[1] = jnp.concatenate(
        [km1, jnp.zeros((LANES - nb, LANES), f32)], axis=0).astype(bf)

    lane_t = lax.broadcasted_iota(jnp.int32, (tq, LANES), 1)
    r_i = lax.broadcasted_iota(jnp.int32, (tq, kt), 0)
    c_i = lax.broadcasted_iota(jnp.int32, (tq, kt), 1)
    nt = (((1,), (1,)), ((), ()))
    heads = (0, 1)
    offs = (HEAD_DIM, 0)
    q_lanes = (lane_t < HEAD_DIM, lane_t >= HEAD_DIM)

    def q_tile(j, carry):
        r0 = pl.multiple_of(j * tq, tq)
        t_last = j // (kt // tq)
        k_last = pl.multiple_of(t_last * kt, kt)
        q2 = q_ref[pl.ds(r0, tq), :]
        q2f = q2.astype(f32)
        causal = (c_i + (k_last - r0)) <= r_i
        qas, ms, accs = [], [], []
        for h in heads:
            blk = lane_t - offs[h]
            in_blk = (blk >= 0) & (blk < nb)
            gate = lax.dot_general(q2, km_ref[h], nt, preferred_element_type=f32)
            valid = in_blk & (blk < j)
            g = jnp.where(valid, gate, NEG)
            sel = jnp.zeros((tq, LANES), jnp.bool_)
            for _ in range(MOBA_TOPK):
                pick = lane_t == jnp.argmax(g, axis=1, keepdims=True)
                sel = sel | pick
                g = jnp.where(pick, NEG, g)
            keep = (sel & valid) | (blk == j)
            pen = jnp.where(in_blk & jnp.logical_not(keep), NEG, 0.0)
            qa = jnp.where(q_lanes[h], q2f, pen).astype(bf)
            sc = lax.dot_general(qa, kaug_ref[h, pl.ds(k_last, kt), :], nt,
                                 preferred_element_type=f32)
            sc = jnp.where(causal, sc, NEG)
            m0 = jnp.max(sc, axis=1, keepdims=True)
            p = jnp.exp(sc - m0)
            acc0 = jnp.dot(p.astype(bf), vaug_ref[h, pl.ds(k_last, kt), :],
                           preferred_element_type=f32)
            qas.append(qa)
            ms.append(m0)
            accs.append(acc0)

        def kv_step(n, mc):
            k0 = pl.multiple_of(n * kt, kt)
            out = []
            for h in heads:
                m_old, acc = mc[2 * h], mc[2 * h + 1]
                sc = lax.dot_general(qas[h], kaug_ref[h, pl.ds(k0, kt), :], nt,
                                     preferred_element_type=f32)
                m_new = jnp.maximum(m_old, jnp.max(sc, axis=1, keepdims=True))
                alpha = jnp.exp(m_old - m_new)
                p = jnp.exp(sc - m_new)
                acc = acc * alpha + jnp.dot(p.astype(bf), vaug_ref[h, pl.ds(k0, kt), :],
                                            preferred_element_type=f32)
                out += [m_new, acc]
            return tuple(out)

        fin = lax.fori_loop(0, t_last, kv_step, (ms[0], accs[0], ms[1], accs[1]))
        o0 = fin[1] / pltpu.roll(fin[1], HEAD_DIM, axis=1)
        o1 = fin[3] / pltpu.roll(fin[3], HEAD_DIM, axis=1)
        o_ref[pl.ds(r0, tq), :] = jnp.where(q_lanes[0], o0, o1).astype(o_ref.dtype)
        return carry

    lax.fori_loop(0, nb, q_tile, 0)


def _moba(p3):
    b, s, _ = p3.shape
    n_pairs = MOBA_HEADS // 2
    blk = (None, s, LANES)
    return pl.pallas_call(
        _moba_kernel,
        grid=(b, n_pairs),
        in_specs=[pl.BlockSpec(blk, lambda i, j: (i, 0, j)),
                  pl.BlockSpec(blk, lambda i, j: (i, 0, n_pairs + j)),
                  pl.BlockSpec(blk, lambda i, j: (i, 0, 2 * n_pairs + j))],
        out_specs=pl.BlockSpec(blk, lambda i, j: (i, 0, j)),
        out_shape=jax.ShapeDtypeStruct((b, s, MOBA_W), jnp.bfloat16),
        scratch_shapes=[pltpu.VMEM((2, s, LANES), jnp.bfloat16),
                        pltpu.VMEM((2, s, LANES), jnp.bfloat16),
                        pltpu.VMEM((2, LANES, LANES), jnp.bfloat16)],
        compiler_params=_cparams(("arbitrary", "arbitrary")),
        name="moba_attn",
    )(p3, p3, p3)


def _dil_kernel(q_ref, k_ref, v_ref, o_ref, lse_ref):
    tq = q_ref.shape[0]
    sub = DIL_SPAN
    kw = 2 * sub
    base_tile = pl.program_id(2) * tq
    lane = lax.broadcasted_iota(jnp.int32, (sub, LANES), 1)
    first = lane < HEAD_DIM
    r_i = lax.broadcasted_iota(jnp.int32, (sub, kw), 0)
    c_i = lax.broadcasted_iota(jnp.int32, (sub, kw), 1)
    nt = (((1,), (1,)), ((), ()))

    def sub_block(sb, carry):
        q0 = pl.multiple_of(sb * sub, sub)
        base = base_tile + q0
        kstart = pl.multiple_of(jnp.maximum(base - sub, 0), sub)
        dist = (base + r_i) - (kstart + c_i)
        mask = (dist >= 0) & (dist <= DIL_SPAN)
        for pr in range(DIL_HEADS_PER_GROUP // 2):
            cs = slice(pr * LANES, (pr + 1) * LANES)
            q2 = q_ref[pl.ds(q0, sub), cs].astype(jnp.float32)
            k2 = k_ref[pl.ds(kstart, kw), cs]
            v2 = v_ref[pl.ds(kstart, kw), cs]
            outs, lses = [], []
            for h in range(2):
                qh = jnp.where(first if h == 0 else jnp.logical_not(first), q2, 0.0)
                qh = qh.astype(jnp.bfloat16)
                sc = lax.dot_general(qh, k2, nt, preferred_element_type=jnp.float32)
                sc = jnp.where(mask, sc, NEG)
                m = jnp.max(sc, axis=1, keepdims=True)
                p = jnp.exp(sc - m)
                l = jnp.sum(p, axis=1, keepdims=True)
                o = jnp.dot(p.astype(jnp.bfloat16), v2, preferred_element_type=jnp.float32)
                outs.append(o / l)
                lses.append(jnp.broadcast_to(m + jnp.log(l), (sub, LANES)))
            o_ref[pl.ds(q0, sub), cs] = jnp.where(first, outs[0], outs[1]).astype(o_ref.dtype)
            lse_ref[pl.ds(q0, sub), cs] = jnp.where(first, lses[0], lses[1])
        return carry

    lax.fori_loop(0, tq // sub, sub_block, 0, unroll=True)


def _dil_group(pg, b, dil):
    w = DIL_OUT_W
    l = pg.shape[0] // b
    pv = pg.reshape(b, l, dil * 3 * w)
    tq = min(DIL_TQ, l)
    o, lse = pl.pallas_call(
        _dil_kernel,
        grid=(b, dil, l // tq),
        in_specs=[pl.BlockSpec((None, tq, w), lambda i, r, t: (i, t, 3 * r)),
                  pl.BlockSpec((None, l, w), lambda i, r, t: (i, 0, 3 * r + 1)),
                  pl.BlockSpec((None, l, w), lambda i, r, t: (i, 0, 3 * r + 2))],
        out_specs=[pl.BlockSpec((None, tq, w), lambda i, r, t: (i, t, r)),
                   pl.BlockSpec((None, tq, w), lambda i, r, t: (i, t, r))],
        out_shape=[jax.ShapeDtypeStruct((b, l, dil * w), jnp.bfloat16),
                   jax.ShapeDtypeStruct((b, l, dil * w), jnp.float32)],
        compiler_params=_cparams(("arbitrary", "arbitrary", "arbitrary")),
        name=f"dil_attn_d{dil}",
    )(pv, pv, pv)
    return o.reshape(b, l * dil, w), lse.reshape(b, l * dil, w)


def _layer_norm(y, g, b):
    mu = jnp.mean(y, axis=-1, keepdims=True)
    yc = y - mu
    var = jnp.mean(yc * yc, axis=-1, keepdims=True)
    return yc * lax.rsqrt(var + LN_EPS) * g + b


def _merge_kernel(alpha, x_ref, ym_ref, bg_ref, cg_ref, hc_ref, cgp_ref, hcp_ref,
                  g0_ref, g1_ref, g2_ref, o1_ref, o2_ref, o3_ref, l1_ref, l2_ref, l3_ref,
                  wm_ref, wc_ref, wd_ref, wo_ref, wsc_ref, lng_ref, lnb_ref, out_ref):
    f32 = jnp.float32
    tm = x_ref.shape[0]
    ch = cg_ref[...].astype(f32) * hc_ref[...].astype(f32)
    chp = cgp_ref[...].astype(f32) * hcp_ref[...].astype(f32)
    chp = jnp.where(pl.program_id(1) == 0, 0.0, chp)
    hp = chp.shape[0]
    prev1 = chp[hp - 1:hp, :]
    prev2 = chp[hp - 2:hp - 1, :]
    row = lax.broadcasted_iota(jnp.int32, (tm, CONV_WIDTH), 0)
    ch1 = jnp.where(row == 0, prev1, pltpu.roll(ch, 1, axis=0))
    ch2 = pltpu.roll(ch, 2, axis=0)
    ch2 = jnp.where(row == 0, prev2, jnp.where(row == 1, prev1, ch2))
    wsc = wsc_ref[...]
    conv = ch * wsc[0:1, :] + ch1 * wsc[1:2, :] + ch2 * wsc[2:3, :]
    y_conv = (bg_ref[...].astype(f32) * conv).astype(jnp.bfloat16)

    l1, l2, l3 = l1_ref[...], l2_ref[...], l3_ref[...]
    lm = jnp.maximum(jnp.maximum(l1, l2), l3)
    e1, e2, e3 = jnp.exp(l1 - lm), jnp.exp(l2 - lm), jnp.exp(l3 - lm)
    den = e1 + e2 + e3
    y_dil = ((e1 / den) * o1_ref[...].astype(f32) + (e2 / den) * o2_ref[...].astype(f32)
             + (e3 / den) * o3_ref[...].astype(f32)).astype(jnp.bfloat16)

    pm = jnp.dot(ym_ref[...], wm_ref[...], preferred_element_type=f32)
    pc = jnp.dot(y_conv, wc_ref[...], preferred_element_type=f32)
    pd = jnp.dot(y_dil, wd_ref[...], preferred_element_type=f32)
    merged = (jax.nn.sigmoid(g0_ref[...].astype(f32)) * pm
              + jax.nn.sigmoid(g1_ref[...].astype(f32)) * pc
              + jax.nn.sigmoid(g2_ref[...].astype(f32)) * pd)
    mix = jnp.dot(merged.astype(jnp.bfloat16), wo_ref[...], preferred_element_type=f32)
    y = alpha * x_ref[...] + mix
    out_ref[...] = _layer_norm(y, lng_ref[...], lnb_ref[...])


def _merge(alpha, x3, ym, p3, dil_outs, wm, wc, wd, wo, wsc, lng, lnb):
    b, s, d = x3.shape
    tm = MERGE_TM
    hp = 16
    cw = CONV_WIDTH
    cblk = OFF_CONV // cw
    gblk = OFF_GATE // d
    row = lambda w: pl.BlockSpec((None, tm, w), lambda i, t: (i, t, 0))
    pcol = lambda w, c: pl.BlockSpec((None, tm, w), lambda i, t, c=c: (i, t, c))
    prev = lambda c: pl.BlockSpec(
        (None, hp, cw), lambda i, t, c=c: (i, jnp.maximum(t * (tm // hp) - 1, 0), c))
    full = lambda a: pl.BlockSpec(a.shape, lambda i, t: (0,) * a.ndim)
    (o1, l1), (o2, l2), (o3, l3) = dil_outs
    in_specs = [row(d), row(MOBA_W),
                pcol(cw, cblk), pcol(cw, cblk + 1), pcol(cw, cblk + 2),
                prev(cblk + 1), prev(cblk + 2),
                pcol(d, gblk), pcol(d, gblk + 1), pcol(d, gblk + 2),
                row(DIL_OUT_W), row(DIL_OUT_W), row(DIL_OUT_W),
                row(DIL_OUT_W), row(DIL_OUT_W), row(DIL_OUT_W),
                full(wm), full(wc), full(wd), full(wo), full(wsc), full(lng), full(lnb)]
    return pl.pallas_call(
        functools.partial(_merge_kernel, alpha),
        grid=(b, s // tm),
        in_specs=in_specs,
        out_specs=row(d),
        out_shape=jax.ShapeDtypeStruct((b, s, d), jnp.float32),
        compiler_params=_cparams(("arbitrary", "arbitrary")),
        name="merge",
    )(x3, ym, p3, p3, p3, p3, p3, p3, p3, p3, o1, o2, o3, l1, l2, l3,
      wm, wc, wd, wo, wsc, lng, lnb)


def _ffn_kernel(alpha, x_ref, xp_ref, wup_ref, wdn_ref, wcv_ref, bcv_ref, lng_ref, lnb_ref,
                out_ref, xs_ref, acc_ref):
    f32 = jnp.float32
    tm = x_ref.shape[0]
    halo = xp_ref.shape[0]
    xp = jnp.where(pl.program_id(1) == 0, 0.0, xp_ref[...])
    xs_ref[0:halo, :] = xp.astype(jnp.bfloat16)
    xs_ref[halo:halo + tm, :] = x_ref[...].astype(jnp.bfloat16)
    acc_ref[...] = jnp.zeros_like(acc_ref)

    def conv(u, w):
        y = u * w[0:1, :] + pltpu.roll(u, 1, axis=0) * w[1:2, :] + pltpu.roll(u, 2, axis=0) * w[2:3, :]
        return y[halo:, :]

    def chunk(c, carry):
        xs = xs_ref[...]
        ug = jnp.dot(xs, wup_ref[c], preferred_element_type=f32)
        uv = jnp.dot(xs, wup_ref[FFN_NCHUNK + c], preferred_element_type=f32)
        yg = conv(ug, wcv_ref[c]) + bcv_ref[c]
        yv = conv(uv, wcv_ref[FFN_NCHUNK + c]) + bcv_ref[FFN_NCHUNK + c]
        act = (yg * jax.nn.sigmoid(yg) * yv).astype(jnp.bfloat16)
        acc_ref[...] += jnp.dot(act, wdn_ref[c], preferred_element_type=f32)
        return carry

    lax.fori_loop(0, FFN_NCHUNK, chunk, 0, unroll=True)
    y = alpha * x_ref[...] + acc_ref[...]
    out_ref[...] = _layer_norm(y, lng_ref[...], lnb_ref[...])


def _ffn(alpha, x3, wup, wdn, wcv, bcv, lng, lnb):
    b, s, d = x3.shape
    tm = FFN_TM
    halo = FFN_HALO
    full = lambda a: pl.BlockSpec(a.shape, lambda i, t: (0,) * a.ndim)
    return pl.pallas_call(
        functools.partial(_ffn_kernel, alpha),
        grid=(b, s // tm),
        in_specs=[pl.BlockSpec((None, tm, d), lambda i, t: (i, t, 0)),
                  pl.BlockSpec((None, halo, d),
                               lambda i, t: (i, jnp.maximum(t * (tm // halo) - 1, 0), 0)),
                  full(wup), full(wdn), full(wcv), full(bcv), full(lng), full(lnb)],
        out_specs=pl.BlockSpec((None, tm, d), lambda i, t: (i, t, 0)),
        out_shape=jax.ShapeDtypeStruct((b, s, d), jnp.float32),
        scratch_shapes=[pltpu.VMEM((tm + halo, d), jnp.bfloat16),
                        pltpu.VMEM((tm, d), jnp.float32)],
        compiler_params=_cparams(("arbitrary", "arbitrary")),
        name="conv_ffn",
    )(x3, x3, wup, wdn, wcv, bcv, lng, lnb)


def _split_w_in(w):
    sizes = (MOBA_W, MOBA_W, MOBA_W, DIL_W, DIL_W, DIL_W,
             CONV_WIDTH, CONV_WIDTH, CONV_WIDTH, N_BRANCH * D_MODEL)
    pts = np.cumsum((0,) + sizes)
    parts = [w[:, int(pts[i]):int(pts[i + 1])] for i in range(len(sizes))]
    qa, ka, va, qd, kd, vd, bg, cg, hc, gl = parts
    scale = HEAD_DIM ** -0.5
    w_main = jnp.concatenate([qa * scale, ka, va, bg, cg, hc, gl], axis=1)
    w_dil = jnp.concatenate([qd * scale, kd, vd], axis=1)
    return w_main.astype(jnp.bfloat16), w_dil.astype(jnp.bfloat16)


def kernel(x, w_in, w_short_conv, w_moba_proj, w_dil_proj, w_conv_proj, w_mix_out,
           ln1_g, ln1_b, w_up, w_ffn_conv, b_ffn_conv, w_down, ln2_g, ln2_b):
    b, s, d = x.shape
    depth = w_in.shape[0]
    alpha = float((2 * depth) ** 0.25)
    bf = jnp.bfloat16
    for l in range(depth):
        w_main, w_dil = _split_w_in(w_in[l])
        x2 = x.reshape(b * s, d)
        p = _in_proj(x2, w_main).reshape(b, s, MAIN_COLS)
        pgs = _dil_proj(x2, w_dil)
        ym = _moba(p)
        dil_outs = [_dil_group(pg, b, dil) for pg, (_, dil) in zip(pgs, DIL_GROUPS)]
        x = _merge(alpha, x, ym, p, dil_outs,
                   w_moba_proj[l].astype(bf), w_conv_proj[l].astype(bf),
                   w_dil_proj[l].astype(bf), w_mix_out[l].astype(bf),
                   w_short_conv[l], ln1_g[l].reshape(1, d), ln1_b[l].reshape(1, d))
        wup = w_up[l].astype(bf).reshape(d, 2 * FFN_NCHUNK, FFN_CHUNK).transpose(1, 0, 2)
        wdn = w_down[l].astype(bf).reshape(FFN_NCHUNK, FFN_CHUNK, d)
        wcv = w_ffn_conv[l].reshape(3, 2 * FFN_NCHUNK, FFN_CHUNK).transpose(1, 0, 2)
        bcv = b_ffn_conv[l].reshape(2 * FFN_NCHUNK, 1, FFN_CHUNK)
        x = _ffn(alpha, x, wup, wdn, wcv, bcv, ln2_g[l].reshape(1, d), ln2_b[l].reshape(1, d))
    return x
```

```python
import functools

import jax
import jax.numpy as jnp
import numpy as np
from jax import lax
from jax.experimental import pallas as pl
from jax.experimental.pallas import tpu as pltpu

D_MODEL = 1024
HEAD_DIM = 64
MOBA_HEADS = 8
MOBA_BLOCK = 256
MOBA_TOPK = 3
MOBA_W = MOBA_HEADS * HEAD_DIM
DIL_GROUPS = ((128, 1), (512, 4), (2048, 16))
DIL_HEADS_PER_GROUP = 4
DIL_W = 3 * DIL_HEADS_PER_GROUP * HEAD_DIM
DIL_OUT_W = DIL_HEADS_PER_GROUP * HEAD_DIM
CONV_WIDTH = 512
N_BRANCH = 3
D_FF = 2816
LN_EPS = 1e-5

OFF_CONV = 3 * MOBA_W
OFF_GATE = OFF_CONV + 3 * CONV_WIDTH
MAIN_COLS = OFF_GATE + N_BRANCH * D_MODEL

LANES = 128
NEG = -1e30
VMEM_LIMIT = 56 * 1024 * 1024

PROJ_TM = 1024
PROJ_TN = 1536
DILP_TM = 512
MOBA_KTILE = 2 * MOBA_BLOCK
DIL_TQ = 512
DIL_SPAN = 128
MERGE_TM = 512
FFN_TM = 512
FFN_HALO = 16
FFN_CHUNK = 256
FFN_NCHUNK = D_FF // FFN_CHUNK


def _cparams(sem):
    return pltpu.CompilerParams(dimension_semantics=sem, vmem_limit_bytes=VMEM_LIMIT)


def _proj_kernel(x_ref, w_ref, o_ref, xb_ref):
    @pl.when(pl.program_id(1) == 0)
    def _():
        xb_ref[...] = x_ref[...].astype(jnp.bfloat16)

    o_ref[...] = jnp.dot(xb_ref[...], w_ref[...],
                         preferred_element_type=jnp.float32).astype(o_ref.dtype)


def _in_proj(x2, w):
    n, d = x2.shape
    cols = w.shape[1]
    return pl.pallas_call(
        _proj_kernel,
        grid=(n // PROJ_TM, cols // PROJ_TN),
        in_specs=[pl.BlockSpec((PROJ_TM, d), lambda i, j: (i, 0)),
                  pl.BlockSpec((d, PROJ_TN), lambda i, j: (0, j))],
        out_specs=pl.BlockSpec((PROJ_TM, PROJ_TN), lambda i, j: (i, j)),
        out_shape=jax.ShapeDtypeStruct((n, cols), jnp.bfloat16),
        scratch_shapes=[pltpu.VMEM((PROJ_TM, d), jnp.bfloat16)],
        compiler_params=_cparams(("arbitrary", "arbitrary")),
        name="in_proj",
    )(x2, w)


def _dil_proj_kernel(x_ref, w_ref, o0_ref, o1_ref, o2_ref, res_ref):
    tm = x_ref.shape[0]
    res = jnp.dot(x_ref[...].astype(jnp.bfloat16), w_ref[...],
                  preferred_element_type=jnp.float32)
    for cb in range(res_ref.shape[0]):
        res_ref[cb] = res[:, cb * LANES:(cb + 1) * LANES]
    slabs = DIL_OUT_W // LANES
    for g, o_ref in enumerate((o0_ref, o1_ref, o2_ref)):
        d = DIL_GROUPS[g][1]
        rows = tm // d
        for r in range(d):
            for j in range(3):
                for sl in range(slabs):
                    src_cb = (j * DIL_W + g * DIL_OUT_W) // LANES + sl
                    dst_c = (r * 3 + j) * DIL_OUT_W + sl * LANES
                    if d == 1:
                        src = res_ref[src_cb]
                    else:
                        src = res_ref[src_cb, pl.ds(r, rows, stride=d), :]
                    o_ref[:, dst_c:dst_c + LANES] = src.astype(o_ref.dtype)


def _dil_proj(x2, w):
    n, d = x2.shape
    tm = DILP_TM
    dils = [dil for _, dil in DIL_GROUPS]
    return pl.pallas_call(
        _dil_proj_kernel,
        grid=(n // tm,),
        in_specs=[pl.BlockSpec((tm, d), lambda i: (i, 0)),
                  pl.BlockSpec(w.shape, lambda i: (0, 0))],
        out_specs=[pl.BlockSpec((tm // dil, dil * 3 * DIL_OUT_W), lambda i: (i, 0))
                   for dil in dils],
        out_shape=[jax.ShapeDtypeStruct((n // dil, dil * 3 * DIL_OUT_W), jnp.bfloat16)
                   for dil in dils],
        scratch_shapes=[pltpu.VMEM((3 * DIL_W // LANES, tm, LANES), jnp.float32)],
        compiler_params=_cparams(("arbitrary",)),
        name="dil_proj",
    )(x2, w)


def _moba_kernel(q_ref, k_ref, v_ref, o_ref, kaug_ref, vaug_ref, km_ref, sc_ref):
    s = q_ref.shape[0]
    nb = s // MOBA_BLOCK
    tq = MOBA_BLOCK
    kt = MOBA_KTILE
    f32 = jnp.float32
    bf = jnp.bfloat16
    lane_s = lax.broadcasted_iota(jnp.int32, (s, LANES), 1)
    row_blk = lax.broadcasted_iota(jnp.int32, (s, LANES), 0) // MOBA_BLOCK
    k2 = k_ref[...].astype(f32)
    v2 = v_ref[...].astype(f32)
    kaug_ref[0] = jnp.where(lane_s < HEAD_DIM, k2,
                            jnp.where(lane_s - HEAD_DIM == row_blk, 1.0, 0.0)).astype(bf)
    kaug_ref[1] = jnp.where(lane_s >= HEAD_DIM, k2,
                            jnp.where(lane_s == row_blk, 1.0, 0.0)).astype(bf)
    vaug_ref[0] = jnp.where(lane_s < HEAD_DIM, v2, 1.0).astype(bf)
    vaug_ref[1] = jnp.where(lane_s >= HEAD_DIM, v2, 1.0).astype(bf)

    offs = (HEAD_DIM, 0)
    ksum = jnp.sum(k2.reshape(nb, MOBA_BLOCK, LANES), axis=1)
    kmean = ksum * (1.0 / MOBA_BLOCK)
    lane_b = lax.broadcasted_iota(jnp.int32, (nb, LANES), 1)
    km0 = jnp.where(lane_b < HEAD_DIM, kmean, 0.0)
    km1 = jnp.where(lane_b >= HEAD_DIM, kmean, 0.0)
    zgap = jnp.zeros((HEAD_DIM - nb, LANES), f32)
    km_ref[...] = jnp.concatenate([km1, zgap, km0, zgap], axis=0).astype(bf)

    lane_t = lax.broadcasted_iota(jnp.int32, (tq, LANES), 1)
    r_i = lax.broadcasted_iota(jnp.int32, (tq, kt), 0)
    c_i = lax.broadcasted_iota(jnp.int32, (tq, kt), 1)
    blk_i = lax.broadcasted_iota(jnp.int32, (nb, tq), 0)
    blk_f = blk_i.astype(f32)
    zgap_t = jnp.zeros((HEAD_DIM - nb, tq), f32)
    nt = (((1,), (1,)), ((), ()))
    heads = (0, 1)
    q_lanes = (lane_t < HEAD_DIM, lane_t >= HEAD_DIM)

    def q_tile(j, carry):
        r0 = pl.multiple_of(j * tq, tq)
        t_last = j // (kt // tq)
        k_last = pl.multiple_of(t_last * kt, kt)
        q2 = q_ref[pl.ds(r0, tq), :]
        q2f = q2.astype(f32)
        gate_t = lax.dot_general(km_ref[...], q2, nt, preferred_element_type=f32)
        valid = blk_i < j
        pens = []
        for h in heads:
            g = jnp.where(valid, gate_t[offs[h]:offs[h] + nb, :], NEG)
            sel = jnp.zeros((nb, tq), jnp.bool_)
            for _ in range(MOBA_TOPK):
                m = jnp.max(g, axis=0, keepdims=True)
                first = jnp.min(jnp.where(g == m, blk_f, 1e9), axis=0, keepdims=True)
                pick = blk_f == first
                sel = sel | pick
                g = jnp.where(pick, NEG, g)
            keep = (sel & valid) | (blk_i == j)
            pens.append(jnp.where(keep, 0.0, NEG))
        pen = jnp.concatenate([pens[1], zgap_t, pens[0], zgap_t], axis=0).T
        causal = (c_i + (k_last - r0)) <= r_i
        qas = []
        for h in heads:
            qa = jnp.where(q_lanes[h], q2f, pen).astype(bf)
            sc = lax.dot_general(qa, kaug_ref[h, pl.ds(k_last, kt), :], nt,
                                 preferred_element_type=f32)
            sc_ref[h] = jnp.where(causal, sc, NEG)
            qas.append(qa)

        def kv_step(i, mc):
            cur = jnp.where(i == 0, t_last, i - 1)
            v0 = pl.multiple_of(cur * kt, kt)
            nxt = jnp.clip(i, 0, jnp.maximum(t_last - 1, 0))
            k0 = pl.multiple_of(nxt * kt, kt)
            out = []
            for h in heads:
                m_old, acc = mc[2 * h], mc[2 * h + 1]
                sc = sc_ref[h]
                sc_next = lax.dot_general(qas[h], kaug_ref[h, pl.ds(k0, kt), :], nt,
                                          preferred_element_type=f32)
                m_new = jnp.maximum(m_old, jnp.max(sc, axis=1, keepdims=True))
                alpha = jnp.exp(m_old - m_new)
                p = jnp.exp(sc - m_new)
                acc = acc * alpha + jnp.dot(p.astype(bf), vaug_ref[h, pl.ds(v0, kt), :],
                                            preferred_element_type=f32)
                sc_ref[h] = sc_next
                out += [m_new, acc]
            return tuple(out)

        m_init = jnp.full((tq, 1), NEG, f32)
        acc_init = jnp.zeros((tq, LANES), f32)
        fin = lax.fori_loop(0, t_last + 1, kv_step, (m_init, acc_init, m_init, acc_init))
        o0 = fin[1] / pltpu.roll(fin[1], HEAD_DIM, axis=1)
        o1 = fin[3] / pltpu.roll(fin[3], HEAD_DIM, axis=1)
        o_ref[pl.ds(r0, tq), :] = jnp.where(q_lanes[0], o0, o1).astype(o_ref.dtype)
        return carry

    lax.fori_loop(0, nb, q_tile, 0)


def _moba(p3):
    b, s, _ = p3.shape
    n_pairs = MOBA_HEADS // 2
    blk = (None, s, LANES)
    return pl.pallas_call(
        _moba_kernel,
        grid=(b, n_pairs),
        in_specs=[pl.BlockSpec(blk, lambda i, j: (i, 0, j)),
                  pl.BlockSpec(blk, lambda i, j: (i, 0, n_pairs + j)),
                  pl.BlockSpec(blk, lambda i, j: (i, 0, 2 * n_pairs + j))],
        out_specs=pl.BlockSpec(blk, lambda i, j: (i, 0, j)),
        out_shape=jax.ShapeDtypeStruct((b, s, MOBA_W), jnp.bfloat16),
        scratch_shapes=[pltpu.VMEM((2, s, LANES), jnp.bfloat16),
                        pltpu.VMEM((2, s, LANES), jnp.bfloat16),
                        pltpu.VMEM((LANES, LANES), jnp.bfloat16),
                        pltpu.VMEM((2, MOBA_BLOCK, MOBA_KTILE), jnp.float32)],
        compiler_params=_cparams(("arbitrary", "arbitrary")),
        name="moba_attn",
    )(p3, p3, p3)


def _dil_kernel(q_ref, k_ref, v_ref, o_ref, lse_ref):
    tq = q_ref.shape[0]
    sub = DIL_SPAN
    kw = 2 * sub
    base_tile = pl.program_id(2) * tq
    lane = lax.broadcasted_iota(jnp.int32, (sub, LANES), 1)
    first = lane < HEAD_DIM
    r_i = lax.broadcasted_iota(jnp.int32, (sub, kw), 0)
    c_i = lax.broadcasted_iota(jnp.int32, (sub, kw), 1)
    nt = (((1,), (1,)), ((), ()))

    def sub_block(sb, carry):
        q0 = pl.multiple_of(sb * sub, sub)
        base = base_tile + q0
        kstart = pl.multiple_of(jnp.maximum(base - sub, 0), sub)
        dist = (base + r_i) - (kstart + c_i)
        mask = (dist >= 0) & (dist <= DIL_SPAN)
        for pr in range(DIL_HEADS_PER_GROUP // 2):
            cs = slice(pr * LANES, (pr + 1) * LANES)
            q2 = q_ref[pl.ds(q0, sub), cs].astype(jnp.float32)
            k2 = k_ref[pl.ds(kstart, kw), cs]
            v2 = v_ref[pl.ds(kstart, kw), cs]
            outs, lses = [], []
            for h in range(2):
                qh = jnp.where(first if h == 0 else jnp.logical_not(first), q2, 0.0)
                qh = qh.astype(jnp.bfloat16)
                sc = lax.dot_general(qh, k2, nt, preferred_element_type=jnp.float32)
                sc = jnp.where(mask, sc, NEG)
                m = jnp.max(sc, axis=1, keepdims=True)
                p = jnp.exp(sc - m)
                l = jnp.sum(p, axis=1, keepdims=True)
                o = jnp.dot(p.astype(jnp.bfloat16), v2, preferred_element_type=jnp.float32)
                outs.append(o / l)
                lses.append(jnp.broadcast_to(m + jnp.log(l), (sub, LANES)))
            o_ref[pl.ds(q0, sub), cs] = jnp.where(first, outs[0], outs[1]).astype(o_ref.dtype)
            lse_ref[pl.ds(q0, sub), cs] = jnp.where(first, lses[0], lses[1])
        return carry

    lax.fori_loop(0, tq // sub, sub_block, 0, unroll=True)


def _dil_group(pg, b, dil):
    w = DIL_OUT_W
    l = pg.shape[0] // b
    pv = pg.reshape(b, l, dil * 3 * w)
    tq = min(DIL_TQ, l)
    o, lse = pl.pallas_call(
        _dil_kernel,
        grid=(b, dil, l // tq),
        in_specs=[pl.BlockSpec((None, tq, w), lambda i, r, t: (i, t, 3 * r)),
                  pl.BlockSpec((None, l, w), lambda i, r, t: (i, 0, 3 * r + 1)),
                  pl.BlockSpec((None, l, w), lambda i, r, t: (i, 0, 3 * r + 2))],
        out_specs=[pl.BlockSpec((None, tq, w), lambda i, r, t: (i, t, r)),
                   pl.BlockSpec((None, tq, w), lambda i, r, t: (i, t, r))],
        out_shape=[jax.ShapeDtypeStruct((b, l, dil * w), jnp.bfloat16),
                   jax.ShapeDtypeStruct((b, l, dil * w), jnp.float32)],
        compiler_params=_cparams(("arbitrary", "arbitrary", "arbitrary")),
        name=f"dil_attn_d{dil}",
    )(pv, pv, pv)
    return o, lse


def _layer_norm(y, g, b):
    mu = jnp.mean(y, axis=-1, keepdims=True)
    yc = y - mu
    var = jnp.mean(yc * yc, axis=-1, keepdims=True)
    return yc * lax.rsqrt(var + LN_EPS) * g + b


def _sigmoid(z):
    return 0.5 * jnp.tanh(0.5 * z) + 0.5


def _natural_order(src_ref, dst_ref, a, dil):
    rows = src_ref.shape[0]
    slabs = DIL_OUT_W // LANES
    for r in range(dil):
        for sl in range(slabs):
            c = r * DIL_OUT_W + sl * LANES
            dst_ref[a, sl, pl.ds(r, rows, stride=dil), :] = (
                src_ref[:, c:c + LANES].astype(jnp.float32))
    return jnp.concatenate([dst_ref[a, sl] for sl in range(slabs)], axis=1)


def _merge_kernel(alpha, x_ref, ym_ref, bg_ref, cg_ref, hc_ref, cgp_ref, hcp_ref,
                  g0_ref, g1_ref, g2_ref, o1_ref, o2_ref, o3_ref, l1_ref, l2_ref, l3_ref,
                  wm_ref, wc_ref, wd_ref, wo_ref, wsc_ref, lng_ref, lnb_ref, out_ref, nat_ref):
    f32 = jnp.float32
    tm = x_ref.shape[0]
    ch = cg_ref[...].astype(f32) * hc_ref[...].astype(f32)
    chp = cgp_ref[...].astype(f32) * hcp_ref[...].astype(f32)
    chp = jnp.where(pl.program_id(1) == 0, 0.0, chp)
    hp = chp.shape[0]
    prev1 = chp[hp - 1:hp, :]
    prev2 = chp[hp - 2:hp - 1, :]
    row = lax.broadcasted_iota(jnp.int32, (tm, CONV_WIDTH), 0)
    ch1 = jnp.where(row == 0, prev1, pltpu.roll(ch, 1, axis=0))
    ch2 = pltpu.roll(ch, 2, axis=0)
    ch2 = jnp.where(row == 0, prev2, jnp.where(row == 1, prev1, ch2))
    wsc = wsc_ref[...]
    conv = ch * wsc[0:1, :] + ch1 * wsc[1:2, :] + ch2 * wsc[2:3, :]
    y_conv = (bg_ref[...].astype(f32) * conv).astype(jnp.bfloat16)

    d2, d3 = DIL_GROUPS[1][1], DIL_GROUPS[2][1]
    o1, l1 = o1_ref[...].astype(f32), l1_ref[...]
    o2, l2 = _natural_order(o2_ref, nat_ref, 0, d2), _natural_order(l2_ref, nat_ref, 1, d2)
    o3, l3 = _natural_order(o3_ref, nat_ref, 2, d3), _natural_order(l3_ref, nat_ref, 3, d3)
    lm = jnp.maximum(jnp.maximum(l1, l2), l3)
    e1, e2, e3 = jnp.exp(l1 - lm), jnp.exp(l2 - lm), jnp.exp(l3 - lm)
    den = e1 + e2 + e3
    y_dil = ((e1 / den) * o1 + (e2 / den) * o2 + (e3 / den) * o3).astype(jnp.bfloat16)

    pm = jnp.dot(ym_ref[...], wm_ref[...], preferred_element_type=f32)
    pc = jnp.dot(y_conv, wc_ref[...], preferred_element_type=f32)
    pd = jnp.dot(y_dil, wd_ref[...], preferred_element_type=f32)
    merged = (_sigmoid(g0_ref[...].astype(f32)) * pm
              + _sigmoid(g1_ref[...].astype(f32)) * pc
              + _sigmoid(g2_ref[...].astype(f32)) * pd)
    mix = jnp.dot(merged.astype(jnp.bfloat16), wo_ref[...], preferred_element_type=f32)
    y = alpha * x_ref[...] + mix
    out_ref[...] = _layer_norm(y, lng_ref[...], lnb_ref[...])


def _merge(alpha, x3, ym, p3, dil_outs, wm, wc, wd, wo, wsc, lng, lnb):
    b, s, d = x3.shape
    tm = MERGE_TM
    hp = 16
    cw = CONV_WIDTH
    cblk = OFF_CONV // cw
    gblk = OFF_GATE // d
    row = lambda w: pl.BlockSpec((None, tm, w), lambda i, t: (i, t, 0))
    pcol = lambda w, c: pl.BlockSpec((None, tm, w), lambda i, t, c=c: (i, t, c))
    prev = lambda c: pl.BlockSpec(
        (None, hp, cw), lambda i, t, c=c: (i, jnp.maximum(t * (tm // hp) - 1, 0), c))
    full = lambda a: pl.BlockSpec(a.shape, lambda i, t: (0,) * a.ndim)
    (o1, l1), (o2, l2), (o3, l3) = dil_outs
    dils = [dil for _, dil in DIL_GROUPS]
    grp = lambda dil: pl.BlockSpec((None, tm // dil, dil * DIL_OUT_W), lambda i, t: (i, t, 0))
    in_specs = [row(d), row(MOBA_W),
                pcol(cw, cblk), pcol(cw, cblk + 1), pcol(cw, cblk + 2),
                prev(cblk + 1), prev(cblk + 2),
                pcol(d, gblk), pcol(d, gblk + 1), pcol(d, gblk + 2),
                grp(dils[0]), grp(dils[1]), grp(dils[2]),
                grp(dils[0]), grp(dils[1]), grp(dils[2]),
                full(wm), full(wc), full(wd), full(wo), full(wsc), full(lng), full(lnb)]
    return pl.pallas_call(
        functools.partial(_merge_kernel, alpha),
        grid=(b, s // tm),
        in_specs=in_specs,
        out_specs=row(d),
        out_shape=jax.ShapeDtypeStruct((b, s, d), jnp.float32),
        scratch_shapes=[pltpu.VMEM((4, DIL_OUT_W // LANES, tm, LANES), jnp.float32)],
        compiler_params=_cparams(("arbitrary", "arbitrary")),
        name="merge",
    )(x3, ym, p3, p3, p3, p3, p3, p3, p3, p3, o1, o2, o3, l1, l2, l3,
      wm, wc, wd, wo, wsc, lng, lnb)


def _ffn_kernel(alpha, x_ref, xp_ref, wup_ref, wdn_ref, wcv_ref, bcv_ref, lng_ref, lnb_ref,
                out_ref, xs_ref, act_ref):
    f32 = jnp.float32
    tm = x_ref.shape[0]
    halo = xp_ref.shape[0]
    xp = jnp.where(pl.program_id(1) == 0, 0.0, xp_ref[...])
    xs_ref[0:halo, :] = xp.astype(jnp.bfloat16)
    xs_ref[halo:halo + tm, :] = x_ref[...].astype(jnp.bfloat16)

    def conv(u, w):
        y = u * w[0:1, :] + pltpu.roll(u, 1, axis=0) * w[1:2, :] + pltpu.roll(u, 2, axis=0) * w[2:3, :]
        return y[halo:, :]

    for c in range(FFN_NCHUNK):
        xs = xs_ref[...]
        ug = jnp.dot(xs, wup_ref[c], preferred_element_type=f32)
        uv = jnp.dot(xs, wup_ref[FFN_NCHUNK + c], preferred_element_type=f32)
        yg = conv(ug, wcv_ref[c]) + bcv_ref[c]
        yv = conv(uv, wcv_ref[FFN_NCHUNK + c]) + bcv_ref[FFN_NCHUNK + c]
        act_ref[:, c * FFN_CHUNK:(c + 1) * FFN_CHUNK] = (
            yg * _sigmoid(yg) * yv).astype(jnp.bfloat16)

    ffn = jnp.dot(act_ref[...], wdn_ref[...], preferred_element_type=f32)
    y = alpha * x_ref[...] + ffn
    out_ref[...] = _layer_norm(y, lng_ref[...], lnb_ref[...])


def _ffn(alpha, x3, wup, wdn, wcv, bcv, lng, lnb):
    b, s, d = x3.shape
    tm = FFN_TM
    halo = FFN_HALO
    full = lambda a: pl.BlockSpec(a.shape, lambda i, t: (0,) * a.ndim)
    return pl.pallas_call(
        functools.partial(_ffn_kernel, alpha),
        grid=(b, s // tm),
        in_specs=[pl.BlockSpec((None, tm, d), lambda i, t: (i, t, 0)),
                  pl.BlockSpec((None, halo, d),
                               lambda i, t: (i, jnp.maximum(t * (tm // halo) - 1, 0), 0)),
                  full(wup), full(wdn), full(wcv), full(bcv), full(lng), full(lnb)],
        out_specs=pl.BlockSpec((None, tm, d), lambda i, t: (i, t, 0)),
        out_shape=jax.ShapeDtypeStruct((b, s, d), jnp.float32),
        scratch_shapes=[pltpu.VMEM((tm + halo, d), jnp.bfloat16),
                        pltpu.VMEM((tm, D_FF), jnp.bfloat16)],
        compiler_params=_cparams(("arbitrary", "arbitrary")),
        name="conv_ffn",
    )(x3, x3, wup, wdn, wcv, bcv, lng, lnb)


def _split_w_in(w):
    sizes = (MOBA_W, MOBA_W, MOBA_W, DIL_W, DIL_W, DIL_W,
             CONV_WIDTH, CONV_WIDTH, CONV_WIDTH, N_BRANCH * D_MODEL)
    pts = np.cumsum((0,) + sizes)
    parts = [w[:, int(pts[i]):int(pts[i + 1])] for i in range(len(sizes))]
    qa, ka, va, qd, kd, vd, bg, cg, hc, gl = parts
    scale = HEAD_DIM ** -0.5
    w_main = jnp.concatenate([qa * scale, ka, va, bg, cg, hc, gl], axis=1)
    w_dil = jnp.concatenate([qd * scale, kd, vd], axis=1)
    return w_main.astype(jnp.bfloat16), w_dil.astype(jnp.bfloat16)


def kernel(x, w_in, w_short_conv, w_moba_proj, w_dil_proj, w_conv_proj, w_mix_out,
           ln1_g, ln1_b, w_up, w_ffn_conv, b_ffn_conv, w_down, ln2_g, ln2_b):
    b, s, d = x.shape
    depth = w_in.shape[0]
    alpha = float((2 * depth) ** 0.25)
    bf = jnp.bfloat16
    for l in range(depth):
        w_main, w_dil = _split_w_in(w_in[l])
        x2 = x.reshape(b * s, d)
        p = _in_proj(x2, w_main).reshape(b, s, MAIN_COLS)
        pgs = _dil_proj(x2, w_dil)
        ym = _moba(p)
        dil_outs = [_dil_group(pg, b, dil) for pg, (_, dil) in zip(pgs, DIL_GROUPS)]
        x = _merge(alpha, x, ym, p, dil_outs,
                   w_moba_proj[l].astype(bf), w_conv_proj[l].astype(bf),
                   w_dil_proj[l].astype(bf), w_mix_out[l].astype(bf),
                   w_short_conv[l], ln1_g[l].reshape(1, d), ln1_b[l].reshape(1, d))
        wup = w_up[l].astype(bf).reshape(d, 2 * FFN_NCHUNK, FFN_CHUNK).transpose(1, 0, 2)
        wdn = w_down[l].astype(bf)
        wcv = w_ffn_conv[l].reshape(3, 2 * FFN_NCHUNK, FFN_CHUNK).transpose(1, 0, 2)
        bcv = b_ffn_conv[l].reshape(2 * FFN_NCHUNK, 1, FFN_CHUNK)
        x = _ffn(alpha, x, wup, wdn, wcv, bcv, ln2_g[l].reshape(1, d), ln2_b[l].reshape(1, d))
    return x
```

```python
import functools

import jax
import jax.numpy as jnp
import numpy as np
from jax import lax
from jax.experimental import pallas as pl
from jax.experimental.pallas import tpu as pltpu

D_MODEL = 1024
HEAD_DIM = 64
MOBA_HEADS = 8
MOBA_BLOCK = 256
MOBA_TOPK = 3
MOBA_W = MOBA_HEADS * HEAD_DIM
DIL_GROUPS = ((128, 1), (512, 4), (2048, 16))
DIL_HEADS_PER_GROUP = 4
DIL_W = 3 * DIL_HEADS_PER_GROUP * HEAD_DIM
DIL_OUT_W = DIL_HEADS_PER_GROUP * HEAD_DIM
CONV_WIDTH = 512
N_BRANCH = 3
D_FF = 2816
LN_EPS = 1e-5

OFF_CONV = 3 * MOBA_W
OFF_GATE = OFF_CONV + 3 * CONV_WIDTH
MAIN_COLS = OFF_GATE + N_BRANCH * D_MODEL

LANES = 128
NEG = -1e30
VMEM_LIMIT = 56 * 1024 * 1024

PROJ_TM = 1024
PROJ_TN = 768
DILP_TM = 512
MOBA_KTILE = 2 * MOBA_BLOCK
DIL_TQ = 512
DIL_SPAN = 128
MERGE_TM = 512
FFN_TM = 512
FFN_HALO = 16
FFN_CHUNK = 256
FFN_NCHUNK = D_FF // FFN_CHUNK


def _cparams(sem):
    return pltpu.CompilerParams(dimension_semantics=sem, vmem_limit_bytes=VMEM_LIMIT)


def _proj_kernel(x_ref, w_ref, cs_ref, o_ref, xb_ref):
    @pl.when(pl.program_id(1) == 0)
    def _():
        xb_ref[...] = x_ref[...].astype(jnp.bfloat16)

    acc = jnp.dot(xb_ref[...], w_ref[...].astype(jnp.bfloat16),
                  preferred_element_type=jnp.float32)
    o_ref[...] = (acc * cs_ref[...]).astype(o_ref.dtype)


def _q_scale_row(q_cols, cols):
    cs = np.ones((1, cols), np.float32)
    cs[:, :q_cols] = HEAD_DIM ** -0.5
    return jnp.asarray(cs)


def _in_proj(x2, w_in, layer):
    n, d = x2.shape
    tn = PROJ_TN
    skip_at = 3 * MOBA_W // tn
    skip = 3 * DIL_W // tn
    src = lambda j: jnp.where(j < skip_at, j, j + skip)
    return pl.pallas_call(
        _proj_kernel,
        grid=(n // PROJ_TM, MAIN_COLS // tn),
        in_specs=[pl.BlockSpec((PROJ_TM, d), lambda i, j: (i, 0)),
                  pl.BlockSpec((None, d, tn), lambda i, j: (layer, 0, src(j))),
                  pl.BlockSpec((1, tn), lambda i, j: (0, j))],
        out_specs=pl.BlockSpec((PROJ_TM, tn), lambda i, j: (i, j)),
        out_shape=jax.ShapeDtypeStruct((n, MAIN_COLS), jnp.bfloat16),
        scratch_shapes=[pltpu.VMEM((PROJ_TM, d), jnp.bfloat16)],
        compiler_params=_cparams(("arbitrary", "arbitrary")),
        name="in_proj",
    )(x2, w_in, _q_scale_row(MOBA_W, MAIN_COLS))


def _dil_proj_kernel(x_ref, wq_ref, wk_ref, wv_ref, o0_ref, o1_ref, o2_ref, wb_ref, res_ref):
    tm = x_ref.shape[0]

    @pl.when(pl.program_id(0) == 0)
    def _():
        for j, w_ref in enumerate((wq_ref, wk_ref, wv_ref)):
            wb_ref[j] = w_ref[...].astype(jnp.bfloat16)

    xb = x_ref[...].astype(jnp.bfloat16)
    per_part = DIL_W // LANES
    for j in range(3):
        res = jnp.dot(xb, wb_ref[j], preferred_element_type=jnp.float32)
        if j == 0:
            res = res * (HEAD_DIM ** -0.5)
        for cb in range(per_part):
            res_ref[j * per_part + cb] = res[:, cb * LANES:(cb + 1) * LANES]
    slabs = DIL_OUT_W // LANES
    for g, o_ref in enumerate((o0_ref, o1_ref, o2_ref)):
        d = DIL_GROUPS[g][1]
        rows = tm // d
        for r in range(d):
            for j in range(3):
                for sl in range(slabs):
                    src_cb = (j * DIL_W + g * DIL_OUT_W) // LANES + sl
                    dst_c = (r * 3 + j) * DIL_OUT_W + sl * LANES
                    if d == 1:
                        src = res_ref[src_cb]
                    else:
                        src = res_ref[src_cb, pl.ds(r, rows, stride=d), :]
                    o_ref[:, dst_c:dst_c + LANES] = src.astype(o_ref.dtype)


def _dil_proj(x2, w_in, layer):
    n, d = x2.shape
    tm = DILP_TM
    dils = [dil for _, dil in DIL_GROUPS]
    first = 3 * MOBA_W // DIL_W
    wspec = lambda j: pl.BlockSpec((None, d, DIL_W), lambda i, j=j: (layer, 0, first + j))
    return pl.pallas_call(
        _dil_proj_kernel,
        grid=(n // tm,),
        in_specs=[pl.BlockSpec((tm, d), lambda i: (i, 0)), wspec(0), wspec(1), wspec(2)],
        out_specs=[pl.BlockSpec((tm // dil, dil * 3 * DIL_OUT_W), lambda i: (i, 0))
                   for dil in dils],
        out_shape=[jax.ShapeDtypeStruct((n // dil, dil * 3 * DIL_OUT_W), jnp.bfloat16)
                   for dil in dils],
        scratch_shapes=[pltpu.VMEM((3, d, DIL_W), jnp.bfloat16),
                        pltpu.VMEM((3 * DIL_W // LANES, tm, LANES), jnp.float32)],
        compiler_params=_cparams(("arbitrary",)),
        name="dil_proj",
    )(x2, w_in, w_in, w_in)


def _moba_kernel(q_ref, k_ref, v_ref, o_ref, kaug_ref, vaug_ref, km_ref, sc_ref, p_ref):
    s = q_ref.shape[0]
    nb = s // MOBA_BLOCK
    tq = kt = MOBA_KTILE
    f32 = jnp.float32
    bf = jnp.bfloat16
    lane_s = lax.broadcasted_iota(jnp.int32, (s, LANES), 1)
    row_blk = lax.broadcasted_iota(jnp.int32, (s, LANES), 0) // MOBA_BLOCK
    k2 = k_ref[...].astype(f32)
    kaug_ref[0] = jnp.where(lane_s < HEAD_DIM, k2,
                            jnp.where(lane_s - HEAD_DIM == row_blk, 1.0, 0.0)).astype(bf)
    kaug_ref[1] = jnp.where(lane_s >= HEAD_DIM, k2,
                            jnp.where(lane_s == row_blk, 1.0, 0.0)).astype(bf)
    lane_k = lax.broadcasted_iota(jnp.int32, (kt, LANES), 1)
    for t in range(s // kt):
        vt = v_ref[t * kt:(t + 1) * kt, :].astype(f32)
        vaug_ref[0, t] = jnp.where(lane_k < HEAD_DIM, vt, 1.0).T.astype(bf)
        vaug_ref[1, t] = jnp.where(lane_k >= HEAD_DIM, vt, 1.0).T.astype(bf)

    offs = (HEAD_DIM, 0)
    ksum = jnp.sum(k2.reshape(nb, MOBA_BLOCK, LANES), axis=1)
    kmean = ksum * (1.0 / MOBA_BLOCK)
    lane_b = lax.broadcasted_iota(jnp.int32, (nb, LANES), 1)
    km0 = jnp.where(lane_b < HEAD_DIM, kmean, 0.0)
    km1 = jnp.where(lane_b >= HEAD_DIM, kmean, 0.0)
    zgap = jnp.zeros((HEAD_DIM - nb, LANES), f32)
    km_ref[...] = jnp.concatenate([km1, zgap, km0, zgap], axis=0).astype(bf)

    lane_t = lax.broadcasted_iota(jnp.int32, (tq, LANES), 1)
    causal = (lax.broadcasted_iota(jnp.int32, (kt, tq), 0)
              <= lax.broadcasted_iota(jnp.int32, (kt, tq), 1))
    blk_i = lax.broadcasted_iota(jnp.int32, (nb, tq), 0)
    blk_f = blk_i.astype(f32)
    q_sub = lax.broadcasted_iota(jnp.int32, (nb, tq), 1) // MOBA_BLOCK
    zgap_t = jnp.zeros((HEAD_DIM - nb, tq), f32)
    nt = (((1,), (1,)), ((), ()))
    heads = (0, 1)
    q_lanes = (lane_t < HEAD_DIM, lane_t >= HEAD_DIM)

    def q_tile(t_last, carry):
        r0 = pl.multiple_of(t_last * tq, tq)
        q2 = q_ref[pl.ds(r0, tq), :]
        q2f = q2.astype(f32)
        gate_t = lax.dot_general(km_ref[...], q2, nt, preferred_element_type=f32)
        j = t_last * (tq // MOBA_BLOCK) + q_sub
        valid = blk_i < j
        pens = []
        for h in heads:
            g = jnp.where(valid, gate_t[offs[h]:offs[h] + nb, :], NEG)
            sel = jnp.zeros((nb, tq), jnp.bool_)
            for _ in range(MOBA_TOPK):
                m = jnp.max(g, axis=0, keepdims=True)
                first = jnp.min(jnp.where(g == m, blk_f, 1e9), axis=0, keepdims=True)
                pick = blk_f == first
                sel = sel | pick
                g = jnp.where(pick, NEG, g)
            keep = (sel & valid) | (blk_i == j)
            pens.append(jnp.where(keep, 0.0, NEG))
        pen = jnp.concatenate([pens[1], zgap_t, pens[0], zgap_t], axis=0).T
        qas = []
        for h in heads:
            qa = jnp.where(q_lanes[h], q2f, pen).astype(bf)
            sc = lax.dot_general(kaug_ref[h, pl.ds(r0, kt), :], qa, nt,
                                 preferred_element_type=f32)
            sc_ref[h] = jnp.where(causal, sc, NEG)
            qas.append(qa)

        p_ref[...] = jnp.zeros_like(p_ref)

        def tile_of(visit):
            return jnp.where(visit <= 0, t_last, visit - 1)

        def kv_step(i, mc):
            t_pv = tile_of(i - 1)
            k0 = pl.multiple_of(jnp.clip(i, 0, jnp.maximum(t_last - 1, 0)) * kt, kt)
            out = []
            for h in heads:
                m_old, acc = mc[2 * h], mc[2 * h + 1]
                pv = jnp.dot(vaug_ref[h, t_pv], p_ref[h], preferred_element_type=f32)
                sc = sc_ref[h]
                sc_next = lax.dot_general(kaug_ref[h, pl.ds(k0, kt), :], qas[h], nt,
                                          preferred_element_type=f32)
                m_new = jnp.maximum(m_old, jnp.max(sc, axis=0, keepdims=True))
                alpha = jnp.exp(m_old - m_new)
                p_ref[h] = jnp.exp(sc - m_new).astype(bf)
                sc_ref[h] = sc_next
                out += [m_new, (acc + pv) * alpha]
            return tuple(out)

        m_init = jnp.full((1, tq), NEG, f32)
        acc_init = jnp.zeros((LANES, tq), f32)
        fin = lax.fori_loop(0, t_last + 1, kv_step, (m_init, acc_init, m_init, acc_init))
        t_fin = tile_of(t_last)
        accs = [fin[2 * h + 1] + jnp.dot(vaug_ref[h, t_fin], p_ref[h],
                                         preferred_element_type=f32) for h in heads]
        o0 = accs[0][:HEAD_DIM] / accs[0][HEAD_DIM:]
        o1 = accs[1][HEAD_DIM:] / accs[1][:HEAD_DIM]
        o_ref[pl.ds(r0, tq), :] = jnp.concatenate([o0, o1], axis=0).T.astype(o_ref.dtype)
        return carry

    lax.fori_loop(0, s // tq, q_tile, 0)


def _moba(p3):
    b, s, _ = p3.shape
    n_pairs = MOBA_HEADS // 2
    blk = (None, s, LANES)
    return pl.pallas_call(
        _moba_kernel,
        grid=(b, n_pairs),
        in_specs=[pl.BlockSpec(blk, lambda i, j: (i, 0, j)),
                  pl.BlockSpec(blk, lambda i, j: (i, 0, n_pairs + j)),
                  pl.BlockSpec(blk, lambda i, j: (i, 0, 2 * n_pairs + j))],
        out_specs=pl.BlockSpec(blk, lambda i, j: (i, 0, j)),
        out_shape=jax.ShapeDtypeStruct((b, s, MOBA_W), jnp.bfloat16),
        scratch_shapes=[pltpu.VMEM((2, s, LANES), jnp.bfloat16),
                        pltpu.VMEM((2, s // MOBA_KTILE, LANES, MOBA_KTILE), jnp.bfloat16),
                        pltpu.VMEM((LANES, LANES), jnp.bfloat16),
                        pltpu.VMEM((2, MOBA_KTILE, MOBA_KTILE), jnp.float32),
                        pltpu.VMEM((2, MOBA_KTILE, MOBA_KTILE), jnp.bfloat16)],
        compiler_params=_cparams(("arbitrary", "arbitrary")),
        name="moba_attn",
    )(p3, p3, p3)


def _dil_kernel(q_ref, k_ref, v_ref, o_ref, lse_ref):
    tq = q_ref.shape[0]
    sub = DIL_SPAN
    kw = 2 * sub
    base_tile = pl.program_id(2) * tq
    lane = lax.broadcasted_iota(jnp.int32, (sub, LANES), 1)
    first = lane < HEAD_DIM
    r_i = lax.broadcasted_iota(jnp.int32, (sub, kw), 0)
    c_i = lax.broadcasted_iota(jnp.int32, (sub, kw), 1)
    nt = (((1,), (1,)), ((), ()))

    def sub_block(sb, carry):
        q0 = pl.multiple_of(sb * sub, sub)
        base = base_tile + q0
        kstart = pl.multiple_of(jnp.maximum(base - sub, 0), sub)
        dist = (base + r_i) - (kstart + c_i)
        mask = (dist >= 0) & (dist <= DIL_SPAN)
        for pr in range(DIL_HEADS_PER_GROUP // 2):
            cs = slice(pr * LANES, (pr + 1) * LANES)
            q2 = q_ref[pl.ds(q0, sub), cs].astype(jnp.float32)
            k2 = k_ref[pl.ds(kstart, kw), cs]
            v2 = v_ref[pl.ds(kstart, kw), cs]
            outs, lses = [], []
            for h in range(2):
                qh = jnp.where(first if h == 0 else jnp.logical_not(first), q2, 0.0)
                qh = qh.astype(jnp.bfloat16)
                sc = lax.dot_general(qh, k2, nt, preferred_element_type=jnp.float32)
                sc = jnp.where(mask, sc, NEG)
                m = jnp.max(sc, axis=1, keepdims=True)
                p = jnp.exp(sc - m)
                l = jnp.sum(p, axis=1, keepdims=True)
                o = jnp.dot(p.astype(jnp.bfloat16), v2, preferred_element_type=jnp.float32)
                outs.append(o / l)
                lses.append(jnp.broadcast_to(m + jnp.log(l), (sub, LANES)))
            o_ref[pl.ds(q0, sub), cs] = jnp.where(first, outs[0], outs[1]).astype(o_ref.dtype)
            lse_ref[pl.ds(q0, sub), cs] = jnp.where(first, lses[0], lses[1])
        return carry

    lax.fori_loop(0, tq // sub, sub_block, 0, unroll=True)


def _dil_group(pg, b, dil):
    w = DIL_OUT_W
    l = pg.shape[0] // b
    pv = pg.reshape(b, l, dil * 3 * w)
    tq = min(DIL_TQ, l)
    o, lse = pl.pallas_call(
        _dil_kernel,
        grid=(b, dil, l // tq),
        in_specs=[pl.BlockSpec((None, tq, w), lambda i, r, t: (i, t, 3 * r)),
                  pl.BlockSpec((None, l, w), lambda i, r, t: (i, 0, 3 * r + 1)),
                  pl.BlockSpec((None, l, w), lambda i, r, t: (i, 0, 3 * r + 2))],
        out_specs=[pl.BlockSpec((None, tq, w), lambda i, r, t: (i, t, r)),
                   pl.BlockSpec((None, tq, w), lambda i, r, t: (i, t, r))],
        out_shape=[jax.ShapeDtypeStruct((b, l, dil * w), jnp.bfloat16),
                   jax.ShapeDtypeStruct((b, l, dil * w), jnp.float32)],
        compiler_params=_cparams(("arbitrary", "arbitrary", "arbitrary")),
        name=f"dil_attn_d{dil}",
    )(pv, pv, pv)
    return o, lse


def _layer_norm(y, g, b):
    mu = jnp.mean(y, axis=-1, keepdims=True)
    yc = y - mu
    var = jnp.mean(yc * yc, axis=-1, keepdims=True)
    return yc * lax.rsqrt(var + LN_EPS) * g + b


def _sigmoid(z):
    return 0.5 * jnp.tanh(0.5 * z) + 0.5


def _natural_order(src_ref, dst_ref, a, dil):
    rows = src_ref.shape[0]
    slabs = DIL_OUT_W // LANES
    for r in range(dil):
        for sl in range(slabs):
            c = r * DIL_OUT_W + sl * LANES
            dst_ref[a, sl, pl.ds(r, rows, stride=dil), :] = (
                src_ref[:, c:c + LANES].astype(jnp.float32))
    return jnp.concatenate([dst_ref[a, sl] for sl in range(slabs)], axis=1)


def _merge_kernel(alpha, x_ref, ym_ref, bg_ref, cg_ref, hc_ref, cgp_ref, hcp_ref,
                  g0_ref, g1_ref, g2_ref, o1_ref, o2_ref, o3_ref, l1_ref, l2_ref, l3_ref,
                  wm_ref, wc_ref, wd_ref, wo_ref, wsc_ref, lng_ref, lnb_ref, out_ref, nat_ref):
    f32 = jnp.float32
    tm = x_ref.shape[0]
    ch = cg_ref[...].astype(f32) * hc_ref[...].astype(f32)
    chp = cgp_ref[...].astype(f32) * hcp_ref[...].astype(f32)
    chp = jnp.where(pl.program_id(1) == 0, 0.0, chp)
    hp = chp.shape[0]
    prev1 = chp[hp - 1:hp, :]
    prev2 = chp[hp - 2:hp - 1, :]
    row = lax.broadcasted_iota(jnp.int32, (tm, CONV_WIDTH), 0)
    ch1 = jnp.where(row == 0, prev1, pltpu.roll(ch, 1, axis=0))
    ch2 = pltpu.roll(ch, 2, axis=0)
    ch2 = jnp.where(row == 0, prev2, jnp.where(row == 1, prev1, ch2))
    wsc = wsc_ref[...]
    conv = ch * wsc[0:1, :] + ch1 * wsc[1:2, :] + ch2 * wsc[2:3, :]
    y_conv = (bg_ref[...].astype(f32) * conv).astype(jnp.bfloat16)

    d2, d3 = DIL_GROUPS[1][1], DIL_GROUPS[2][1]
    o1, l1 = o1_ref[...].astype(f32), l1_ref[...]
    o2, l2 = _natural_order(o2_ref, nat_ref, 0, d2), _natural_order(l2_ref, nat_ref, 1, d2)
    o3, l3 = _natural_order(o3_ref, nat_ref, 2, d3), _natural_order(l3_ref, nat_ref, 3, d3)
    lm = jnp.maximum(jnp.maximum(l1, l2), l3)
    e1, e2, e3 = jnp.exp(l1 - lm), jnp.exp(l2 - lm), jnp.exp(l3 - lm)
    den = e1 + e2 + e3
    y_dil = ((e1 / den) * o1 + (e2 / den) * o2 + (e3 / den) * o3).astype(jnp.bfloat16)

    pm = jnp.dot(ym_ref[...], wm_ref[...], preferred_element_type=f32)
    pc = jnp.dot(y_conv, wc_ref[...], preferred_element_type=f32)
    pd = jnp.dot(y_dil, wd_ref[...], preferred_element_type=f32)
    merged = (_sigmoid(g0_ref[...].astype(f32)) * pm
              + _sigmoid(g1_ref[...].astype(f32)) * pc
              + _sigmoid(g2_ref[...].astype(f32)) * pd)
    mix = jnp.dot(merged.astype(jnp.bfloat16), wo_ref[...], preferred_element_type=f32)
    y = alpha * x_ref[...] + mix
    out_ref[...] = _layer_norm(y, lng_ref[...], lnb_ref[...])


def _merge(alpha, x3, ym, p3, dil_outs, wm, wc, wd, wo, wsc, lng, lnb):
    b, s, d = x3.shape
    tm = MERGE_TM
    hp = 16
    cw = CONV_WIDTH
    cblk = OFF_CONV // cw
    gblk = OFF_GATE // d
    row = lambda w: pl.BlockSpec((None, tm, w), lambda i, t: (i, t, 0))
    pcol = lambda w, c: pl.BlockSpec((None, tm, w), lambda i, t, c=c: (i, t, c))
    prev = lambda c: pl.BlockSpec(
        (None, hp, cw), lambda i, t, c=c: (i, jnp.maximum(t * (tm // hp) - 1, 0), c))
    full = lambda a: pl.BlockSpec(a.shape, lambda i, t: (0,) * a.ndim)
    (o1, l1), (o2, l2), (o3, l3) = dil_outs
    dils = [dil for _, dil in DIL_GROUPS]
    grp = lambda dil: pl.BlockSpec((None, tm // dil, dil * DIL_OUT_W), lambda i, t: (i, t, 0))
    in_specs = [row(d), row(MOBA_W),
                pcol(cw, cblk), pcol(cw, cblk + 1), pcol(cw, cblk + 2),
                prev(cblk + 1), prev(cblk + 2),
                pcol(d, gblk), pcol(d, gblk + 1), pcol(d, gblk + 2),
                grp(dils[0]), grp(dils[1]), grp(dils[2]),
                grp(dils[0]), grp(dils[1]), grp(dils[2]),
                full(wm), full(wc), full(wd), full(wo), full(wsc), full(lng), full(lnb)]
    return pl.pallas_call(
        functools.partial(_merge_kernel, alpha),
        grid=(b, s // tm),
        in_specs=in_specs,
        out_specs=row(d),
        out_shape=jax.ShapeDtypeStruct((b, s, d), jnp.float32),
        scratch_shapes=[pltpu.VMEM((4, DIL_OUT_W // LANES, tm, LANES), jnp.float32)],
        compiler_params=_cparams(("arbitrary", "arbitrary")),
        name="merge",
    )(x3, ym, p3, p3, p3, p3, p3, p3, p3, p3, o1, o2, o3, l1, l2, l3,
      wm, wc, wd, wo, wsc, lng, lnb)


def _ffn_kernel(alpha, x_ref, xp_ref, wup_ref, wdn_ref, wcv_ref, bcv_ref, lng_ref, lnb_ref,
                out_ref, xs_ref, act_ref):
    f32 = jnp.float32
    tm = x_ref.shape[0]
    halo = xp_ref.shape[0]
    xp = jnp.where(pl.program_id(1) == 0, 0.0, xp_ref[...])
    xs_ref[0:halo, :] = xp.astype(jnp.bfloat16)
    xs_ref[halo:halo + tm, :] = x_ref[...].astype(jnp.bfloat16)

    def conv(u, w):
        y = u * w[0:1, :] + pltpu.roll(u, 1, axis=0) * w[1:2, :] + pltpu.roll(u, 2, axis=0) * w[2:3, :]
        return y[halo:, :]

    for c in range(FFN_NCHUNK):
        gc = slice(c * FFN_CHUNK, (c + 1) * FFN_CHUNK)
        vc = slice(D_FF + c * FFN_CHUNK, D_FF + (c + 1) * FFN_CHUNK)
        xs = xs_ref[...]
        ug = jnp.dot(xs, wup_ref[:, gc], preferred_element_type=f32)
        uv = jnp.dot(xs, wup_ref[:, vc], preferred_element_type=f32)
        yg = conv(ug, wcv_ref[:, gc]) + bcv_ref[:, gc]
        yv = conv(uv, wcv_ref[:, vc]) + bcv_ref[:, vc]
        act_ref[:, gc] = (yg * _sigmoid(yg) * yv).astype(jnp.bfloat16)

    ffn = jnp.dot(act_ref[...], wdn_ref[...], preferred_element_type=f32)
    y = alpha * x_ref[...] + ffn
    out_ref[...] = _layer_norm(y, lng_ref[...], lnb_ref[...])


def _ffn(alpha, x3, wup, wdn, wcv, bcv, lng, lnb):
    b, s, d = x3.shape
    tm = FFN_TM
    halo = FFN_HALO
    full = lambda a: pl.BlockSpec(a.shape, lambda i, t: (0,) * a.ndim)
    return pl.pallas_call(
        functools.partial(_ffn_kernel, alpha),
        grid=(b, s // tm),
        in_specs=[pl.BlockSpec((None, tm, d), lambda i, t: (i, t, 0)),
                  pl.BlockSpec((None, halo, d),
                               lambda i, t: (i, jnp.maximum(t * (tm // halo) - 1, 0), 0)),
                  full(wup), full(wdn), full(wcv), full(bcv), full(lng), full(lnb)],
        out_specs=pl.BlockSpec((None, tm, d), lambda i, t: (i, t, 0)),
        out_shape=jax.ShapeDtypeStruct((b, s, d), jnp.float32),
        scratch_shapes=[pltpu.VMEM((tm + halo, d), jnp.bfloat16),
                        pltpu.VMEM((tm, D_FF), jnp.bfloat16)],
        compiler_params=_cparams(("arbitrary", "arbitrary")),
        name="conv_ffn",
    )(x3, x3, wup, wdn, wcv, bcv, lng, lnb)


def kernel(x, w_in, w_short_conv, w_moba_proj, w_dil_proj, w_conv_proj, w_mix_out,
           ln1_g, ln1_b, w_up, w_ffn_conv, b_ffn_conv, w_down, ln2_g, ln2_b):
    b, s, d = x.shape
    depth = w_in.shape[0]
    alpha = float((2 * depth) ** 0.25)
    bf = jnp.bfloat16
    for l in range(depth):
        x2 = x.reshape(b * s, d)
        p = _in_proj(x2, w_in, l).reshape(b, s, MAIN_COLS)
        pgs = _dil_proj(x2, w_in, l)
        ym = _moba(p)
        dil_outs = [_dil_group(pg, b, dil) for pg, (_, dil) in zip(pgs, DIL_GROUPS)]
        x = _merge(alpha, x, ym, p, dil_outs,
                   w_moba_proj[l].astype(bf), w_conv_proj[l].astype(bf),
                   w_dil_proj[l].astype(bf), w_mix_out[l].astype(bf),
                   w_short_conv[l], ln1_g[l].reshape(1, d), ln1_b[l].reshape(1, d))
        x = _ffn(alpha, x, w_up[l].astype(bf), w_down[l].astype(bf), w_ffn_conv[l],
                 b_ffn_conv[l].reshape(1, 2 * D_FF),
                 ln2_g[l].reshape(1, d), ln2_b[l].reshape(1, d))
    return x
```

```python
import functools

import jax
import jax.numpy as jnp
import numpy as np
from jax import lax
from jax.experimental import pallas as pl
from jax.experimental.pallas import tpu as pltpu

D_MODEL = 1024
HEAD_DIM = 64
MOBA_HEADS = 8
MOBA_BLOCK = 256
MOBA_TOPK = 3
MOBA_W = MOBA_HEADS * HEAD_DIM
DIL_GROUPS = ((128, 1), (512, 4), (2048, 16))
DIL_HEADS_PER_GROUP = 4
DIL_W = 3 * DIL_HEADS_PER_GROUP * HEAD_DIM
DIL_OUT_W = DIL_HEADS_PER_GROUP * HEAD_DIM
CONV_WIDTH = 512
N_BRANCH = 3
D_FF = 2816
LN_EPS = 1e-5

OFF_CONV = 3 * MOBA_W
OFF_GATE = OFF_CONV + 3 * CONV_WIDTH
MAIN_COLS = OFF_GATE + N_BRANCH * D_MODEL

LANES = 128
NEG = -1e30
VMEM_LIMIT = 56 * 1024 * 1024

PROJ_TM = 2048
PROJ_TN = 768
DILP_TM = 512
MOBA_KTILE = 2 * MOBA_BLOCK
DIL_TQ = 512
DIL_SPAN = 128
MERGE_TM = 512
FFN_TM = 512
FFN_HALO = 16
FFN_CHUNK = 256
FFN_NCHUNK = D_FF // FFN_CHUNK


def _cparams(sem):
    return pltpu.CompilerParams(dimension_semantics=sem, vmem_limit_bytes=VMEM_LIMIT)


def _proj_kernel(x_ref, w_ref, cs_ref, o_ref, xb_ref):
    @pl.when(pl.program_id(1) == 0)
    def _():
        xb_ref[...] = x_ref[...].astype(jnp.bfloat16)

    acc = jnp.dot(xb_ref[...], w_ref[...].astype(jnp.bfloat16),
                  preferred_element_type=jnp.float32)
    o_ref[...] = (acc * cs_ref[...]).astype(o_ref.dtype)


def _q_scale_row(q_cols, cols):
    cs = np.ones((1, cols), np.float32)
    cs[:, :q_cols] = HEAD_DIM ** -0.5
    return jnp.asarray(cs)


def _in_proj(x2, w_in, layer):
    n, d = x2.shape
    tn = PROJ_TN
    skip_at = 3 * MOBA_W // tn
    skip = 3 * DIL_W // tn
    src = lambda j: jnp.where(j < skip_at, j, j + skip)
    return pl.pallas_call(
        _proj_kernel,
        grid=(n // PROJ_TM, MAIN_COLS // tn),
        in_specs=[pl.BlockSpec((PROJ_TM, d), lambda i, j: (i, 0)),
                  pl.BlockSpec((None, d, tn), lambda i, j: (layer, 0, src(j))),
                  pl.BlockSpec((1, tn), lambda i, j: (0, j))],
        out_specs=pl.BlockSpec((PROJ_TM, tn), lambda i, j: (i, j)),
        out_shape=jax.ShapeDtypeStruct((n, MAIN_COLS), jnp.bfloat16),
        scratch_shapes=[pltpu.VMEM((PROJ_TM, d), jnp.bfloat16)],
        compiler_params=_cparams(("arbitrary", "arbitrary")),
        name="in_proj",
    )(x2, w_in, _q_scale_row(MOBA_W, MAIN_COLS))


def _dil_proj_kernel(x_ref, wq_ref, wk_ref, wv_ref, o0_ref, o1_ref, o2_ref, wb_ref, res_ref):
    tm = x_ref.shape[0]

    @pl.when(pl.program_id(0) == 0)
    def _():
        for j, w_ref in enumerate((wq_ref, wk_ref, wv_ref)):
            wb_ref[j] = w_ref[...].astype(jnp.bfloat16)

    xb = x_ref[...].astype(jnp.bfloat16)
    per_part = DIL_W // LANES
    for j in range(3):
        res = jnp.dot(xb, wb_ref[j], preferred_element_type=jnp.float32)
        if j == 0:
            res = res * (HEAD_DIM ** -0.5)
        for cb in range(per_part):
            res_ref[j * per_part + cb] = res[:, cb * LANES:(cb + 1) * LANES]
    slabs = DIL_OUT_W // LANES
    for g, o_ref in enumerate((o0_ref, o1_ref, o2_ref)):
        d = DIL_GROUPS[g][1]
        rows = tm // d
        for r in range(d):
            for j in range(3):
                for sl in range(slabs):
                    src_cb = (j * DIL_W + g * DIL_OUT_W) // LANES + sl
                    dst_c = (r * 3 + j) * DIL_OUT_W + sl * LANES
                    if d == 1:
                        src = res_ref[src_cb]
                    else:
                        src = res_ref[src_cb, pl.ds(r, rows, stride=d), :]
                    o_ref[:, dst_c:dst_c + LANES] = src.astype(o_ref.dtype)


def _dil_proj(x2, w_in, layer):
    n, d = x2.shape
    tm = DILP_TM
    dils = [dil for _, dil in DIL_GROUPS]
    first = 3 * MOBA_W // DIL_W
    wspec = lambda j: pl.BlockSpec((None, d, DIL_W), lambda i, j=j: (layer, 0, first + j))
    return pl.pallas_call(
        _dil_proj_kernel,
        grid=(n // tm,),
        in_specs=[pl.BlockSpec((tm, d), lambda i: (i, 0)), wspec(0), wspec(1), wspec(2)],
        out_specs=[pl.BlockSpec((tm // dil, dil * 3 * DIL_OUT_W), lambda i: (i, 0))
                   for dil in dils],
        out_shape=[jax.ShapeDtypeStruct((n // dil, dil * 3 * DIL_OUT_W), jnp.bfloat16)
                   for dil in dils],
        scratch_shapes=[pltpu.VMEM((3, d, DIL_W), jnp.bfloat16),
                        pltpu.VMEM((3 * DIL_W // LANES, tm, LANES), jnp.float32)],
        compiler_params=_cparams(("arbitrary",)),
        name="dil_proj",
    )(x2, w_in, w_in, w_in)


def _moba_kernel(q_ref, k_ref, v_ref, o_ref, kaug_ref, vaug_ref, km_ref, sc_ref, p_ref):
    s = q_ref.shape[0]
    nb = s // MOBA_BLOCK
    tq = kt = MOBA_KTILE
    f32 = jnp.float32
    bf = jnp.bfloat16
    lane_s = lax.broadcasted_iota(jnp.int32, (s, LANES), 1)
    row_blk = lax.broadcasted_iota(jnp.int32, (s, LANES), 0) // MOBA_BLOCK
    k2 = k_ref[...].astype(f32)
    kaug_ref[0] = jnp.where(lane_s < HEAD_DIM, k2,
                            jnp.where(lane_s - HEAD_DIM == row_blk, 1.0, 0.0)).astype(bf)
    kaug_ref[1] = jnp.where(lane_s >= HEAD_DIM, k2,
                            jnp.where(lane_s == row_blk, 1.0, 0.0)).astype(bf)
    lane_k = lax.broadcasted_iota(jnp.int32, (kt, LANES), 1)
    for t in range(s // kt):
        vt = v_ref[t * kt:(t + 1) * kt, :].astype(f32)
        vaug_ref[0, t] = jnp.where(lane_k < HEAD_DIM, vt, 1.0).T.astype(bf)
        vaug_ref[1, t] = jnp.where(lane_k >= HEAD_DIM, vt, 1.0).T.astype(bf)

    offs = (HEAD_DIM, 0)
    ksum = jnp.sum(k2.reshape(nb, MOBA_BLOCK, LANES), axis=1)
    kmean = ksum * (1.0 / MOBA_BLOCK)
    lane_b = lax.broadcasted_iota(jnp.int32, (nb, LANES), 1)
    km0 = jnp.where(lane_b < HEAD_DIM, kmean, 0.0)
    km1 = jnp.where(lane_b >= HEAD_DIM, kmean, 0.0)
    zgap = jnp.zeros((HEAD_DIM - nb, LANES), f32)
    km_ref[...] = jnp.concatenate([km1, zgap, km0, zgap], axis=0).astype(bf)

    lane_t = lax.broadcasted_iota(jnp.int32, (tq, LANES), 1)
    causal = (lax.broadcasted_iota(jnp.int32, (kt, tq), 0)
              <= lax.broadcasted_iota(jnp.int32, (kt, tq), 1))
    blk_i = lax.broadcasted_iota(jnp.int32, (nb, tq), 0)
    blk_f = blk_i.astype(f32)
    q_sub = lax.broadcasted_iota(jnp.int32, (nb, tq), 1) // MOBA_BLOCK
    zgap_t = jnp.zeros((HEAD_DIM - nb, tq), f32)
    nt = (((1,), (1,)), ((), ()))
    heads = (0, 1)
    q_lanes = (lane_t < HEAD_DIM, lane_t >= HEAD_DIM)

    def q_tile(t_last, carry):
        r0 = pl.multiple_of(t_last * tq, tq)
        q2 = q_ref[pl.ds(r0, tq), :]
        q2f = q2.astype(f32)
        gate_t = lax.dot_general(km_ref[...], q2, nt, preferred_element_type=f32)
        j = t_last * (tq // MOBA_BLOCK) + q_sub
        valid = blk_i < j
        pens = []
        for h in heads:
            g = jnp.where(valid, gate_t[offs[h]:offs[h] + nb, :], NEG)
            sel = jnp.zeros((nb, tq), jnp.bool_)
            for _ in range(MOBA_TOPK):
                m = jnp.max(g, axis=0, keepdims=True)
                first = jnp.min(jnp.where(g == m, blk_f, 1e9), axis=0, keepdims=True)
                pick = blk_f == first
                sel = sel | pick
                g = jnp.where(pick, NEG, g)
            keep = (sel & valid) | (blk_i == j)
            pens.append(jnp.where(keep, 0.0, NEG))
        pen = jnp.concatenate([pens[1], zgap_t, pens[0], zgap_t], axis=0).T
        qas = []
        for h in heads:
            qa = jnp.where(q_lanes[h], q2f, pen).astype(bf)
            sc = lax.dot_general(kaug_ref[h, pl.ds(r0, kt), :], qa, nt,
                                 preferred_element_type=f32)
            sc_ref[h] = jnp.where(causal, sc, NEG)
            qas.append(qa)

        p_ref[...] = jnp.zeros_like(p_ref)

        def tile_of(visit):
            return jnp.where(visit <= 0, t_last, visit - 1)

        def kv_step(i, mc):
            t_pv = tile_of(i - 1)
            k0 = pl.multiple_of(jnp.clip(i, 0, jnp.maximum(t_last - 1, 0)) * kt, kt)
            out = []
            for h in heads:
                m_old, acc = mc[2 * h], mc[2 * h + 1]
                pv = jnp.dot(vaug_ref[h, t_pv], p_ref[h], preferred_element_type=f32)
                sc = sc_ref[h]
                sc_next = lax.dot_general(kaug_ref[h, pl.ds(k0, kt), :], qas[h], nt,
                                          preferred_element_type=f32)
                m_new = jnp.maximum(m_old, jnp.max(sc, axis=0, keepdims=True))
                alpha = jnp.exp(m_old - m_new)
                p_ref[h] = jnp.exp(sc - m_new).astype(bf)
                sc_ref[h] = sc_next
                out += [m_new, (acc + pv) * alpha]
            return tuple(out)

        m_init = jnp.full((1, tq), NEG, f32)
        acc_init = jnp.zeros((LANES, tq), f32)
        def two_steps(ii, mc):
            return kv_step(2 * ii + 1, kv_step(2 * ii, mc))

        n_steps = t_last + 1
        mc = lax.fori_loop(0, n_steps // 2, two_steps, (m_init, acc_init, m_init, acc_init))
        fin = lax.cond(n_steps % 2 == 1, lambda c: kv_step(t_last, c), lambda c: c, mc)
        t_fin = tile_of(t_last)
        accs = [fin[2 * h + 1] + jnp.dot(vaug_ref[h, t_fin], p_ref[h],
                                         preferred_element_type=f32) for h in heads]
        o0 = accs[0][:HEAD_DIM] / accs[0][HEAD_DIM:]
        o1 = accs[1][HEAD_DIM:] / accs[1][:HEAD_DIM]
        o_ref[pl.ds(r0, tq), :] = jnp.concatenate([o0, o1], axis=0).T.astype(o_ref.dtype)
        return carry

    lax.fori_loop(0, s // tq, q_tile, 0)


def _moba(p3):
    b, s, _ = p3.shape
    n_pairs = MOBA_HEADS // 2
    blk = (None, s, LANES)
    return pl.pallas_call(
        _moba_kernel,
        grid=(b, n_pairs),
        in_specs=[pl.BlockSpec(blk, lambda i, j: (i, 0, j)),
                  pl.BlockSpec(blk, lambda i, j: (i, 0, n_pairs + j)),
                  pl.BlockSpec(blk, lambda i, j: (i, 0, 2 * n_pairs + j))],
        out_specs=pl.BlockSpec(blk, lambda i, j: (i, 0, j)),
        out_shape=jax.ShapeDtypeStruct((b, s, MOBA_W), jnp.bfloat16),
        scratch_shapes=[pltpu.VMEM((2, s, LANES), jnp.bfloat16),
                        pltpu.VMEM((2, s // MOBA_KTILE, LANES, MOBA_KTILE), jnp.bfloat16),
                        pltpu.VMEM((LANES, LANES), jnp.bfloat16),
                        pltpu.VMEM((2, MOBA_KTILE, MOBA_KTILE), jnp.float32),
                        pltpu.VMEM((2, MOBA_KTILE, MOBA_KTILE), jnp.bfloat16)],
        compiler_params=_cparams(("arbitrary", "arbitrary")),
        name="moba_attn",
    )(p3, p3, p3)


def _dil_kernel(q_ref, k_ref, v_ref, o_ref, lse_ref):
    tq = q_ref.shape[0]
    sub = DIL_SPAN
    kw = 2 * sub
    base_tile = pl.program_id(2) * tq
    lane = lax.broadcasted_iota(jnp.int32, (sub, LANES), 1)
    first = lane < HEAD_DIM
    r_i = lax.broadcasted_iota(jnp.int32, (sub, kw), 0)
    c_i = lax.broadcasted_iota(jnp.int32, (sub, kw), 1)
    nt = (((1,), (1,)), ((), ()))

    def sub_block(sb, carry):
        q0 = pl.multiple_of(sb * sub, sub)
        base = base_tile + q0
        kstart = pl.multiple_of(jnp.maximum(base - sub, 0), sub)
        dist = (base + r_i) - (kstart + c_i)
        mask = (dist >= 0) & (dist <= DIL_SPAN)
        for pr in range(DIL_HEADS_PER_GROUP // 2):
            cs = slice(pr * LANES, (pr + 1) * LANES)
            q2 = q_ref[pl.ds(q0, sub), cs].astype(jnp.float32)
            k2 = k_ref[pl.ds(kstart, kw), cs]
            v2 = v_ref[pl.ds(kstart, kw), cs]
            outs, lses = [], []
            for h in range(2):
                qh = jnp.where(first if h == 0 else jnp.logical_not(first), q2, 0.0)
                qh = qh.astype(jnp.bfloat16)
                sc = lax.dot_general(qh, k2, nt, preferred_element_type=jnp.float32)
                sc = jnp.where(mask, sc, NEG)
                m = jnp.max(sc, axis=1, keepdims=True)
                p = jnp.exp(sc - m)
                l = jnp.sum(p, axis=1, keepdims=True)
                o = jnp.dot(p.astype(jnp.bfloat16), v2, preferred_element_type=jnp.float32)
                outs.append(o / l)
                lses.append(jnp.broadcast_to(m + jnp.log(l), (sub, LANES)))
            o_ref[pl.ds(q0, sub), cs] = jnp.where(first, outs[0], outs[1]).astype(o_ref.dtype)
            lse_ref[pl.ds(q0, sub), cs] = jnp.where(first, lses[0], lses[1])
        return carry

    lax.fori_loop(0, tq // sub, sub_block, 0, unroll=True)


def _dil_group(pg, b, dil):
    w = DIL_OUT_W
    l = pg.shape[0] // b
    pv = pg.reshape(b, l, dil * 3 * w)
    tq = min(DIL_TQ, l)
    o, lse = pl.pallas_call(
        _dil_kernel,
        grid=(b, dil, l // tq),
        in_specs=[pl.BlockSpec((None, tq, w), lambda i, r, t: (i, t, 3 * r)),
                  pl.BlockSpec((None, l, w), lambda i, r, t: (i, 0, 3 * r + 1)),
                  pl.BlockSpec((None, l, w), lambda i, r, t: (i, 0, 3 * r + 2))],
        out_specs=[pl.BlockSpec((None, tq, w), lambda i, r, t: (i, t, r)),
                   pl.BlockSpec((None, tq, w), lambda i, r, t: (i, t, r))],
        out_shape=[jax.ShapeDtypeStruct((b, l, dil * w), jnp.bfloat16),
                   jax.ShapeDtypeStruct((b, l, dil * w), jnp.float32)],
        compiler_params=_cparams(("arbitrary", "arbitrary", "arbitrary")),
        name=f"dil_attn_d{dil}",
    )(pv, pv, pv)
    return o, lse


def _layer_norm(y, g, b):
    mu = jnp.mean(y, axis=-1, keepdims=True)
    yc = y - mu
    var = jnp.mean(yc * yc, axis=-1, keepdims=True)
    return yc * lax.rsqrt(var + LN_EPS) * g + b


def _sigmoid(z):
    return 0.5 * jnp.tanh(0.5 * z) + 0.5


def _natural_order(src_ref, dst_ref, a, dil):
    rows = src_ref.shape[0]
    slabs = DIL_OUT_W // LANES
    for r in range(dil):
        for sl in range(slabs):
            c = r * DIL_OUT_W + sl * LANES
            dst_ref[a, sl, pl.ds(r, rows, stride=dil), :] = (
                src_ref[:, c:c + LANES].astype(jnp.float32))
    return jnp.concatenate([dst_ref[a, sl] for sl in range(slabs)], axis=1)


def _merge_kernel(alpha, x_ref, ym_ref, bg_ref, cg_ref, hc_ref, cgp_ref, hcp_ref,
                  g0_ref, g1_ref, g2_ref, o1_ref, o2_ref, o3_ref, l1_ref, l2_ref, l3_ref,
                  wm_ref, wc_ref, wd_ref, wo_ref, wsc_ref, lng_ref, lnb_ref, out_ref, nat_ref):
    f32 = jnp.float32
    tm = x_ref.shape[0]
    ch = cg_ref[...].astype(f32) * hc_ref[...].astype(f32)
    chp = cgp_ref[...].astype(f32) * hcp_ref[...].astype(f32)
    chp = jnp.where(pl.program_id(1) == 0, 0.0, chp)
    hp = chp.shape[0]
    prev1 = chp[hp - 1:hp, :]
    prev2 = chp[hp - 2:hp - 1, :]
    row = lax.broadcasted_iota(jnp.int32, (tm, CONV_WIDTH), 0)
    ch1 = jnp.where(row == 0, prev1, pltpu.roll(ch, 1, axis=0))
    ch2 = pltpu.roll(ch, 2, axis=0)
    ch2 = jnp.where(row == 0, prev2, jnp.where(row == 1, prev1, ch2))
    wsc = wsc_ref[...]
    conv = ch * wsc[0:1, :] + ch1 * wsc[1:2, :] + ch2 * wsc[2:3, :]
    y_conv = (bg_ref[...].astype(f32) * conv).astype(jnp.bfloat16)

    d2, d3 = DIL_GROUPS[1][1], DIL_GROUPS[2][1]
    o1, l1 = o1_ref[...].astype(f32), l1_ref[...]
    o2, l2 = _natural_order(o2_ref, nat_ref, 0, d2), _natural_order(l2_ref, nat_ref, 1, d2)
    o3, l3 = _natural_order(o3_ref, nat_ref, 2, d3), _natural_order(l3_ref, nat_ref, 3, d3)
    lm = jnp.maximum(jnp.maximum(l1, l2), l3)
    e1, e2, e3 = jnp.exp(l1 - lm), jnp.exp(l2 - lm), jnp.exp(l3 - lm)
    den = e1 + e2 + e3
    y_dil = ((e1 / den) * o1 + (e2 / den) * o2 + (e3 / den) * o3).astype(jnp.bfloat16)

    pm = jnp.dot(ym_ref[...], wm_ref[...], preferred_element_type=f32)
    pc = jnp.dot(y_conv, wc_ref[...], preferred_element_type=f32)
    pd = jnp.dot(y_dil, wd_ref[...], preferred_element_type=f32)
    merged = (_sigmoid(g0_ref[...].astype(f32)) * pm
              + _sigmoid(g1_ref[...].astype(f32)) * pc
              + _sigmoid(g2_ref[...].astype(f32)) * pd)
    mix = jnp.dot(merged.astype(jnp.bfloat16), wo_ref[...], preferred_element_type=f32)
    y = alpha * x_ref[...] + mix
    out_ref[...] = _layer_norm(y, lng_ref[...], lnb_ref[...])


def _merge(alpha, x3, ym, p3, dil_outs, wm, wc, wd, wo, wsc, lng, lnb):
    b, s, d = x3.shape
    tm = MERGE_TM
    hp = 16
    cw = CONV_WIDTH
    cblk = OFF_CONV // cw
    gblk = OFF_GATE // d
    row = lambda w: pl.BlockSpec((None, tm, w), lambda i, t: (i, t, 0))
    pcol = lambda w, c: pl.BlockSpec((None, tm, w), lambda i, t, c=c: (i, t, c))
    prev = lambda c: pl.BlockSpec(
        (None, hp, cw), lambda i, t, c=c: (i, jnp.maximum(t * (tm // hp) - 1, 0), c))
    full = lambda a: pl.BlockSpec(a.shape, lambda i, t: (0,) * a.ndim)
    (o1, l1), (o2, l2), (o3, l3) = dil_outs
    dils = [dil for _, dil in DIL_GROUPS]
    grp = lambda dil: pl.BlockSpec((None, tm // dil, dil * DIL_OUT_W), lambda i, t: (i, t, 0))
    in_specs = [row(d), row(MOBA_W),
                pcol(cw, cblk), pcol(cw, cblk + 1), pcol(cw, cblk + 2),
                prev(cblk + 1), prev(cblk + 2),
                pcol(d, gblk), pcol(d, gblk + 1), pcol(d, gblk + 2),
                grp(dils[0]), grp(dils[1]), grp(dils[2]),
                grp(dils[0]), grp(dils[1]), grp(dils[2]),
                full(wm), full(wc), full(wd), full(wo), full(wsc), full(lng), full(lnb)]
    return pl.pallas_call(
        functools.partial(_merge_kernel, alpha),
        grid=(b, s // tm),
        in_specs=in_specs,
        out_specs=row(d),
        out_shape=jax.ShapeDtypeStruct((b, s, d), jnp.float32),
        scratch_shapes=[pltpu.VMEM((4, DIL_OUT_W // LANES, tm, LANES), jnp.float32)],
        compiler_params=_cparams(("arbitrary", "arbitrary")),
        name="merge",
    )(x3, ym, p3, p3, p3, p3, p3, p3, p3, p3, o1, o2, o3, l1, l2, l3,
      wm, wc, wd, wo, wsc, lng, lnb)


def _ffn_kernel(alpha, x_ref, xp_ref, wup_ref, wdn_ref, wcv_ref, bcv_ref, lng_ref, lnb_ref,
                out_ref, xs_ref, act_ref):
    f32 = jnp.float32
    tm = x_ref.shape[0]
    halo = xp_ref.shape[0]
    xp = jnp.where(pl.program_id(1) == 0, 0.0, xp_ref[...])
    xs_ref[0:halo, :] = xp.astype(jnp.bfloat16)
    xs_ref[halo:halo + tm, :] = x_ref[...].astype(jnp.bfloat16)

    def conv(u, w):
        y = u * w[0:1, :] + pltpu.roll(u, 1, axis=0) * w[1:2, :] + pltpu.roll(u, 2, axis=0) * w[2:3, :]
        return y[halo:, :]

    for c in range(FFN_NCHUNK):
        gc = slice(c * FFN_CHUNK, (c + 1) * FFN_CHUNK)
        vc = slice(D_FF + c * FFN_CHUNK, D_FF + (c + 1) * FFN_CHUNK)
        xs = xs_ref[...]
        ug = jnp.dot(xs, wup_ref[:, gc], preferred_element_type=f32)
        uv = jnp.dot(xs, wup_ref[:, vc], preferred_element_type=f32)
        yg = conv(ug, wcv_ref[:, gc]) + bcv_ref[:, gc]
        yv = conv(uv, wcv_ref[:, vc]) + bcv_ref[:, vc]
        act_ref[:, gc] = (yg * _sigmoid(yg) * yv).astype(jnp.bfloat16)

    ffn = jnp.dot(act_ref[...], wdn_ref[...], preferred_element_type=f32)
    y = alpha * x_ref[...] + ffn
    out_ref[...] = _layer_norm(y, lng_ref[...], lnb_ref[...])


def _ffn(alpha, x3, wup, wdn, wcv, bcv, lng, lnb):
    b, s, d = x3.shape
    tm = FFN_TM
    halo = FFN_HALO
    full = lambda a: pl.BlockSpec(a.shape, lambda i, t: (0,) * a.ndim)
    return pl.pallas_call(
        functools.partial(_ffn_kernel, alpha),
        grid=(b, s // tm),
        in_specs=[pl.BlockSpec((None, tm, d), lambda i, t: (i, t, 0)),
                  pl.BlockSpec((None, halo, d),
                               lambda i, t: (i, jnp.maximum(t * (tm // halo) - 1, 0), 0)),
                  full(wup), full(wdn), full(wcv), full(bcv), full(lng), full(lnb)],
        out_specs=pl.BlockSpec((None, tm, d), lambda i, t: (i, t, 0)),
        out_shape=jax.ShapeDtypeStruct((b, s, d), jnp.float32),
        scratch_shapes=[pltpu.VMEM((tm + halo, d), jnp.bfloat16),
                        pltpu.VMEM((tm, D_FF), jnp.bfloat16)],
        compiler_params=_cparams(("arbitrary", "arbitrary")),
        name="conv_ffn",
    )(x3, x3, wup, wdn, wcv, bcv, lng, lnb)


def kernel(x, w_in, w_short_conv, w_moba_proj, w_dil_proj, w_conv_proj, w_mix_out,
           ln1_g, ln1_b, w_up, w_ffn_conv, b_ffn_conv, w_down, ln2_g, ln2_b):
    b, s, d = x.shape
    depth = w_in.shape[0]
    alpha = float((2 * depth) ** 0.25)
    bf = jnp.bfloat16
    for l in range(depth):
        x2 = x.reshape(b * s, d)
        p = _in_proj(x2, w_in, l).reshape(b, s, MAIN_COLS)
        pgs = _dil_proj(x2, w_in, l)
        ym = _moba(p)
        dil_outs = [_dil_group(pg, b, dil) for pg, (_, dil) in zip(pgs, DIL_GROUPS)]
        x = _merge(alpha, x, ym, p, dil_outs,
                   w_moba_proj[l].astype(bf), w_conv_proj[l].astype(bf),
                   w_dil_proj[l].astype(bf), w_mix_out[l].astype(bf),
                   w_short_conv[l], ln1_g[l].reshape(1, d), ln1_b[l].reshape(1, d))
        x = _ffn(alpha, x, w_up[l].astype(bf), w_down[l].astype(bf), w_ffn_conv[l],
                 b_ffn_conv[l].reshape(1, 2 * D_FF),
                 ln2_g[l].reshape(1, d), ln2_b[l].reshape(1, d))
    return x
```

```python
import functools

import jax
import jax.numpy as jnp
import numpy as np
from jax import lax
from jax.experimental import pallas as pl
from jax.experimental.pallas import tpu as pltpu

D_MODEL = 1024
HEAD_DIM = 64
MOBA_HEADS = 8
MOBA_BLOCK = 256
MOBA_TOPK = 3
MOBA_W = MOBA_HEADS * HEAD_DIM
DIL_GROUPS = ((128, 1), (512, 4), (2048, 16))
DIL_HEADS_PER_GROUP = 4
DIL_W = 3 * DIL_HEADS_PER_GROUP * HEAD_DIM
DIL_OUT_W = DIL_HEADS_PER_GROUP * HEAD_DIM
CONV_WIDTH = 512
N_BRANCH = 3
D_FF = 2816
LN_EPS = 1e-5

OFF_CONV = 3 * MOBA_W
OFF_GATE = OFF_CONV + 3 * CONV_WIDTH
MAIN_COLS = OFF_GATE + N_BRANCH * D_MODEL

LANES = 128
NEG = -1e30
VMEM_LIMIT = 56 * 1024 * 1024

PROJ_TM = 2048
PROJ_TN = 768
DILP_TM = 512
MOBA_KTILE = 2 * MOBA_BLOCK
MOBA_UNROLL = 2
DIL_TQ = 512
DIL_SPAN = 128
MERGE_TM = 512
FFN_TM = 512
FFN_HALO = 16
FFN_CHUNK = 256
FFN_NCHUNK = D_FF // FFN_CHUNK


def _cparams(sem):
    return pltpu.CompilerParams(dimension_semantics=sem, vmem_limit_bytes=VMEM_LIMIT)


def _proj_kernel(x_ref, w_ref, cs_ref, o_ref, xb_ref, wb_ref):
    j = pl.program_id(1)

    @pl.when(j == 0)
    def _():
        xb_ref[...] = x_ref[...].astype(jnp.bfloat16)

    @pl.when(pl.program_id(0) == 0)
    def _():
        wb_ref[j] = w_ref[...].astype(jnp.bfloat16)

    acc = jnp.dot(xb_ref[...], wb_ref[j], preferred_element_type=jnp.float32)
    o_ref[...] = (acc * cs_ref[...]).astype(o_ref.dtype)


def _q_scale_row(q_cols, cols):
    cs = np.ones((1, cols), np.float32)
    cs[:, :q_cols] = HEAD_DIM ** -0.5
    return jnp.asarray(cs)


def _in_proj(x2, w_in, layer):
    n, d = x2.shape
    tn = PROJ_TN
    skip_at = 3 * MOBA_W // tn
    skip = 3 * DIL_W // tn
    n_ct = MAIN_COLS // tn
    src = lambda j: jnp.where(j < skip_at, j, j + skip)
    w_tile = lambda i, j: src(jnp.where(i == 0, j, n_ct - 1))
    return pl.pallas_call(
        _proj_kernel,
        grid=(n // PROJ_TM, n_ct),
        in_specs=[pl.BlockSpec((PROJ_TM, d), lambda i, j: (i, 0)),
                  pl.BlockSpec((None, d, tn), lambda i, j: (layer, 0, w_tile(i, j))),
                  pl.BlockSpec((1, tn), lambda i, j: (0, j))],
        out_specs=pl.BlockSpec((PROJ_TM, tn), lambda i, j: (i, j)),
        out_shape=jax.ShapeDtypeStruct((n, MAIN_COLS), jnp.bfloat16),
        scratch_shapes=[pltpu.VMEM((PROJ_TM, d), jnp.bfloat16),
                        pltpu.VMEM((n_ct, d, tn), jnp.bfloat16)],
        compiler_params=_cparams(("arbitrary", "arbitrary")),
        name="in_proj",
    )(x2, w_in, _q_scale_row(MOBA_W, MAIN_COLS))


def _dil_proj_kernel(x_ref, wq_ref, wk_ref, wv_ref, o0_ref, o1_ref, o2_ref, wb_ref, res_ref):
    tm = x_ref.shape[0]

    @pl.when(pl.program_id(0) == 0)
    def _():
        for j, w_ref in enumerate((wq_ref, wk_ref, wv_ref)):
            wb_ref[j] = w_ref[...].astype(jnp.bfloat16)

    xb = x_ref[...].astype(jnp.bfloat16)
    per_part = DIL_W // LANES
    for j in range(3):
        res = jnp.dot(xb, wb_ref[j], preferred_element_type=jnp.float32)
        if j == 0:
            res = res * (HEAD_DIM ** -0.5)
        for cb in range(per_part):
            res_ref[j * per_part + cb] = res[:, cb * LANES:(cb + 1) * LANES]
    slabs = DIL_OUT_W // LANES
    for g, o_ref in enumerate((o0_ref, o1_ref, o2_ref)):
        d = DIL_GROUPS[g][1]
        rows = tm // d
        for r in range(d):
            for j in range(3):
                for sl in range(slabs):
                    src_cb = (j * DIL_W + g * DIL_OUT_W) // LANES + sl
                    dst_c = (r * 3 + j) * DIL_OUT_W + sl * LANES
                    if d == 1:
                        src = res_ref[src_cb]
                    else:
                        src = res_ref[src_cb, pl.ds(r, rows, stride=d), :]
                    o_ref[:, dst_c:dst_c + LANES] = src.astype(o_ref.dtype)


def _dil_proj(x2, w_in, layer):
    n, d = x2.shape
    tm = DILP_TM
    dils = [dil for _, dil in DIL_GROUPS]
    first = 3 * MOBA_W // DIL_W
    wspec = lambda j: pl.BlockSpec((None, d, DIL_W), lambda i, j=j: (layer, 0, first + j))
    return pl.pallas_call(
        _dil_proj_kernel,
        grid=(n // tm,),
        in_specs=[pl.BlockSpec((tm, d), lambda i: (i, 0)), wspec(0), wspec(1), wspec(2)],
        out_specs=[pl.BlockSpec((tm // dil, dil * 3 * DIL_OUT_W), lambda i: (i, 0))
                   for dil in dils],
        out_shape=[jax.ShapeDtypeStruct((n // dil, dil * 3 * DIL_OUT_W), jnp.bfloat16)
                   for dil in dils],
        scratch_shapes=[pltpu.VMEM((3, d, DIL_W), jnp.bfloat16),
                        pltpu.VMEM((3 * DIL_W // LANES, tm, LANES), jnp.float32)],
        compiler_params=_cparams(("arbitrary",)),
        name="dil_proj",
    )(x2, w_in, w_in, w_in)


def _moba_kernel(q_ref, k_ref, v_ref, o_ref, kaug_ref, vaug_ref, km_ref, sc_ref):
    s = q_ref.shape[0]
    nb = s // MOBA_BLOCK
    tq = kt = MOBA_KTILE
    f32 = jnp.float32
    bf = jnp.bfloat16
    lane_s = lax.broadcasted_iota(jnp.int32, (s, LANES), 1)
    row_blk = lax.broadcasted_iota(jnp.int32, (s, LANES), 0) // MOBA_BLOCK
    k2 = k_ref[...].astype(f32)
    kaug_ref[0] = jnp.where(lane_s < HEAD_DIM, k2,
                            jnp.where(lane_s - HEAD_DIM == row_blk, 1.0, 0.0)).astype(bf)
    kaug_ref[1] = jnp.where(lane_s >= HEAD_DIM, k2,
                            jnp.where(lane_s == row_blk, 1.0, 0.0)).astype(bf)
    lane_k = lax.broadcasted_iota(jnp.int32, (kt, LANES), 1)
    for t in range(s // kt):
        vt = v_ref[t * kt:(t + 1) * kt, :].astype(f32)
        vaug_ref[0, t] = jnp.where(lane_k < HEAD_DIM, vt, 1.0).T.astype(bf)
        vaug_ref[1, t] = jnp.where(lane_k >= HEAD_DIM, vt, 1.0).T.astype(bf)

    offs = (HEAD_DIM, 0)
    ksum = jnp.sum(k2.reshape(nb, MOBA_BLOCK, LANES), axis=1)
    kmean = ksum * (1.0 / MOBA_BLOCK)
    lane_b = lax.broadcasted_iota(jnp.int32, (nb, LANES), 1)
    km0 = jnp.where(lane_b < HEAD_DIM, kmean, 0.0)
    km1 = jnp.where(lane_b >= HEAD_DIM, kmean, 0.0)
    zgap = jnp.zeros((HEAD_DIM - nb, LANES), f32)
    km_ref[...] = jnp.concatenate([km1, zgap, km0, zgap], axis=0).astype(bf)

    lane_t = lax.broadcasted_iota(jnp.int32, (tq, LANES), 1)
    causal = (lax.broadcasted_iota(jnp.int32, (kt, tq), 0)
              <= lax.broadcasted_iota(jnp.int32, (kt, tq), 1))
    blk_i = lax.broadcasted_iota(jnp.int32, (nb, tq), 0)
    blk_f = blk_i.astype(f32)
    q_sub = lax.broadcasted_iota(jnp.int32, (nb, tq), 1) // MOBA_BLOCK
    zgap_t = jnp.zeros((HEAD_DIM - nb, tq), f32)
    nt = (((1,), (1,)), ((), ()))
    heads = (0, 1)
    q_lanes = (lane_t < HEAD_DIM, lane_t >= HEAD_DIM)

    def q_tile(t_last, carry):
        r0 = pl.multiple_of(t_last * tq, tq)
        q2 = q_ref[pl.ds(r0, tq), :]
        q2f = q2.astype(f32)
        gate_t = lax.dot_general(km_ref[...], q2, nt, preferred_element_type=f32)
        j = t_last * (tq // MOBA_BLOCK) + q_sub
        valid = blk_i < j
        pens = []
        for h in heads:
            g = jnp.where(valid, gate_t[offs[h]:offs[h] + nb, :], NEG)
            sel = jnp.zeros((nb, tq), jnp.bool_)
            for _ in range(MOBA_TOPK):
                m = jnp.max(g, axis=0, keepdims=True)
                first = jnp.min(jnp.where(g == m, blk_f, 1e9), axis=0, keepdims=True)
                pick = blk_f == first
                sel = sel | pick
                g = jnp.where(pick, NEG, g)
            keep = (sel & valid) | (blk_i == j)
            pens.append(jnp.where(keep, 0.0, NEG))
        pen = jnp.concatenate([pens[1], zgap_t, pens[0], zgap_t], axis=0).T
        qas = []
        for h in heads:
            qa = jnp.where(q_lanes[h], q2f, pen).astype(bf)
            sc = lax.dot_general(kaug_ref[h, pl.ds(r0, kt), :], qa, nt,
                                 preferred_element_type=f32)
            sc_ref[h] = jnp.where(causal, sc, NEG)
            qas.append(qa)

        def tile_of(visit):
            return jnp.where(visit <= 0, t_last, visit - 1)

        def kv_step(i, mc):
            t_pv = tile_of(i)
            k0 = pl.multiple_of(jnp.clip(i, 0, jnp.maximum(t_last - 1, 0)) * kt, kt)
            out = []
            for h in heads:
                m_old, acc = mc[2 * h], mc[2 * h + 1]
                sc = sc_ref[h]
                sc_next = lax.dot_general(kaug_ref[h, pl.ds(k0, kt), :], qas[h], nt,
                                          preferred_element_type=f32)
                m_new = jnp.maximum(m_old, jnp.max(sc, axis=0, keepdims=True))
                alpha = jnp.exp(m_old - m_new)
                p = jnp.exp(sc - m_new).astype(bf)
                pv = jnp.dot(vaug_ref[h, t_pv], p, preferred_element_type=f32)
                sc_ref[h] = sc_next
                out += [m_new, acc * alpha + pv]
            return tuple(out)

        m_init = jnp.full((1, tq), NEG, f32)
        acc_init = jnp.zeros((LANES, tq), f32)
        def multi_step(ii, mc):
            for u in range(MOBA_UNROLL):
                mc = kv_step(MOBA_UNROLL * ii + u, mc)
            return mc

        n_steps = t_last + 1
        n_main = n_steps // MOBA_UNROLL
        mc = lax.fori_loop(0, n_main, multi_step, (m_init, acc_init, m_init, acc_init))
        fin = lax.fori_loop(n_main * MOBA_UNROLL, n_steps, kv_step, mc)
        accs = [fin[2 * h + 1] for h in heads]
        o0 = accs[0][:HEAD_DIM] / accs[0][HEAD_DIM:]
        o1 = accs[1][HEAD_DIM:] / accs[1][:HEAD_DIM]
        o_ref[pl.ds(r0, tq), :] = jnp.concatenate([o0, o1], axis=0).T.astype(o_ref.dtype)
        return carry

    lax.fori_loop(0, s // tq, q_tile, 0)


def _moba(p3):
    b, s, _ = p3.shape
    n_pairs = MOBA_HEADS // 2
    blk = (None, s, LANES)
    return pl.pallas_call(
        _moba_kernel,
        grid=(b, n_pairs),
        in_specs=[pl.BlockSpec(blk, lambda i, j: (i, 0, j)),
                  pl.BlockSpec(blk, lambda i, j: (i, 0, n_pairs + j)),
                  pl.BlockSpec(blk, lambda i, j: (i, 0, 2 * n_pairs + j))],
        out_specs=pl.BlockSpec(blk, lambda i, j: (i, 0, j)),
        out_shape=jax.ShapeDtypeStruct((b, s, MOBA_W), jnp.bfloat16),
        scratch_shapes=[pltpu.VMEM((2, s, LANES), jnp.bfloat16),
                        pltpu.VMEM((2, s // MOBA_KTILE, LANES, MOBA_KTILE), jnp.bfloat16),
                        pltpu.VMEM((LANES, LANES), jnp.bfloat16),
                        pltpu.VMEM((2, MOBA_KTILE, MOBA_KTILE), jnp.float32)],
        compiler_params=_cparams(("arbitrary", "arbitrary")),
        name="moba_attn",
    )(p3, p3, p3)


def _dil_kernel(q_ref, k_ref, v_ref, o_ref, lse_ref):
    tq = q_ref.shape[0]
    sub = DIL_SPAN
    kw = 2 * sub
    base_tile = pl.program_id(2) * tq
    lane = lax.broadcasted_iota(jnp.int32, (sub, LANES), 1)
    first = lane < HEAD_DIM
    r_i = lax.broadcasted_iota(jnp.int32, (sub, kw), 0)
    c_i = lax.broadcasted_iota(jnp.int32, (sub, kw), 1)
    nt = (((1,), (1,)), ((), ()))

    def sub_block(sb, carry):
        q0 = pl.multiple_of(sb * sub, sub)
        base = base_tile + q0
        kstart = pl.multiple_of(jnp.maximum(base - sub, 0), sub)
        dist = (base + r_i) - (kstart + c_i)
        mask = (dist >= 0) & (dist <= DIL_SPAN)
        for pr in range(DIL_HEADS_PER_GROUP // 2):
            cs = slice(pr * LANES, (pr + 1) * LANES)
            q2 = q_ref[pl.ds(q0, sub), cs].astype(jnp.float32)
            k2 = k_ref[pl.ds(kstart, kw), cs]
            v2 = v_ref[pl.ds(kstart, kw), cs]
            outs, lses = [], []
            for h in range(2):
                qh = jnp.where(first if h == 0 else jnp.logical_not(first), q2, 0.0)
                qh = qh.astype(jnp.bfloat16)
                sc = lax.dot_general(qh, k2, nt, preferred_element_type=jnp.float32)
                sc = jnp.where(mask, sc, NEG)
                m = jnp.max(sc, axis=1, keepdims=True)
                p = jnp.exp(sc - m)
                l = jnp.sum(p, axis=1, keepdims=True)
                o = jnp.dot(p.astype(jnp.bfloat16), v2, preferred_element_type=jnp.float32)
                outs.append(o / l)
                lses.append(jnp.broadcast_to(m + jnp.log(l), (sub, LANES)))
            o_ref[pl.ds(q0, sub), cs] = jnp.where(first, outs[0], outs[1]).astype(o_ref.dtype)
            lse_ref[pl.ds(q0, sub), cs] = jnp.where(first, lses[0], lses[1])
        return carry

    lax.fori_loop(0, tq // sub, sub_block, 0, unroll=True)


def _dil_group(pg, b, dil):
    w = DIL_OUT_W
    l = pg.shape[0] // b
    pv = pg.reshape(b, l, dil * 3 * w)
    tq = min(DIL_TQ, l)
    o, lse = pl.pallas_call(
        _dil_kernel,
        grid=(b, dil, l // tq),
        in_specs=[pl.BlockSpec((None, tq, w), lambda i, r, t: (i, t, 3 * r)),
                  pl.BlockSpec((None, l, w), lambda i, r, t: (i, 0, 3 * r + 1)),
                  pl.BlockSpec((None, l, w), lambda i, r, t: (i, 0, 3 * r + 2))],
        out_specs=[pl.BlockSpec((None, tq, w), lambda i, r, t: (i, t, r)),
                   pl.BlockSpec((None, tq, w), lambda i, r, t: (i, t, r))],
        out_shape=[jax.ShapeDtypeStruct((b, l, dil * w), jnp.bfloat16),
                   jax.ShapeDtypeStruct((b, l, dil * w), jnp.float32)],
        compiler_params=_cparams(("arbitrary", "arbitrary", "arbitrary")),
        name=f"dil_attn_d{dil}",
    )(pv, pv, pv)
    return o, lse


def _layer_norm(y, g, b):
    mu = jnp.mean(y, axis=-1, keepdims=True)
    yc = y - mu
    var = jnp.mean(yc * yc, axis=-1, keepdims=True)
    return yc * lax.rsqrt(var + LN_EPS) * g + b


def _sigmoid(z):
    return 0.5 * jnp.tanh(0.5 * z) + 0.5


def _natural_order(src_ref, dst_ref, a, dil):
    rows = src_ref.shape[0]
    slabs = DIL_OUT_W // LANES
    for r in range(dil):
        for sl in range(slabs):
            c = r * DIL_OUT_W + sl * LANES
            dst_ref[a, sl, pl.ds(r, rows, stride=dil), :] = (
                src_ref[:, c:c + LANES].astype(jnp.float32))
    return jnp.concatenate([dst_ref[a, sl] for sl in range(slabs)], axis=1)


def _merge_kernel(alpha, x_ref, ym_ref, bg_ref, cg_ref, hc_ref, cgp_ref, hcp_ref,
                  g0_ref, g1_ref, g2_ref, o1_ref, o2_ref, o3_ref, l1_ref, l2_ref, l3_ref,
                  wm_ref, wc_ref, wd_ref, wo_ref, wsc_ref, lng_ref, lnb_ref, out_ref, nat_ref):
    f32 = jnp.float32
    tm = x_ref.shape[0]
    ch = cg_ref[...].astype(f32) * hc_ref[...].astype(f32)
    chp = cgp_ref[...].astype(f32) * hcp_ref[...].astype(f32)
    chp = jnp.where(pl.program_id(1) == 0, 0.0, chp)
    hp = chp.shape[0]
    prev1 = chp[hp - 1:hp, :]
    prev2 = chp[hp - 2:hp - 1, :]
    row = lax.broadcasted_iota(jnp.int32, (tm, CONV_WIDTH), 0)
    ch1 = jnp.where(row == 0, prev1, pltpu.roll(ch, 1, axis=0))
    ch2 = pltpu.roll(ch, 2, axis=0)
    ch2 = jnp.where(row == 0, prev2, jnp.where(row == 1, prev1, ch2))
    wsc = wsc_ref[...]
    conv = ch * wsc[0:1, :] + ch1 * wsc[1:2, :] + ch2 * wsc[2:3, :]
    y_conv = (bg_ref[...].astype(f32) * conv).astype(jnp.bfloat16)

    d2, d3 = DIL_GROUPS[1][1], DIL_GROUPS[2][1]
    o1, l1 = o1_ref[...].astype(f32), l1_ref[...]
    o2, l2 = _natural_order(o2_ref, nat_ref, 0, d2), _natural_order(l2_ref, nat_ref, 1, d2)
    o3, l3 = _natural_order(o3_ref, nat_ref, 2, d3), _natural_order(l3_ref, nat_ref, 3, d3)
    lm = jnp.maximum(jnp.maximum(l1, l2), l3)
    e1, e2, e3 = jnp.exp(l1 - lm), jnp.exp(l2 - lm), jnp.exp(l3 - lm)
    den = e1 + e2 + e3
    y_dil = ((e1 / den) * o1 + (e2 / den) * o2 + (e3 / den) * o3).astype(jnp.bfloat16)

    pm = jnp.dot(ym_ref[...], wm_ref[...], preferred_element_type=f32)
    pc = jnp.dot(y_conv, wc_ref[...], preferred_element_type=f32)
    pd = jnp.dot(y_dil, wd_ref[...], preferred_element_type=f32)
    merged = (_sigmoid(g0_ref[...].astype(f32)) * pm
              + _sigmoid(g1_ref[...].astype(f32)) * pc
              + _sigmoid(g2_ref[...].astype(f32)) * pd)
    mix = jnp.dot(merged.astype(jnp.bfloat16), wo_ref[...], preferred_element_type=f32)
    y = alpha * x_ref[...] + mix
    out_ref[...] = _layer_norm(y, lng_ref[...], lnb_ref[...])


def _merge(alpha, x3, ym, p3, dil_outs, wm, wc, wd, wo, wsc, lng, lnb):
    b, s, d = x3.shape
    tm = MERGE_TM
    hp = 16
    cw = CONV_WIDTH
    cblk = OFF_CONV // cw
    gblk = OFF_GATE // d
    row = lambda w: pl.BlockSpec((None, tm, w), lambda i, t: (i, t, 0))
    pcol = lambda w, c: pl.BlockSpec((None, tm, w), lambda i, t, c=c: (i, t, c))
    prev = lambda c: pl.BlockSpec(
        (None, hp, cw), lambda i, t, c=c: (i, jnp.maximum(t * (tm // hp) - 1, 0), c))
    full = lambda a: pl.BlockSpec(a.shape, lambda i, t: (0,) * a.ndim)
    (o1, l1), (o2, l2), (o3, l3) = dil_outs
    dils = [dil for _, dil in DIL_GROUPS]
    grp = lambda dil: pl.BlockSpec((None, tm // dil, dil * DIL_OUT_W), lambda i, t: (i, t, 0))
    in_specs = [row(d), row(MOBA_W),
                pcol(cw, cblk), pcol(cw, cblk + 1), pcol(cw, cblk + 2),
                prev(cblk + 1), prev(cblk + 2),
                pcol(d, gblk), pcol(d, gblk + 1), pcol(d, gblk + 2),
                grp(dils[0]), grp(dils[1]), grp(dils[2]),
                grp(dils[0]), grp(dils[1]), grp(dils[2]),
                full(wm), full(wc), full(wd), full(wo), full(wsc), full(lng), full(lnb)]
    return pl.pallas_call(
        functools.partial(_merge_kernel, alpha),
        grid=(b, s // tm),
        in_specs=in_specs,
        out_specs=row(d),
        out_shape=jax.ShapeDtypeStruct((b, s, d), jnp.float32),
        scratch_shapes=[pltpu.VMEM((4, DIL_OUT_W // LANES, tm, LANES), jnp.float32)],
        compiler_params=_cparams(("arbitrary", "arbitrary")),
        name="merge",
    )(x3, ym, p3, p3, p3, p3, p3, p3, p3, p3, o1, o2, o3, l1, l2, l3,
      wm, wc, wd, wo, wsc, lng, lnb)


def _ffn_kernel(alpha, x_ref, xp_ref, wup_ref, wdn_ref, wcv_ref, bcv_ref, lng_ref, lnb_ref,
                out_ref, xs_ref, act_ref):
    f32 = jnp.float32
    tm = x_ref.shape[0]
    halo = xp_ref.shape[0]
    xp = jnp.where(pl.program_id(1) == 0, 0.0, xp_ref[...])
    xs_ref[0:halo, :] = xp.astype(jnp.bfloat16)
    xs_ref[halo:halo + tm, :] = x_ref[...].astype(jnp.bfloat16)

    def conv(u, w):
        y = u * w[0:1, :] + pltpu.roll(u, 1, axis=0) * w[1:2, :] + pltpu.roll(u, 2, axis=0) * w[2:3, :]
        return y[halo:, :]

    for c in range(FFN_NCHUNK):
        gc = slice(c * FFN_CHUNK, (c + 1) * FFN_CHUNK)
        vc = slice(D_FF + c * FFN_CHUNK, D_FF + (c + 1) * FFN_CHUNK)
        xs = xs_ref[...]
        ug = jnp.dot(xs, wup_ref[:, gc], preferred_element_type=f32)
        uv = jnp.dot(xs, wup_ref[:, vc], preferred_element_type=f32)
        yg = conv(ug, wcv_ref[:, gc]) + bcv_ref[:, gc]
        yv = conv(uv, wcv_ref[:, vc]) + bcv_ref[:, vc]
        act_ref[:, gc] = (yg * _sigmoid(yg) * yv).astype(jnp.bfloat16)

    ffn = jnp.dot(act_ref[...], wdn_ref[...], preferred_element_type=f32)
    y = alpha * x_ref[...] + ffn
    out_ref[...] = _layer_norm(y, lng_ref[...], lnb_ref[...])


def _ffn(alpha, x3, wup, wdn, wcv, bcv, lng, lnb):
    b, s, d = x3.shape
    tm = FFN_TM
    halo = FFN_HALO
    full = lambda a: pl.BlockSpec(a.shape, lambda i, t: (0,) * a.ndim)
    return pl.pallas_call(
        functools.partial(_ffn_kernel, alpha),
        grid=(b, s // tm),
        in_specs=[pl.BlockSpec((None, tm, d), lambda i, t: (i, t, 0)),
                  pl.BlockSpec((None, halo, d),
                               lambda i, t: (i, jnp.maximum(t * (tm // halo) - 1, 0), 0)),
                  full(wup), full(wdn), full(wcv), full(bcv), full(lng), full(lnb)],
        out_specs=pl.BlockSpec((None, tm, d), lambda i, t: (i, t, 0)),
        out_shape=jax.ShapeDtypeStruct((b, s, d), jnp.float32),
        scratch_shapes=[pltpu.VMEM((tm + halo, d), jnp.bfloat16),
                        pltpu.VMEM((tm, D_FF), jnp.bfloat16)],
        compiler_params=_cparams(("arbitrary", "arbitrary")),
        name="conv_ffn",
    )(x3, x3, wup, wdn, wcv, bcv, lng, lnb)


def kernel(x, w_in, w_short_conv, w_moba_proj, w_dil_proj, w_conv_proj, w_mix_out,
           ln1_g, ln1_b, w_up, w_ffn_conv, b_ffn_conv, w_down, ln2_g, ln2_b):
    b, s, d = x.shape
    depth = w_in.shape[0]
    alpha = float((2 * depth) ** 0.25)
    bf = jnp.bfloat16
    for l in range(depth):
        x2 = x.reshape(b * s, d)
        p = _in_proj(x2, w_in, l).reshape(b, s, MAIN_COLS)
        pgs = _dil_proj(x2, w_in, l)
        ym = _moba(p)
        dil_outs = [_dil_group(pg, b, dil) for pg, (_, dil) in zip(pgs, DIL_GROUPS)]
        x = _merge(alpha, x, ym, p, dil_outs,
                   w_moba_proj[l].astype(bf), w_conv_proj[l].astype(bf),
                   w_dil_proj[l].astype(bf), w_mix_out[l].astype(bf),
                   w_short_conv[l], ln1_g[l].reshape(1, d), ln1_b[l].reshape(1, d))
        x = _ffn(alpha, x, w_up[l].astype(bf), w_down[l].astype(bf), w_ffn_conv[l],
                 b_ffn_conv[l].reshape(1, 2 * D_FF),
                 ln2_g[l].reshape(1, d), ln2_b[l].reshape(1, d))
    return x
```

```python
import functools

import jax
import jax.numpy as jnp
import numpy as np
from jax import lax
from jax.experimental import pallas as pl
from jax.experimental.pallas import tpu as pltpu

D_MODEL = 1024
HEAD_DIM = 64
MOBA_HEADS = 8
MOBA_BLOCK = 256
MOBA_TOPK = 3
MOBA_W = MOBA_HEADS * HEAD_DIM
DIL_GROUPS = ((128, 1), (512, 4), (2048, 16))
DIL_HEADS_PER_GROUP = 4
DIL_W = 3 * DIL_HEADS_PER_GROUP * HEAD_DIM
DIL_OUT_W = DIL_HEADS_PER_GROUP * HEAD_DIM
CONV_WIDTH = 512
N_BRANCH = 3
D_FF = 2816
LN_EPS = 1e-5

OFF_CONV = 3 * MOBA_W
OFF_GATE = OFF_CONV + 3 * CONV_WIDTH
MAIN_COLS = OFF_GATE + N_BRANCH * D_MODEL

LOG2E = float(np.log2(np.e))
LN2 = float(np.log(2.0))
Q_SCALE = HEAD_DIM ** -0.5 * LOG2E

LANES = 128
NEG = -1e30
VMEM_LIMIT = 56 * 1024 * 1024

PROJ_TM = 2048
PROJ_TN = 768
DILP_TM = 512
MOBA_KTILE = 2 * MOBA_BLOCK
MOBA_UNROLL = 2
MOBA_ONES = 16
DIL_TQ = 512
DIL_SPAN = 128
MERGE_TM = 512
FFN_TM = 512
FFN_HALO = 16
FFN_CHUNK = 256
FFN_NCHUNK = D_FF // FFN_CHUNK


def _cparams(sem):
    return pltpu.CompilerParams(dimension_semantics=sem, vmem_limit_bytes=VMEM_LIMIT)


def _proj_kernel(x_ref, w_ref, cs_ref, o_ref, xb_ref, wb_ref):
    j = pl.program_id(1)

    @pl.when(j == 0)
    def _():
        xb_ref[...] = x_ref[...].astype(jnp.bfloat16)

    @pl.when(pl.program_id(0) == 0)
    def _():
        wb_ref[j] = w_ref[...].astype(jnp.bfloat16)

    acc = jnp.dot(xb_ref[...], wb_ref[j], preferred_element_type=jnp.float32)
    o_ref[...] = (acc * cs_ref[...]).astype(o_ref.dtype)


def _q_scale_row(q_cols, cols):
    cs = np.ones((1, cols), np.float32)
    cs[:, :q_cols] = Q_SCALE
    return jnp.asarray(cs)


def _in_proj(x2, w_in, layer):
    n, d = x2.shape
    tn = PROJ_TN
    skip_at = 3 * MOBA_W // tn
    skip = 3 * DIL_W // tn
    n_ct = MAIN_COLS // tn
    src = lambda j: jnp.where(j < skip_at, j, j + skip)
    w_tile = lambda i, j: src(jnp.where(i == 0, j, n_ct - 1))
    return pl.pallas_call(
        _proj_kernel,
        grid=(n // PROJ_TM, n_ct),
        in_specs=[pl.BlockSpec((PROJ_TM, d), lambda i, j: (i, 0)),
                  pl.BlockSpec((None, d, tn), lambda i, j: (layer, 0, w_tile(i, j))),
                  pl.BlockSpec((1, tn), lambda i, j: (0, j))],
        out_specs=pl.BlockSpec((PROJ_TM, tn), lambda i, j: (i, j)),
        out_shape=jax.ShapeDtypeStruct((n, MAIN_COLS), jnp.bfloat16),
        scratch_shapes=[pltpu.VMEM((PROJ_TM, d), jnp.bfloat16),
                        pltpu.VMEM((n_ct, d, tn), jnp.bfloat16)],
        compiler_params=_cparams(("arbitrary", "arbitrary")),
        name="in_proj",
    )(x2, w_in, _q_scale_row(MOBA_W, MAIN_COLS))


def _dil_proj_kernel(x_ref, wq_ref, wk_ref, wv_ref, o0_ref, o1_ref, o2_ref, wb_ref, res_ref):
    tm = x_ref.shape[0]

    @pl.when(pl.program_id(0) == 0)
    def _():
        for j, w_ref in enumerate((wq_ref, wk_ref, wv_ref)):
            wb_ref[j] = w_ref[...].astype(jnp.bfloat16)

    xb = x_ref[...].astype(jnp.bfloat16)
    per_part = DIL_W // LANES
    for j in range(3):
        res = jnp.dot(xb, wb_ref[j], preferred_element_type=jnp.float32)
        if j == 0:
            res = res * Q_SCALE
        for cb in range(per_part):
            res_ref[j * per_part + cb] = res[:, cb * LANES:(cb + 1) * LANES]
    slabs = DIL_OUT_W // LANES
    for g, o_ref in enumerate((o0_ref, o1_ref, o2_ref)):
        d = DIL_GROUPS[g][1]
        rows = tm // d
        for r in range(d):
            for j in range(3):
                for sl in range(slabs):
                    src_cb = (j * DIL_W + g * DIL_OUT_W) // LANES + sl
                    dst_c = (r * 3 + j) * DIL_OUT_W + sl * LANES
                    if d == 1:
                        src = res_ref[src_cb]
                    else:
                        src = res_ref[src_cb, pl.ds(r, rows, stride=d), :]
                    o_ref[:, dst_c:dst_c + LANES] = src.astype(o_ref.dtype)


def _dil_proj(x2, w_in, layer):
    n, d = x2.shape
    tm = DILP_TM
    dils = [dil for _, dil in DIL_GROUPS]
    first = 3 * MOBA_W // DIL_W
    wspec = lambda j: pl.BlockSpec((None, d, DIL_W), lambda i, j=j: (layer, 0, first + j))
    return pl.pallas_call(
        _dil_proj_kernel,
        grid=(n // tm,),
        in_specs=[pl.BlockSpec((tm, d), lambda i: (i, 0)), wspec(0), wspec(1), wspec(2)],
        out_specs=[pl.BlockSpec((tm // dil, dil * 3 * DIL_OUT_W), lambda i: (i, 0))
                   for dil in dils],
        out_shape=[jax.ShapeDtypeStruct((n // dil, dil * 3 * DIL_OUT_W), jnp.bfloat16)
                   for dil in dils],
        scratch_shapes=[pltpu.VMEM((3, d, DIL_W), jnp.bfloat16),
                        pltpu.VMEM((3 * DIL_W // LANES, tm, LANES), jnp.float32)],
        compiler_params=_cparams(("arbitrary",)),
        name="dil_proj",
    )(x2, w_in, w_in, w_in)


def _moba_kernel(q_ref, k_ref, v_ref, o_ref, kaug_ref, vaug_ref, km_ref, sc_ref):
    s = q_ref.shape[0]
    nb = s // MOBA_BLOCK
    tq = kt = MOBA_KTILE
    f32 = jnp.float32
    bf = jnp.bfloat16
    lane_s = lax.broadcasted_iota(jnp.int32, (s, LANES), 1)
    row_blk = lax.broadcasted_iota(jnp.int32, (s, LANES), 0) // MOBA_BLOCK
    k2 = k_ref[...].astype(f32)
    kaug_ref[0] = jnp.where(lane_s < HEAD_DIM, k2,
                            jnp.where(lane_s - HEAD_DIM == row_blk, 1.0, 0.0)).astype(bf)
    kaug_ref[1] = jnp.where(lane_s >= HEAD_DIM, k2,
                            jnp.where(lane_s == row_blk, 1.0, 0.0)).astype(bf)
    vr = HEAD_DIM + MOBA_ONES
    lane_k = lax.broadcasted_iota(jnp.int32, (kt, LANES), 1)
    for t in range(s // kt):
        vt = v_ref[t * kt:(t + 1) * kt, :].astype(f32)
        vaug_ref[0, t] = jnp.where(lane_k < HEAD_DIM, vt, 1.0).T[:vr].astype(bf)
        vaug_ref[1, t] = jnp.where(lane_k >= HEAD_DIM, vt, 1.0).T[LANES - vr:].astype(bf)

    offs = (HEAD_DIM, 0)
    ksum = jnp.sum(k2.reshape(nb, MOBA_BLOCK, LANES), axis=1)
    kmean = ksum * (1.0 / MOBA_BLOCK)
    lane_b = lax.broadcasted_iota(jnp.int32, (nb, LANES), 1)
    km0 = jnp.where(lane_b < HEAD_DIM, kmean, 0.0)
    km1 = jnp.where(lane_b >= HEAD_DIM, kmean, 0.0)
    zgap = jnp.zeros((HEAD_DIM - nb, LANES), f32)
    km_ref[...] = jnp.concatenate([km1, zgap, km0, zgap], axis=0).astype(bf)

    lane_t = lax.broadcasted_iota(jnp.int32, (tq, LANES), 1)
    causal = (lax.broadcasted_iota(jnp.int32, (kt, tq), 0)
              <= lax.broadcasted_iota(jnp.int32, (kt, tq), 1))
    blk_i = lax.broadcasted_iota(jnp.int32, (nb, tq), 0)
    blk_f = blk_i.astype(f32)
    q_sub = lax.broadcasted_iota(jnp.int32, (nb, tq), 1) // MOBA_BLOCK
    zgap_t = jnp.zeros((HEAD_DIM - nb, tq), f32)
    nt = (((1,), (1,)), ((), ()))
    heads = (0, 1)
    q_lanes = (lane_t < HEAD_DIM, lane_t >= HEAD_DIM)

    def q_tile(t_last, carry):
        r0 = pl.multiple_of(t_last * tq, tq)
        q2 = q_ref[pl.ds(r0, tq), :]
        q2f = q2.astype(f32)
        gate_t = lax.dot_general(km_ref[...], q2, nt, preferred_element_type=f32)
        j = t_last * (tq // MOBA_BLOCK) + q_sub
        valid = blk_i < j
        pens = []
        for h in heads:
            g = jnp.where(valid, gate_t[offs[h]:offs[h] + nb, :], NEG)
            sel = jnp.zeros((nb, tq), jnp.bool_)
            for _ in range(MOBA_TOPK):
                m = jnp.max(g, axis=0, keepdims=True)
                first = jnp.min(jnp.where(g == m, blk_f, 1e9), axis=0, keepdims=True)
                pick = blk_f == first
                sel = sel | pick
                g = jnp.where(pick, NEG, g)
            keep = (sel & valid) | (blk_i == j)
            pens.append(jnp.where(keep, 0.0, NEG))
        pen = jnp.concatenate([pens[1], zgap_t, pens[0], zgap_t], axis=0).T
        qas = []
        for h in heads:
            qa = jnp.where(q_lanes[h], q2f, pen).astype(bf)
            sc = lax.dot_general(kaug_ref[h, pl.ds(r0, kt), :], qa, nt,
                                 preferred_element_type=f32)
            sc_ref[h] = jnp.where(causal, sc, NEG)
            qas.append(qa)

        def tile_of(visit):
            return jnp.where(visit <= 0, t_last, visit - 1)

        def kv_step(i, mc):
            t_pv = tile_of(i)
            k0 = pl.multiple_of(jnp.clip(i, 0, jnp.maximum(t_last - 1, 0)) * kt, kt)
            out = []
            for h in heads:
                m_old, acc = mc[2 * h], mc[2 * h + 1]
                sc = sc_ref[h]
                sc_next = lax.dot_general(kaug_ref[h, pl.ds(k0, kt), :], qas[h], nt,
                                          preferred_element_type=f32)
                m_new = jnp.maximum(m_old, jnp.max(sc, axis=0, keepdims=True))
                alpha = jnp.exp2(m_old - m_new)
                p = jnp.exp2(sc - m_new).astype(bf)
                pv = jnp.dot(vaug_ref[h, t_pv], p, preferred_element_type=f32)
                sc_ref[h] = sc_next
                out += [m_new, acc * alpha + pv]
            return tuple(out)

        m_init = jnp.full((1, tq), NEG, f32)
        acc_init = jnp.zeros((vr, tq), f32)
        def multi_step(ii, mc):
            for u in range(MOBA_UNROLL):
                mc = kv_step(MOBA_UNROLL * ii + u, mc)
            return mc

        n_steps = t_last + 1
        n_main = n_steps // MOBA_UNROLL
        mc = lax.fori_loop(0, n_main, multi_step, (m_init, acc_init, m_init, acc_init))
        fin = lax.fori_loop(n_main * MOBA_UNROLL, n_steps, kv_step, mc)
        o0 = fin[1][:HEAD_DIM] / fin[1][HEAD_DIM:HEAD_DIM + 1]
        o1 = fin[3][MOBA_ONES:] / fin[3][0:1]
        o_ref[pl.ds(r0, tq), :] = jnp.concatenate([o0, o1], axis=0).T.astype(o_ref.dtype)
        return carry

    lax.fori_loop(0, s // tq, q_tile, 0)


def _moba(p3):
    b, s, _ = p3.shape
    n_pairs = MOBA_HEADS // 2
    blk = (None, s, LANES)
    return pl.pallas_call(
        _moba_kernel,
        grid=(b, n_pairs),
        in_specs=[pl.BlockSpec(blk, lambda i, j: (i, 0, j)),
                  pl.BlockSpec(blk, lambda i, j: (i, 0, n_pairs + j)),
                  pl.BlockSpec(blk, lambda i, j: (i, 0, 2 * n_pairs + j))],
        out_specs=pl.BlockSpec(blk, lambda i, j: (i, 0, j)),
        out_shape=jax.ShapeDtypeStruct((b, s, MOBA_W), jnp.bfloat16),
        scratch_shapes=[pltpu.VMEM((2, s, LANES), jnp.bfloat16),
                        pltpu.VMEM((2, s // MOBA_KTILE, HEAD_DIM + MOBA_ONES, MOBA_KTILE),
                                   jnp.bfloat16),
                        pltpu.VMEM((LANES, LANES), jnp.bfloat16),
                        pltpu.VMEM((2, MOBA_KTILE, MOBA_KTILE), jnp.float32)],
        compiler_params=_cparams(("arbitrary", "arbitrary")),
        name="moba_attn",
    )(p3, p3, p3)


def _dil_kernel(q_ref, k_ref, v_ref, o_ref, lse_ref):
    tq = q_ref.shape[0]
    sub = DIL_SPAN
    kw = 2 * sub
    base_tile = pl.program_id(2) * tq
    lane = lax.broadcasted_iota(jnp.int32, (sub, LANES), 1)
    first = lane < HEAD_DIM
    r_i = lax.broadcasted_iota(jnp.int32, (sub, kw), 0)
    c_i = lax.broadcasted_iota(jnp.int32, (sub, kw), 1)
    nt = (((1,), (1,)), ((), ()))

    def sub_block(sb, carry):
        q0 = pl.multiple_of(sb * sub, sub)
        base = base_tile + q0
        kstart = pl.multiple_of(jnp.maximum(base - sub, 0), sub)
        dist = (base + r_i) - (kstart + c_i)
        mask = (dist >= 0) & (dist <= DIL_SPAN)
        for pr in range(DIL_HEADS_PER_GROUP // 2):
            cs = slice(pr * LANES, (pr + 1) * LANES)
            q2 = q_ref[pl.ds(q0, sub), cs].astype(jnp.float32)
            k2 = k_ref[pl.ds(kstart, kw), cs]
            v2 = v_ref[pl.ds(kstart, kw), cs]
            outs, lses = [], []
            for h in range(2):
                qh = jnp.where(first if h == 0 else jnp.logical_not(first), q2, 0.0)
                qh = qh.astype(jnp.bfloat16)
                sc = lax.dot_general(qh, k2, nt, preferred_element_type=jnp.float32)
                sc = jnp.where(mask, sc, NEG)
                m = jnp.max(sc, axis=1, keepdims=True)
                p = jnp.exp2(sc - m)
                l = jnp.sum(p, axis=1, keepdims=True)
                o = jnp.dot(p.astype(jnp.bfloat16), v2, preferred_element_type=jnp.float32)
                outs.append(o / l)
                lses.append(jnp.broadcast_to(m * LN2 + jnp.log(l), (sub, LANES)))
            o_ref[pl.ds(q0, sub), cs] = jnp.where(first, outs[0], outs[1]).astype(o_ref.dtype)
            lse_ref[pl.ds(q0, sub), cs] = jnp.where(first, lses[0], lses[1])
        return carry

    lax.fori_loop(0, tq // sub, sub_block, 0, unroll=True)


def _dil_group(pg, b, dil):
    w = DIL_OUT_W
    l = pg.shape[0] // b
    pv = pg.reshape(b, l, dil * 3 * w)
    tq = min(DIL_TQ, l)
    o, lse = pl.pallas_call(
        _dil_kernel,
        grid=(b, dil, l // tq),
        in_specs=[pl.BlockSpec((None, tq, w), lambda i, r, t: (i, t, 3 * r)),
                  pl.BlockSpec((None, l, w), lambda i, r, t: (i, 0, 3 * r + 1)),
                  pl.BlockSpec((None, l, w), lambda i, r, t: (i, 0, 3 * r + 2))],
        out_specs=[pl.BlockSpec((None, tq, w), lambda i, r, t: (i, t, r)),
                   pl.BlockSpec((None, tq, w), lambda i, r, t: (i, t, r))],
        out_shape=[jax.ShapeDtypeStruct((b, l, dil * w), jnp.bfloat16),
                   jax.ShapeDtypeStruct((b, l, dil * w), jnp.float32)],
        compiler_params=_cparams(("arbitrary", "arbitrary", "arbitrary")),
        name=f"dil_attn_d{dil}",
    )(pv, pv, pv)
    return o, lse


def _layer_norm(y, g, b):
    mu = jnp.mean(y, axis=-1, keepdims=True)
    yc = y - mu
    var = jnp.mean(yc * yc, axis=-1, keepdims=True)
    return yc * lax.rsqrt(var + LN_EPS) * g + b


def _sigmoid(z):
    return 0.5 * jnp.tanh(0.5 * z) + 0.5


def _natural_order(src_ref, dst_ref, a, dil):
    rows = src_ref.shape[0]
    slabs = DIL_OUT_W // LANES
    for r in range(dil):
        for sl in range(slabs):
            c = r * DIL_OUT_W + sl * LANES
            dst_ref[a, sl, pl.ds(r, rows, stride=dil), :] = (
                src_ref[:, c:c + LANES].astype(jnp.float32))
    return jnp.concatenate([dst_ref[a, sl] for sl in range(slabs)], axis=1)


def _merge_kernel(alpha, x_ref, ym_ref, bg_ref, cg_ref, hc_ref, cgp_ref, hcp_ref,
                  g0_ref, g1_ref, g2_ref, o1_ref, o2_ref, o3_ref, l1_ref, l2_ref, l3_ref,
                  wm_ref, wc_ref, wd_ref, wo_ref, wsc_ref, lng_ref, lnb_ref, out_ref, nat_ref):
    f32 = jnp.float32
    tm = x_ref.shape[0]
    ch = cg_ref[...].astype(f32) * hc_ref[...].astype(f32)
    chp = cgp_ref[...].astype(f32) * hcp_ref[...].astype(f32)
    chp = jnp.where(pl.program_id(1) == 0, 0.0, chp)
    hp = chp.shape[0]
    prev1 = chp[hp - 1:hp, :]
    prev2 = chp[hp - 2:hp - 1, :]
    row = lax.broadcasted_iota(jnp.int32, (tm, CONV_WIDTH), 0)
    ch1 = jnp.where(row == 0, prev1, pltpu.roll(ch, 1, axis=0))
    ch2 = pltpu.roll(ch, 2, axis=0)
    ch2 = jnp.where(row == 0, prev2, jnp.where(row == 1, prev1, ch2))
    wsc = wsc_ref[...]
    conv = ch * wsc[0:1, :] + ch1 * wsc[1:2, :] + ch2 * wsc[2:3, :]
    y_conv = (bg_ref[...].astype(f32) * conv).astype(jnp.bfloat16)

    d2, d3 = DIL_GROUPS[1][1], DIL_GROUPS[2][1]
    o1, l1 = o1_ref[...].astype(f32), l1_ref[...]
    o2, l2 = _natural_order(o2_ref, nat_ref, 0, d2), _natural_order(l2_ref, nat_ref, 1, d2)
    o3, l3 = _natural_order(o3_ref, nat_ref, 2, d3), _natural_order(l3_ref, nat_ref, 3, d3)
    lm = jnp.maximum(jnp.maximum(l1, l2), l3)
    e1, e2, e3 = jnp.exp(l1 - lm), jnp.exp(l2 - lm), jnp.exp(l3 - lm)
    y_dil = ((e1 * o1 + e2 * o2 + e3 * o3) / (e1 + e2 + e3)).astype(jnp.bfloat16)

    pm = jnp.dot(ym_ref[...], wm_ref[...], preferred_element_type=f32)
    pc = jnp.dot(y_conv, wc_ref[...], preferred_element_type=f32)
    pd = jnp.dot(y_dil, wd_ref[...], preferred_element_type=f32)
    merged = (_sigmoid(g0_ref[...].astype(f32)) * pm
              + _sigmoid(g1_ref[...].astype(f32)) * pc
              + _sigmoid(g2_ref[...].astype(f32)) * pd)
    mix = jnp.dot(merged.astype(jnp.bfloat16), wo_ref[...], preferred_element_type=f32)
    y = alpha * x_ref[...] + mix
    out_ref[...] = _layer_norm(y, lng_ref[...], lnb_ref[...])


def _merge(alpha, x3, ym, p3, dil_outs, wm, wc, wd, wo, wsc, lng, lnb):
    b, s, d = x3.shape
    tm = MERGE_TM
    hp = 16
    cw = CONV_WIDTH
    cblk = OFF_CONV // cw
    gblk = OFF_GATE // d
    row = lambda w: pl.BlockSpec((None, tm, w), lambda i, t: (i, t, 0))
    pcol = lambda w, c: pl.BlockSpec((None, tm, w), lambda i, t, c=c: (i, t, c))
    prev = lambda c: pl.BlockSpec(
        (None, hp, cw), lambda i, t, c=c: (i, jnp.maximum(t * (tm // hp) - 1, 0), c))
    full = lambda a: pl.BlockSpec(a.shape, lambda i, t: (0,) * a.ndim)
    (o1, l1), (o2, l2), (o3, l3) = dil_outs
    dils = [dil for _, dil in DIL_GROUPS]
    grp = lambda dil: pl.BlockSpec((None, tm // dil, dil * DIL_OUT_W), lambda i, t: (i, t, 0))
    in_specs = [row(d), row(MOBA_W),
                pcol(cw, cblk), pcol(cw, cblk + 1), pcol(cw, cblk + 2),
                prev(cblk + 1), prev(cblk + 2),
                pcol(d, gblk), pcol(d, gblk + 1), pcol(d, gblk + 2),
                grp(dils[0]), grp(dils[1]), grp(dils[2]),
                grp(dils[0]), grp(dils[1]), grp(dils[2]),
                full(wm), full(wc), full(wd), full(wo), full(wsc), full(lng), full(lnb)]
    return pl.pallas_call(
        functools.partial(_merge_kernel, alpha),
        grid=(b, s // tm),
        in_specs=in_specs,
        out_specs=row(d),
        out_shape=jax.ShapeDtypeStruct((b, s, d), jnp.float32),
        scratch_shapes=[pltpu.VMEM((4, DIL_OUT_W // LANES, tm, LANES), jnp.float32)],
        compiler_params=_cparams(("arbitrary", "arbitrary")),
        name="merge",
    )(x3, ym, p3, p3, p3, p3, p3, p3, p3, p3, o1, o2, o3, l1, l2, l3,
      wm, wc, wd, wo, wsc, lng, lnb)


def _ffn_kernel(alpha, x_ref, xp_ref, wup_ref, wdn_ref, wcv_ref, bcv_ref, lng_ref, lnb_ref,
                out_ref, xs_ref, act_ref):
    f32 = jnp.float32
    tm = x_ref.shape[0]
    halo = xp_ref.shape[0]
    xp = jnp.where(pl.program_id(1) == 0, 0.0, xp_ref[...])
    xs_ref[0:halo, :] = xp.astype(jnp.bfloat16)
    xs_ref[halo:halo + tm, :] = x_ref[...].astype(jnp.bfloat16)

    def conv(u, w):
        y = u * w[0:1, :] + pltpu.roll(u, 1, axis=0) * w[1:2, :] + pltpu.roll(u, 2, axis=0) * w[2:3, :]
        return y[halo:, :]

    for c in range(FFN_NCHUNK):
        gc = slice(c * FFN_CHUNK, (c + 1) * FFN_CHUNK)
        vc = slice(D_FF + c * FFN_CHUNK, D_FF + (c + 1) * FFN_CHUNK)
        xs = xs_ref[...]
        ug = jnp.dot(xs, wup_ref[:, gc], preferred_element_type=f32)
        uv = jnp.dot(xs, wup_ref[:, vc], preferred_element_type=f32)
        yg = conv(ug, wcv_ref[:, gc]) + bcv_ref[:, gc]
        yv = conv(uv, wcv_ref[:, vc]) + bcv_ref[:, vc]
        act_ref[:, gc] = (yg * _sigmoid(yg) * yv).astype(jnp.bfloat16)

    ffn = jnp.dot(act_ref[...], wdn_ref[...], preferred_element_type=f32)
    y = alpha * x_ref[...] + ffn
    out_ref[...] = _layer_norm(y, lng_ref[...], lnb_ref[...])


def _ffn(alpha, x3, wup, wdn, wcv, bcv, lng, lnb):
    b, s, d = x3.shape
    tm = FFN_TM
    halo = FFN_HALO
    full = lambda a: pl.BlockSpec(a.shape, lambda i, t: (0,) * a.ndim)
    return pl.pallas_call(
        functools.partial(_ffn_kernel, alpha),
        grid=(b, s // tm),
        in_specs=[pl.BlockSpec((None, tm, d), lambda i, t: (i, t, 0)),
                  pl.BlockSpec((None, halo, d),
                               lambda i, t: (i, jnp.maximum(t * (tm // halo) - 1, 0), 0)),
                  full(wup), full(wdn), full(wcv), full(bcv), full(lng), full(lnb)],
        out_specs=pl.BlockSpec((None, tm, d), lambda i, t: (i, t, 0)),
        out_shape=jax.ShapeDtypeStruct((b, s, d), jnp.float32),
        scratch_shapes=[pltpu.VMEM((tm + halo, d), jnp.bfloat16),
                        pltpu.VMEM((tm, D_FF), jnp.bfloat16)],
        compiler_params=_cparams(("arbitrary", "arbitrary")),
        name="conv_ffn",
    )(x3, x3, wup, wdn, wcv, bcv, lng, lnb)


def kernel(x, w_in, w_short_conv, w_moba_proj, w_dil_proj, w_conv_proj, w_mix_out,
           ln1_g, ln1_b, w_up, w_ffn_conv, b_ffn_conv, w_down, ln2_g, ln2_b):
    b, s, d = x.shape
    depth = w_in.shape[0]
    alpha = float((2 * depth) ** 0.25)
    bf = jnp.bfloat16
    for l in range(depth):
        x2 = x.reshape(b * s, d)
        p = _in_proj(x2, w_in, l).reshape(b, s, MAIN_COLS)
        pgs = _dil_proj(x2, w_in, l)
        ym = _moba(p)
        dil_outs = [_dil_group(pg, b, dil) for pg, (_, dil) in zip(pgs, DIL_GROUPS)]
        x = _merge(alpha, x, ym, p, dil_outs,
                   w_moba_proj[l].astype(bf), w_conv_proj[l].astype(bf),
                   w_dil_proj[l].astype(bf), w_mix_out[l].astype(bf),
                   w_short_conv[l], ln1_g[l].reshape(1, d), ln1_b[l].reshape(1, d))
        x = _ffn(alpha, x, w_up[l].astype(bf), w_down[l].astype(bf), w_ffn_conv[l],
                 b_ffn_conv[l].reshape(1, 2 * D_FF),
                 ln2_g[l].reshape(1, d), ln2_b[l].reshape(1, d))
    return x
```

```python
import functools

import jax
import jax.numpy as jnp
import numpy as np
from jax import lax
from jax.experimental import pallas as pl
from jax.experimental.pallas import tpu as pltpu

D_MODEL = 1024
HEAD_DIM = 64
MOBA_HEADS = 8
MOBA_BLOCK = 256
MOBA_TOPK = 3
MOBA_W = MOBA_HEADS * HEAD_DIM
DIL_GROUPS = ((128, 1), (512, 4), (2048, 16))
DIL_HEADS_PER_GROUP = 4
DIL_W = 3 * DIL_HEADS_PER_GROUP * HEAD_DIM
DIL_OUT_W = DIL_HEADS_PER_GROUP * HEAD_DIM
CONV_WIDTH = 512
N_BRANCH = 3
D_FF = 2816
LN_EPS = 1e-5

OFF_CONV = 3 * MOBA_W
OFF_GATE = OFF_CONV + 3 * CONV_WIDTH
MAIN_COLS = OFF_GATE + N_BRANCH * D_MODEL

LOG2E = float(np.log2(np.e))
LN2 = float(np.log(2.0))
Q_SCALE = HEAD_DIM ** -0.5 * LOG2E

LANES = 128
NEG = -1e30
VMEM_LIMIT = 56 * 1024 * 1024

PROJ_TM = 2048
PROJ_TN = 768
DILP_TM = 512
MOBA_KTILE = 2 * MOBA_BLOCK
MOBA_UNROLL = 2
MOBA_ONES = 16
DIL_TQ = 512
DIL_SPAN = 128
MERGE_TM = 512
FFN_TM = 512
FFN_HALO = 16
FFN_CHUNK = 256
FFN_NCHUNK = D_FF // FFN_CHUNK


def _cparams(sem):
    return pltpu.CompilerParams(dimension_semantics=sem, vmem_limit_bytes=VMEM_LIMIT)


def _proj_kernel(x_ref, w_ref, cs_ref, o_ref, xb_ref, wb_ref):
    j = pl.program_id(1)

    @pl.when(j == 0)
    def _():
        xb_ref[...] = x_ref[...].astype(jnp.bfloat16)

    @pl.when(pl.program_id(0) == 0)
    def _():
        wb_ref[j] = w_ref[...].astype(jnp.bfloat16)

    acc = jnp.dot(xb_ref[...], wb_ref[j], preferred_element_type=jnp.float32)
    o_ref[...] = (acc * cs_ref[...]).astype(o_ref.dtype)


def _main_scale_row():
    cs = np.ones((1, MAIN_COLS), np.float32)
    cs[:, :MOBA_W] = Q_SCALE
    cs[:, OFF_GATE:] = 0.5
    return jnp.asarray(cs)


def _in_proj(x2, w_in, layer):
    n, d = x2.shape
    tn = PROJ_TN
    skip_at = 3 * MOBA_W // tn
    skip = 3 * DIL_W // tn
    n_ct = MAIN_COLS // tn
    src = lambda j: jnp.where(j < skip_at, j, j + skip)
    w_tile = lambda i, j: src(jnp.where(i == 0, j, n_ct - 1))
    return pl.pallas_call(
        _proj_kernel,
        grid=(n // PROJ_TM, n_ct),
        in_specs=[pl.BlockSpec((PROJ_TM, d), lambda i, j: (i, 0)),
                  pl.BlockSpec((None, d, tn), lambda i, j: (layer, 0, w_tile(i, j))),
                  pl.BlockSpec((1, tn), lambda i, j: (0, j))],
        out_specs=pl.BlockSpec((PROJ_TM, tn), lambda i, j: (i, j)),
        out_shape=jax.ShapeDtypeStruct((n, MAIN_COLS), jnp.bfloat16),
        scratch_shapes=[pltpu.VMEM((PROJ_TM, d), jnp.bfloat16),
                        pltpu.VMEM((n_ct, d, tn), jnp.bfloat16)],
        compiler_params=_cparams(("arbitrary", "arbitrary")),
        name="in_proj",
    )(x2, w_in, _main_scale_row())


def _dil_proj_kernel(x_ref, wq_ref, wk_ref, wv_ref, o0_ref, o1_ref, o2_ref, wb_ref, res_ref):
    tm = x_ref.shape[0]

    @pl.when(pl.program_id(0) == 0)
    def _():
        for j, w_ref in enumerate((wq_ref, wk_ref, wv_ref)):
            wb_ref[j] = w_ref[...].astype(jnp.bfloat16)

    xb = x_ref[...].astype(jnp.bfloat16)
    per_part = DIL_W // LANES
    for j in range(3):
        res = jnp.dot(xb, wb_ref[j], preferred_element_type=jnp.float32)
        if j == 0:
            res = res * Q_SCALE
        for cb in range(per_part):
            res_ref[j * per_part + cb] = res[:, cb * LANES:(cb + 1) * LANES]
    slabs = DIL_OUT_W // LANES
    for g, o_ref in enumerate((o0_ref, o1_ref, o2_ref)):
        d = DIL_GROUPS[g][1]
        rows = tm // d
        for r in range(d):
            for j in range(3):
                for sl in range(slabs):
                    src_cb = (j * DIL_W + g * DIL_OUT_W) // LANES + sl
                    dst_c = (r * 3 + j) * DIL_OUT_W + sl * LANES
                    if d == 1:
                        src = res_ref[src_cb]
                    else:
                        src = res_ref[src_cb, pl.ds(r, rows, stride=d), :]
                    o_ref[:, dst_c:dst_c + LANES] = src.astype(o_ref.dtype)


def _dil_proj(x2, w_in, layer):
    n, d = x2.shape
    tm = DILP_TM
    dils = [dil for _, dil in DIL_GROUPS]
    first = 3 * MOBA_W // DIL_W
    wspec = lambda j: pl.BlockSpec((None, d, DIL_W), lambda i, j=j: (layer, 0, first + j))
    return pl.pallas_call(
        _dil_proj_kernel,
        grid=(n // tm,),
        in_specs=[pl.BlockSpec((tm, d), lambda i: (i, 0)), wspec(0), wspec(1), wspec(2)],
        out_specs=[pl.BlockSpec((tm // dil, dil * 3 * DIL_OUT_W), lambda i: (i, 0))
                   for dil in dils],
        out_shape=[jax.ShapeDtypeStruct((n // dil, dil * 3 * DIL_OUT_W), jnp.bfloat16)
                   for dil in dils],
        scratch_shapes=[pltpu.VMEM((3, d, DIL_W), jnp.bfloat16),
                        pltpu.VMEM((3 * DIL_W // LANES, tm, LANES), jnp.float32)],
        compiler_params=_cparams(("arbitrary",)),
        name="dil_proj",
    )(x2, w_in, w_in, w_in)


def _moba_kernel(q_ref, k_ref, v_ref, o_ref, kaug_ref, vaug_ref, km_ref, sc_ref):
    s = q_ref.shape[0]
    nb = s // MOBA_BLOCK
    tq = kt = MOBA_KTILE
    f32 = jnp.float32
    bf = jnp.bfloat16
    lane_s = lax.broadcasted_iota(jnp.int32, (s, LANES), 1)
    row_blk = lax.broadcasted_iota(jnp.int32, (s, LANES), 0) // MOBA_BLOCK
    k2 = k_ref[...].astype(f32)
    kaug_ref[0] = jnp.where(lane_s < HEAD_DIM, k2,
                            jnp.where(lane_s - HEAD_DIM == row_blk, 1.0, 0.0)).astype(bf)
    kaug_ref[1] = jnp.where(lane_s >= HEAD_DIM, k2,
                            jnp.where(lane_s == row_blk, 1.0, 0.0)).astype(bf)
    vr = HEAD_DIM + MOBA_ONES
    lane_k = lax.broadcasted_iota(jnp.int32, (kt, LANES), 1)
    for t in range(s // kt):
        vt = v_ref[t * kt:(t + 1) * kt, :].astype(f32)
        vaug_ref[0, t] = jnp.where(lane_k < HEAD_DIM, vt, 1.0).T[:vr].astype(bf)
        vaug_ref[1, t] = jnp.where(lane_k >= HEAD_DIM, vt, 1.0).T[LANES - vr:].astype(bf)

    offs = (HEAD_DIM, 0)
    ksum = jnp.sum(k2.reshape(nb, MOBA_BLOCK, LANES), axis=1)
    kmean = ksum * (1.0 / MOBA_BLOCK)
    lane_b = lax.broadcasted_iota(jnp.int32, (nb, LANES), 1)
    km0 = jnp.where(lane_b < HEAD_DIM, kmean, 0.0)
    km1 = jnp.where(lane_b >= HEAD_DIM, kmean, 0.0)
    zgap = jnp.zeros((HEAD_DIM - nb, LANES), f32)
    km_ref[...] = jnp.concatenate([km1, zgap, km0, zgap], axis=0).astype(bf)

    lane_t = lax.broadcasted_iota(jnp.int32, (tq, LANES), 1)
    causal = (lax.broadcasted_iota(jnp.int32, (kt, tq), 0)
              <= lax.broadcasted_iota(jnp.int32, (kt, tq), 1))
    blk_i = lax.broadcasted_iota(jnp.int32, (nb, tq), 0)
    blk_f = blk_i.astype(f32)
    q_sub = lax.broadcasted_iota(jnp.int32, (nb, tq), 1) // MOBA_BLOCK
    zgap_t = jnp.zeros((HEAD_DIM - nb, tq), f32)
    nt = (((1,), (1,)), ((), ()))
    heads = (0, 1)
    q_lanes = (lane_t < HEAD_DIM, lane_t >= HEAD_DIM)

    def q_tile(t_last, carry):
        r0 = pl.multiple_of(t_last * tq, tq)
        q2 = q_ref[pl.ds(r0, tq), :]
        q2f = q2.astype(f32)
        gate_t = lax.dot_general(km_ref[...], q2, nt, preferred_element_type=f32)
        j = t_last * (tq // MOBA_BLOCK) + q_sub
        valid = blk_i < j
        pens = []
        for h in heads:
            g = jnp.where(valid, gate_t[offs[h]:offs[h] + nb, :], NEG)
            sel = jnp.zeros((nb, tq), jnp.bool_)
            for _ in range(MOBA_TOPK):
                m = jnp.max(g, axis=0, keepdims=True)
                first = jnp.min(jnp.where(g == m, blk_f, 1e9), axis=0, keepdims=True)
                pick = blk_f == first
                sel = sel | pick
                g = jnp.where(pick, NEG, g)
            keep = (sel & valid) | (blk_i == j)
            pens.append(jnp.where(keep, 0.0, NEG))
        pen = jnp.concatenate([pens[1], zgap_t, pens[0], zgap_t], axis=0).T
        qas = []
        for h in heads:
            qa = jnp.where(q_lanes[h], q2f, pen).astype(bf)
            sc = lax.dot_general(kaug_ref[h, pl.ds(r0, kt), :], qa, nt,
                                 preferred_element_type=f32)
            sc_ref[h] = jnp.where(causal, sc, NEG)
            qas.append(qa)

        def tile_of(visit):
            return jnp.where(visit <= 0, t_last, visit - 1)

        def kv_step(i, mc):
            t_pv = tile_of(i)
            k0 = pl.multiple_of(jnp.clip(i, 0, jnp.maximum(t_last - 1, 0)) * kt, kt)
            out = []
            for h in heads:
                m_old, acc = mc[2 * h], mc[2 * h + 1]
                sc = sc_ref[h]
                sc_next = lax.dot_general(kaug_ref[h, pl.ds(k0, kt), :], qas[h], nt,
                                          preferred_element_type=f32)
                m_new = jnp.maximum(m_old, jnp.max(sc, axis=0, keepdims=True))
                alpha = jnp.exp2(m_old - m_new)
                p = jnp.exp2(sc - m_new).astype(bf)
                pv = jnp.dot(vaug_ref[h, t_pv], p, preferred_element_type=f32)
                sc_ref[h] = sc_next
                out += [m_new, acc * alpha + pv]
            return tuple(out)

        m_init = jnp.full((1, tq), NEG, f32)
        acc_init = jnp.zeros((vr, tq), f32)
        def multi_step(ii, mc):
            for u in range(MOBA_UNROLL):
                mc = kv_step(MOBA_UNROLL * ii + u, mc)
            return mc

        n_steps = t_last + 1
        n_main = n_steps // MOBA_UNROLL
        mc = lax.fori_loop(0, n_main, multi_step, (m_init, acc_init, m_init, acc_init))
        fin = lax.fori_loop(n_main * MOBA_UNROLL, n_steps, kv_step, mc)
        o0 = fin[1][:HEAD_DIM] / fin[1][HEAD_DIM:HEAD_DIM + 1]
        o1 = fin[3][MOBA_ONES:] / fin[3][0:1]
        o_ref[pl.ds(r0, tq), :] = jnp.concatenate([o0, o1], axis=0).T.astype(o_ref.dtype)
        return carry

    lax.fori_loop(0, s // tq, q_tile, 0)


def _moba(p3):
    b, s, _ = p3.shape
    n_pairs = MOBA_HEADS // 2
    blk = (None, s, LANES)
    return pl.pallas_call(
        _moba_kernel,
        grid=(b, n_pairs),
        in_specs=[pl.BlockSpec(blk, lambda i, j: (i, 0, j)),
                  pl.BlockSpec(blk, lambda i, j: (i, 0, n_pairs + j)),
                  pl.BlockSpec(blk, lambda i, j: (i, 0, 2 * n_pairs + j))],
        out_specs=pl.BlockSpec(blk, lambda i, j: (i, 0, j)),
        out_shape=jax.ShapeDtypeStruct((b, s, MOBA_W), jnp.bfloat16),
        scratch_shapes=[pltpu.VMEM((2, s, LANES), jnp.bfloat16),
                        pltpu.VMEM((2, s // MOBA_KTILE, HEAD_DIM + MOBA_ONES, MOBA_KTILE),
                                   jnp.bfloat16),
                        pltpu.VMEM((LANES, LANES), jnp.bfloat16),
                        pltpu.VMEM((2, MOBA_KTILE, MOBA_KTILE), jnp.float32)],
        compiler_params=_cparams(("arbitrary", "arbitrary")),
        name="moba_attn",
    )(p3, p3, p3)


def _dil_kernel(q_ref, k_ref, v_ref, o_ref, lse_ref):
    tq = q_ref.shape[0]
    sub = DIL_SPAN
    kw = 2 * sub
    base_tile = pl.program_id(2) * tq
    lane = lax.broadcasted_iota(jnp.int32, (sub, LANES), 1)
    first = lane < HEAD_DIM
    diff = (lax.broadcasted_iota(jnp.int32, (sub, kw), 0)
            - lax.broadcasted_iota(jnp.int32, (sub, kw), 1))
    band = lambda dist: jnp.where((dist >= 0) & (dist <= DIL_SPAN), 0.0, NEG)
    bias_in = band(diff + sub)
    bias_first = jnp.where(base_tile == 0, band(diff), bias_in)
    nt = (((1,), (1,)), ((), ()))

    for sb in range(tq // sub):
        q0 = sb * sub
        kstart = pl.multiple_of(jnp.maximum(base_tile + q0 - sub, 0), sub)
        bias = bias_first if sb == 0 else bias_in
        for pr in range(DIL_HEADS_PER_GROUP // 2):
            cs = slice(pr * LANES, (pr + 1) * LANES)
            q2 = q_ref[pl.ds(q0, sub), cs].astype(jnp.float32)
            k2 = k_ref[pl.ds(kstart, kw), cs]
            v2 = v_ref[pl.ds(kstart, kw), cs]
            outs, lses = [], []
            for h in range(2):
                qh = jnp.where(first if h == 0 else jnp.logical_not(first), q2, 0.0)
                qh = qh.astype(jnp.bfloat16)
                sc = lax.dot_general(qh, k2, nt, preferred_element_type=jnp.float32) + bias
                m = jnp.max(sc, axis=1, keepdims=True)
                p = jnp.exp2(sc - m)
                l = jnp.sum(p, axis=1, keepdims=True)
                o = jnp.dot(p.astype(jnp.bfloat16), v2, preferred_element_type=jnp.float32)
                outs.append(o / l)
                lses.append(jnp.broadcast_to(m * LN2 + jnp.log(l), (sub, LANES)))
            o_ref[pl.ds(q0, sub), cs] = jnp.where(first, outs[0], outs[1]).astype(o_ref.dtype)
            lse_ref[pl.ds(q0, sub), cs] = jnp.where(first, lses[0], lses[1])


def _dil_group(pg, b, dil):
    w = DIL_OUT_W
    l = pg.shape[0] // b
    pv = pg.reshape(b, l, dil * 3 * w)
    tq = min(DIL_TQ, l)
    o, lse = pl.pallas_call(
        _dil_kernel,
        grid=(b, dil, l // tq),
        in_specs=[pl.BlockSpec((None, tq, w), lambda i, r, t: (i, t, 3 * r)),
                  pl.BlockSpec((None, l, w), lambda i, r, t: (i, 0, 3 * r + 1)),
                  pl.BlockSpec((None, l, w), lambda i, r, t: (i, 0, 3 * r + 2))],
        out_specs=[pl.BlockSpec((None, tq, w), lambda i, r, t: (i, t, r)),
                   pl.BlockSpec((None, tq, w), lambda i, r, t: (i, t, r))],
        out_shape=[jax.ShapeDtypeStruct((b, l, dil * w), jnp.bfloat16),
                   jax.ShapeDtypeStruct((b, l, dil * w), jnp.float32)],
        compiler_params=_cparams(("arbitrary", "arbitrary", "arbitrary")),
        name=f"dil_attn_d{dil}",
    )(pv, pv, pv)
    return o, lse


def _layer_norm(y, g, b):
    mu = jnp.mean(y, axis=-1, keepdims=True)
    yc = y - mu
    var = jnp.mean(yc * yc, axis=-1, keepdims=True)
    return yc * lax.rsqrt(var + LN_EPS) * g + b


def _twice_sigmoid_of_double(h):
    return 1.0 + jnp.tanh(h)


def _natural_order(src_ref, dst_ref, a, dil):
    rows = src_ref.shape[0]
    slabs = DIL_OUT_W // LANES
    for r in range(dil):
        for sl in range(slabs):
            c = r * DIL_OUT_W + sl * LANES
            dst_ref[a, sl, pl.ds(r, rows, stride=dil), :] = (
                src_ref[:, c:c + LANES].astype(jnp.float32))
    return jnp.concatenate([dst_ref[a, sl] for sl in range(slabs)], axis=1)


def _merge_kernel(alpha, x_ref, ym_ref, bg_ref, cg_ref, hc_ref, cgp_ref, hcp_ref,
                  g0_ref, g1_ref, g2_ref, o1_ref, o2_ref, o3_ref, l1_ref, l2_ref, l3_ref,
                  wm_ref, wc_ref, wd_ref, wo_ref, wsc_ref, lng_ref, lnb_ref, out_ref, nat_ref):
    f32 = jnp.float32
    tm = x_ref.shape[0]
    ch = cg_ref[...].astype(f32) * hc_ref[...].astype(f32)
    chp = cgp_ref[...].astype(f32) * hcp_ref[...].astype(f32)
    chp = jnp.where(pl.program_id(1) == 0, 0.0, chp)
    hp = chp.shape[0]
    prev1 = chp[hp - 1:hp, :]
    prev2 = chp[hp - 2:hp - 1, :]
    row = lax.broadcasted_iota(jnp.int32, (tm, CONV_WIDTH), 0)
    ch1 = jnp.where(row == 0, prev1, pltpu.roll(ch, 1, axis=0))
    ch2 = pltpu.roll(ch, 2, axis=0)
    ch2 = jnp.where(row == 0, prev2, jnp.where(row == 1, prev1, ch2))
    wsc = wsc_ref[...]
    conv = ch * wsc[0:1, :] + ch1 * wsc[1:2, :] + ch2 * wsc[2:3, :]
    y_conv = (bg_ref[...].astype(f32) * conv).astype(jnp.bfloat16)

    d2, d3 = DIL_GROUPS[1][1], DIL_GROUPS[2][1]
    o1, l1 = o1_ref[...].astype(f32), l1_ref[...]
    o2, l2 = _natural_order(o2_ref, nat_ref, 0, d2), _natural_order(l2_ref, nat_ref, 1, d2)
    o3, l3 = _natural_order(o3_ref, nat_ref, 2, d3), _natural_order(l3_ref, nat_ref, 3, d3)
    lm = jnp.maximum(jnp.maximum(l1, l2), l3)
    e1, e2, e3 = jnp.exp(l1 - lm), jnp.exp(l2 - lm), jnp.exp(l3 - lm)
    y_dil = ((e1 * o1 + e2 * o2 + e3 * o3) / (e1 + e2 + e3)).astype(jnp.bfloat16)

    pm = jnp.dot(ym_ref[...], wm_ref[...], preferred_element_type=f32)
    pc = jnp.dot(y_conv, wc_ref[...], preferred_element_type=f32)
    pd = jnp.dot(y_dil, wd_ref[...], preferred_element_type=f32)
    merged = (_twice_sigmoid_of_double(g0_ref[...].astype(f32)) * pm
              + _twice_sigmoid_of_double(g1_ref[...].astype(f32)) * pc
              + _twice_sigmoid_of_double(g2_ref[...].astype(f32)) * pd)
    mix = jnp.dot(merged.astype(jnp.bfloat16), wo_ref[...], preferred_element_type=f32)
    y = alpha * x_ref[...] + mix
    out_ref[...] = _layer_norm(y, lng_ref[...], lnb_ref[...])


def _merge(alpha, x3, ym, p3, dil_outs, wm, wc, wd, wo, wsc, lng, lnb):
    b, s, d = x3.shape
    tm = MERGE_TM
    hp = 16
    cw = CONV_WIDTH
    cblk = OFF_CONV // cw
    gblk = OFF_GATE // d
    row = lambda w: pl.BlockSpec((None, tm, w), lambda i, t: (i, t, 0))
    pcol = lambda w, c: pl.BlockSpec((None, tm, w), lambda i, t, c=c: (i, t, c))
    prev = lambda c: pl.BlockSpec(
        (None, hp, cw), lambda i, t, c=c: (i, jnp.maximum(t * (tm // hp) - 1, 0), c))
    full = lambda a: pl.BlockSpec(a.shape, lambda i, t: (0,) * a.ndim)
    (o1, l1), (o2, l2), (o3, l3) = dil_outs
    dils = [dil for _, dil in DIL_GROUPS]
    grp = lambda dil: pl.BlockSpec((None, tm // dil, dil * DIL_OUT_W), lambda i, t: (i, t, 0))
    in_specs = [row(d), row(MOBA_W),
                pcol(cw, cblk), pcol(cw, cblk + 1), pcol(cw, cblk + 2),
                prev(cblk + 1), prev(cblk + 2),
                pcol(d, gblk), pcol(d, gblk + 1), pcol(d, gblk + 2),
                grp(dils[0]), grp(dils[1]), grp(dils[2]),
                grp(dils[0]), grp(dils[1]), grp(dils[2]),
                full(wm), full(wc), full(wd), full(wo), full(wsc), full(lng), full(lnb)]
    return pl.pallas_call(
        functools.partial(_merge_kernel, alpha),
        grid=(b, s // tm),
        in_specs=in_specs,
        out_specs=row(d),
        out_shape=jax.ShapeDtypeStruct((b, s, d), jnp.float32),
        scratch_shapes=[pltpu.VMEM((4, DIL_OUT_W // LANES, tm, LANES), jnp.float32)],
        compiler_params=_cparams(("arbitrary", "arbitrary")),
        name="merge",
    )(x3, ym, p3, p3, p3, p3, p3, p3, p3, p3, o1, o2, o3, l1, l2, l3,
      wm, wc, wd, wo, wsc, lng, lnb)


def _ffn_kernel(alpha, x_ref, xp_ref, wup_ref, wdn_ref, wcv_ref, bcv_ref, lng_ref, lnb_ref,
                out_ref, xs_ref, act_ref):
    f32 = jnp.float32
    tm = x_ref.shape[0]
    halo = xp_ref.shape[0]
    xp = jnp.where(pl.program_id(1) == 0, 0.0, xp_ref[...])
    xs_ref[0:halo, :] = xp.astype(jnp.bfloat16)
    xs_ref[halo:halo + tm, :] = x_ref[...].astype(jnp.bfloat16)

    def conv(u, w):
        y = u * w[0:1, :] + pltpu.roll(u, 1, axis=0) * w[1:2, :] + pltpu.roll(u, 2, axis=0) * w[2:3, :]
        return y[halo:, :]

    for c in range(FFN_NCHUNK):
        gc = slice(c * FFN_CHUNK, (c + 1) * FFN_CHUNK)
        vc = slice(D_FF + c * FFN_CHUNK, D_FF + (c + 1) * FFN_CHUNK)
        xs = xs_ref[...]
        ug = jnp.dot(xs, wup_ref[:, gc], preferred_element_type=f32)
        uv = jnp.dot(xs, wup_ref[:, vc], preferred_element_type=f32)
        hg = conv(ug, wcv_ref[:, gc]) + bcv_ref[:, gc]
        yv = conv(uv, wcv_ref[:, vc]) + bcv_ref[:, vc]
        act_ref[:, gc] = (hg * _twice_sigmoid_of_double(hg) * yv).astype(jnp.bfloat16)

    ffn = jnp.dot(act_ref[...], wdn_ref[...], preferred_element_type=f32)
    y = alpha * x_ref[...] + ffn
    out_ref[...] = _layer_norm(y, lng_ref[...], lnb_ref[...])


def _ffn(alpha, x3, wup, wdn, wcv, bcv, lng, lnb):
    b, s, d = x3.shape
    tm = FFN_TM
    halo = FFN_HALO
    full = lambda a: pl.BlockSpec(a.shape, lambda i, t: (0,) * a.ndim)
    return pl.pallas_call(
        functools.partial(_ffn_kernel, alpha),
        grid=(b, s // tm),
        in_specs=[pl.BlockSpec((None, tm, d), lambda i, t: (i, t, 0)),
                  pl.BlockSpec((None, halo, d),
                               lambda i, t: (i, jnp.maximum(t * (tm // halo) - 1, 0), 0)),
                  full(wup), full(wdn), full(wcv), full(bcv), full(lng), full(lnb)],
        out_specs=pl.BlockSpec((None, tm, d), lambda i, t: (i, t, 0)),
        out_shape=jax.ShapeDtypeStruct((b, s, d), jnp.float32),
        scratch_shapes=[pltpu.VMEM((tm + halo, d), jnp.bfloat16),
                        pltpu.VMEM((tm, D_FF), jnp.bfloat16)],
        compiler_params=_cparams(("arbitrary", "arbitrary")),
        name="conv_ffn",
    )(x3, x3, wup, wdn, wcv, bcv, lng, lnb)


def kernel(x, w_in, w_short_conv, w_moba_proj, w_dil_proj, w_conv_proj, w_mix_out,
           ln1_g, ln1_b, w_up, w_ffn_conv, b_ffn_conv, w_down, ln2_g, ln2_b):
    b, s, d = x.shape
    depth = w_in.shape[0]
    alpha = float((2 * depth) ** 0.25)
    bf = jnp.bfloat16
    for l in range(depth):
        x2 = x.reshape(b * s, d)
        p = _in_proj(x2, w_in, l).reshape(b, s, MAIN_COLS)
        pgs = _dil_proj(x2, w_in, l)
        ym = _moba(p)
        dil_outs = [_dil_group(pg, b, dil) for pg, (_, dil) in zip(pgs, DIL_GROUPS)]
        half = lambda w: (0.5 * w).astype(bf)
        x = _merge(alpha, x, ym, p, dil_outs,
                   half(w_moba_proj[l]), half(w_conv_proj[l]), half(w_dil_proj[l]),
                   w_mix_out[l].astype(bf),
                   w_short_conv[l], ln1_g[l].reshape(1, d), ln1_b[l].reshape(1, d))
        gate_half = jnp.where(jnp.arange(2 * D_FF) < D_FF, 0.5, 1.0)
        x = _ffn(alpha, x, w_up[l].astype(bf), w_down[l].astype(bf),
                 w_ffn_conv[l] * gate_half, (b_ffn_conv[l] * gate_half).reshape(1, 2 * D_FF),
                 ln2_g[l].reshape(1, d), ln2_b[l].reshape(1, d))
    return x
```

```python
import functools

import jax
import jax.numpy as jnp
import numpy as np
from jax import lax
from jax.experimental import pallas as pl
from jax.experimental.pallas import tpu as pltpu

D_MODEL = 1024
HEAD_DIM = 64
MOBA_HEADS = 8
MOBA_BLOCK = 256
MOBA_TOPK = 3
MOBA_W = MOBA_HEADS * HEAD_DIM
DIL_GROUPS = ((128, 1), (512, 4), (2048, 16))
DIL_HEADS_PER_GROUP = 4
DIL_W = 3 * DIL_HEADS_PER_GROUP * HEAD_DIM
DIL_OUT_W = DIL_HEADS_PER_GROUP * HEAD_DIM
CONV_WIDTH = 512
N_BRANCH = 3
D_FF = 2816
LN_EPS = 1e-5

OFF_CONV = 3 * MOBA_W
OFF_GATE = OFF_CONV + 3 * CONV_WIDTH
MAIN_COLS = OFF_GATE + N_BRANCH * D_MODEL

LOG2E = float(np.log2(np.e))
LN2 = float(np.log(2.0))
Q_SCALE = HEAD_DIM ** -0.5 * LOG2E

LANES = 128
NEG = -1e30
VMEM_LIMIT = 56 * 1024 * 1024

PROJ_TM = 2048
PROJ_TN = 768
DILP_TM = 1024
MOBA_KTILE = 2 * MOBA_BLOCK
MOBA_UNROLL = 2
MOBA_ONES = 16
DIL_TQ = 512
DIL_SPAN = 128
MERGE_TM = 512
FFN_TM = 512
FFN_HALO = 16
FFN_CHUNK = 256
FFN_NCHUNK = D_FF // FFN_CHUNK


def _cparams(sem):
    return pltpu.CompilerParams(dimension_semantics=sem, vmem_limit_bytes=VMEM_LIMIT)


def _proj_kernel(x_ref, w_ref, cs_ref, o_ref, xb_ref, wb_ref):
    j = pl.program_id(1)

    @pl.when(j == 0)
    def _():
        xb_ref[...] = x_ref[...].astype(jnp.bfloat16)

    @pl.when(pl.program_id(0) == 0)
    def _():
        wb_ref[j] = w_ref[...].astype(jnp.bfloat16)

    acc = jnp.dot(xb_ref[...], wb_ref[j], preferred_element_type=jnp.float32)
    o_ref[...] = (acc * cs_ref[...]).astype(o_ref.dtype)


def _main_scale_row():
    cs = np.ones((1, MAIN_COLS), np.float32)
    cs[:, :MOBA_W] = Q_SCALE
    cs[:, OFF_GATE:] = 0.5
    return jnp.asarray(cs)


def _in_proj(x2, w_in, layer):
    n, d = x2.shape
    tn = PROJ_TN
    skip_at = 3 * MOBA_W // tn
    skip = 3 * DIL_W // tn
    n_ct = MAIN_COLS // tn
    src = lambda j: jnp.where(j < skip_at, j, j + skip)
    w_tile = lambda i, j: src(jnp.where(i == 0, j, n_ct - 1))
    return pl.pallas_call(
        _proj_kernel,
        grid=(n // PROJ_TM, n_ct),
        in_specs=[pl.BlockSpec((PROJ_TM, d), lambda i, j: (i, 0)),
                  pl.BlockSpec((None, d, tn), lambda i, j: (layer, 0, w_tile(i, j))),
                  pl.BlockSpec((1, tn), lambda i, j: (0, j))],
        out_specs=pl.BlockSpec((PROJ_TM, tn), lambda i, j: (i, j)),
        out_shape=jax.ShapeDtypeStruct((n, MAIN_COLS), jnp.bfloat16),
        scratch_shapes=[pltpu.VMEM((PROJ_TM, d), jnp.bfloat16),
                        pltpu.VMEM((n_ct, d, tn), jnp.bfloat16)],
        compiler_params=_cparams(("arbitrary", "arbitrary")),
        name="in_proj",
    )(x2, w_in, _main_scale_row())


def _dil_proj_kernel(x_ref, wq_ref, wk_ref, wv_ref, o0_ref, o1_ref, o2_ref, wb_ref, res_ref):
    tm = x_ref.shape[0]

    @pl.when(pl.program_id(0) == 0)
    def _():
        for j, w_ref in enumerate((wq_ref, wk_ref, wv_ref)):
            wb_ref[j] = w_ref[...].astype(jnp.bfloat16)

    xb = x_ref[...].astype(jnp.bfloat16)
    per_part = DIL_W // LANES
    for j in range(3):
        res = jnp.dot(xb, wb_ref[j], preferred_element_type=jnp.float32)
        if j == 0:
            res = res * Q_SCALE
        for cb in range(per_part):
            res_ref[j * per_part + cb] = res[:, cb * LANES:(cb + 1) * LANES]
    slabs = DIL_OUT_W // LANES
    for g, o_ref in enumerate((o0_ref, o1_ref, o2_ref)):
        d = DIL_GROUPS[g][1]
        rows = tm // d
        for r in range(d):
            for j in range(3):
                for sl in range(slabs):
                    src_cb = (j * DIL_W + g * DIL_OUT_W) // LANES + sl
                    dst_c = (r * 3 + j) * DIL_OUT_W + sl * LANES
                    if d == 1:
                        src = res_ref[src_cb]
                    else:
                        src = res_ref[src_cb, pl.ds(r, rows, stride=d), :]
                    o_ref[:, dst_c:dst_c + LANES] = src.astype(o_ref.dtype)


def _dil_proj(x2, w_in, layer):
    n, d = x2.shape
    tm = DILP_TM
    dils = [dil for _, dil in DIL_GROUPS]
    first = 3 * MOBA_W // DIL_W
    wspec = lambda j: pl.BlockSpec((None, d, DIL_W), lambda i, j=j: (layer, 0, first + j))
    return pl.pallas_call(
        _dil_proj_kernel,
        grid=(n // tm,),
        in_specs=[pl.BlockSpec((tm, d), lambda i: (i, 0)), wspec(0), wspec(1), wspec(2)],
        out_specs=[pl.BlockSpec((tm // dil, dil * 3 * DIL_OUT_W), lambda i: (i, 0))
                   for dil in dils],
        out_shape=[jax.ShapeDtypeStruct((n // dil, dil * 3 * DIL_OUT_W), jnp.bfloat16)
                   for dil in dils],
        scratch_shapes=[pltpu.VMEM((3, d, DIL_W), jnp.bfloat16),
                        pltpu.VMEM((3 * DIL_W // LANES, tm, LANES), jnp.float32)],
        compiler_params=_cparams(("arbitrary",)),
        name="dil_proj",
    )(x2, w_in, w_in, w_in)


def _moba_kernel(q_ref, k_ref, v_ref, o_ref, kaug_ref, vaug_ref, km_ref, sc_ref):
    s = q_ref.shape[0]
    nb = s // MOBA_BLOCK
    tq = kt = MOBA_KTILE
    f32 = jnp.float32
    bf = jnp.bfloat16
    sub_bf = 16
    lane_r = lax.broadcasted_iota(jnp.int32, (sub_bf, LANES), 1)
    row01 = lambda cond: cond.astype(f32).astype(bf)
    blk01 = lambda cond: jnp.broadcast_to(
        row01(cond)[None], (MOBA_BLOCK // sub_bf, sub_bf, LANES)).reshape(MOBA_BLOCK, LANES)
    head_lanes = (blk01(lane_r < HEAD_DIM), blk01(lane_r >= HEAD_DIM))
    for n in range(nb):
        rows = slice(n * MOBA_BLOCK, (n + 1) * MOBA_BLOCK)
        kblk = k_ref[rows, :]
        kaug_ref[0, rows, :] = kblk * head_lanes[0] + blk01(lane_r == HEAD_DIM + n)
        kaug_ref[1, rows, :] = kblk * head_lanes[1] + blk01(lane_r == n)
    vr = HEAD_DIM + MOBA_ONES
    lane_k = lax.broadcasted_iota(jnp.int32, (kt, LANES), 1)
    for t in range(s // kt):
        vt = v_ref[t * kt:(t + 1) * kt, :].astype(f32)
        vaug_ref[0, t] = jnp.where(lane_k < HEAD_DIM, vt, 1.0).T[:vr].astype(bf)
        vaug_ref[1, t] = jnp.where(lane_k >= HEAD_DIM, vt, 1.0).T[LANES - vr:].astype(bf)

    offs = (HEAD_DIM, 0)
    in_blk = (lax.broadcasted_iota(jnp.int32, (nb, s), 1) // MOBA_BLOCK
              == lax.broadcasted_iota(jnp.int32, (nb, s), 0))
    ksum = jnp.dot(row01(in_blk), k_ref[...], preferred_element_type=f32)
    kmean = ksum * (1.0 / MOBA_BLOCK)
    lane_b = lax.broadcasted_iota(jnp.int32, (nb, LANES), 1)
    km0 = jnp.where(lane_b < HEAD_DIM, kmean, 0.0)
    km1 = jnp.where(lane_b >= HEAD_DIM, kmean, 0.0)
    zgap = jnp.zeros((HEAD_DIM - nb, LANES), f32)
    km_ref[...] = jnp.concatenate([km1, zgap, km0, zgap], axis=0).astype(bf)

    lane_t = lax.broadcasted_iota(jnp.int32, (tq, LANES), 1)
    causal = (lax.broadcasted_iota(jnp.int32, (kt, tq), 0)
              <= lax.broadcasted_iota(jnp.int32, (kt, tq), 1))
    blk_i = lax.broadcasted_iota(jnp.int32, (nb, tq), 0)
    blk_f = blk_i.astype(f32)
    q_sub = lax.broadcasted_iota(jnp.int32, (nb, tq), 1) // MOBA_BLOCK
    zgap_t = jnp.zeros((HEAD_DIM - nb, tq), f32)
    nt = (((1,), (1,)), ((), ()))
    heads = (0, 1)
    q_lanes = (lane_t < HEAD_DIM, lane_t >= HEAD_DIM)

    def q_tile(t_last, carry):
        r0 = pl.multiple_of(t_last * tq, tq)
        q2 = q_ref[pl.ds(r0, tq), :]
        q2f = q2.astype(f32)
        gate_t = lax.dot_general(km_ref[...], q2, nt, preferred_element_type=f32)
        j = t_last * (tq // MOBA_BLOCK) + q_sub
        valid = blk_i < j
        pens = []
        for h in heads:
            g = jnp.where(valid, gate_t[offs[h]:offs[h] + nb, :], NEG)
            sel = jnp.zeros((nb, tq), jnp.bool_)
            for _ in range(MOBA_TOPK):
                m = jnp.max(g, axis=0, keepdims=True)
                first = jnp.min(jnp.where(g == m, blk_f, 1e9), axis=0, keepdims=True)
                pick = blk_f == first
                sel = sel | pick
                g = jnp.where(pick, NEG, g)
            keep = (sel & valid) | (blk_i == j)
            pens.append(jnp.where(keep, 0.0, NEG))
        pen = jnp.concatenate([pens[1], zgap_t, pens[0], zgap_t], axis=0).T
        qas = []
        for h in heads:
            qa = jnp.where(q_lanes[h], q2f, pen).astype(bf)
            sc = lax.dot_general(kaug_ref[h, pl.ds(r0, kt), :], qa, nt,
                                 preferred_element_type=f32)
            sc_ref[h] = jnp.where(causal, sc, NEG)
            qas.append(qa)

        def tile_of(visit):
            return jnp.where(visit <= 0, t_last, visit - 1)

        def kv_step(i, mc):
            t_pv = tile_of(i)
            k0 = pl.multiple_of(jnp.clip(i, 0, jnp.maximum(t_last - 1, 0)) * kt, kt)
            out = []
            for h in heads:
                m_old, acc = mc[2 * h], mc[2 * h + 1]
                sc = sc_ref[h]
                sc_next = lax.dot_general(kaug_ref[h, pl.ds(k0, kt), :], qas[h], nt,
                                          preferred_element_type=f32)
                m_new = jnp.maximum(m_old, jnp.max(sc, axis=0, keepdims=True))
                alpha = jnp.exp2(m_old - m_new)
                p = jnp.exp2(sc - m_new).astype(bf)
                pv = jnp.dot(vaug_ref[h, t_pv], p, preferred_element_type=f32)
                sc_ref[h] = sc_next
                out += [m_new, acc * alpha + pv]
            return tuple(out)

        m_init = jnp.full((1, tq), NEG, f32)
        acc_init = jnp.zeros((vr, tq), f32)
        def multi_step(ii, mc):
            for u in range(MOBA_UNROLL):
                mc = kv_step(MOBA_UNROLL * ii + u, mc)
            return mc

        n_steps = t_last + 1
        n_main = n_steps // MOBA_UNROLL
        mc = lax.fori_loop(0, n_main, multi_step, (m_init, acc_init, m_init, acc_init))
        fin = lax.fori_loop(n_main * MOBA_UNROLL, n_steps, kv_step, mc)
        o0 = fin[1][:HEAD_DIM] / fin[1][HEAD_DIM:HEAD_DIM + 1]
        o1 = fin[3][MOBA_ONES:] / fin[3][0:1]
        o_ref[pl.ds(r0, tq), :] = jnp.concatenate([o0, o1], axis=0).T.astype(o_ref.dtype)
        return carry

    lax.fori_loop(0, s // tq, q_tile, 0)


def _moba(p3):
    b, s, _ = p3.shape
    n_pairs = MOBA_HEADS // 2
    blk = (None, s, LANES)
    return pl.pallas_call(
        _moba_kernel,
        grid=(b, n_pairs),
        in_specs=[pl.BlockSpec(blk, lambda i, j: (i, 0, j)),
                  pl.BlockSpec(blk, lambda i, j: (i, 0, n_pairs + j)),
                  pl.BlockSpec(blk, lambda i, j: (i, 0, 2 * n_pairs + j))],
        out_specs=pl.BlockSpec(blk, lambda i, j: (i, 0, j)),
        out_shape=jax.ShapeDtypeStruct((b, s, MOBA_W), jnp.bfloat16),
        scratch_shapes=[pltpu.VMEM((2, s, LANES), jnp.bfloat16),
                        pltpu.VMEM((2, s // MOBA_KTILE, HEAD_DIM + MOBA_ONES, MOBA_KTILE),
                                   jnp.bfloat16),
                        pltpu.VMEM((LANES, LANES), jnp.bfloat16),
                        pltpu.VMEM((2, MOBA_KTILE, MOBA_KTILE), jnp.float32)],
        compiler_params=_cparams(("arbitrary", "arbitrary")),
        name="moba_attn",
    )(p3, p3, p3)


def _dil_kernel(q_ref, k_ref, v_ref, o_ref, lse_ref):
    tq = q_ref.shape[0]
    sub = DIL_SPAN
    kw = 2 * sub
    base_tile = pl.program_id(2) * tq
    lane = lax.broadcasted_iota(jnp.int32, (sub, LANES), 1)
    first = lane < HEAD_DIM
    diff = (lax.broadcasted_iota(jnp.int32, (sub, kw), 0)
            - lax.broadcasted_iota(jnp.int32, (sub, kw), 1))
    band = lambda dist: jnp.where((dist >= 0) & (dist <= DIL_SPAN), 0.0, NEG)
    bias_in = band(diff + sub)
    bias_first = jnp.where(base_tile == 0, band(diff), bias_in)
    nt = (((1,), (1,)), ((), ()))

    for sb in range(tq // sub):
        q0 = sb * sub
        kstart = pl.multiple_of(jnp.maximum(base_tile + q0 - sub, 0), sub)
        bias = bias_first if sb == 0 else bias_in
        for pr in range(DIL_HEADS_PER_GROUP // 2):
            cs = slice(pr * LANES, (pr + 1) * LANES)
            q2 = q_ref[pl.ds(q0, sub), cs].astype(jnp.float32)
            k2 = k_ref[pl.ds(kstart, kw), cs]
            v2 = v_ref[pl.ds(kstart, kw), cs]
            outs, lses = [], []
            for h in range(2):
                qh = jnp.where(first if h == 0 else jnp.logical_not(first), q2, 0.0)
                qh = qh.astype(jnp.bfloat16)
                sc = lax.dot_general(qh, k2, nt, preferred_element_type=jnp.float32) + bias
                m = jnp.max(sc, axis=1, keepdims=True)
                p = jnp.exp2(sc - m)
                l = jnp.sum(p, axis=1, keepdims=True)
                o = jnp.dot(p.astype(jnp.bfloat16), v2, preferred_element_type=jnp.float32)
                outs.append(o / l)
                lses.append(jnp.broadcast_to(m * LN2 + jnp.log(l), (sub, LANES)))
            o_ref[pl.ds(q0, sub), cs] = jnp.where(first, outs[0], outs[1]).astype(o_ref.dtype)
            lse_ref[pl.ds(q0, sub), cs] = jnp.where(first, lses[0], lses[1])


def _dil_group(pg, b, dil):
    w = DIL_OUT_W
    l = pg.shape[0] // b
    pv = pg.reshape(b, l, dil * 3 * w)
    tq = min(DIL_TQ, l)
    o, lse = pl.pallas_call(
        _dil_kernel,
        grid=(b, dil, l // tq),
        in_specs=[pl.BlockSpec((None, tq, w), lambda i, r, t: (i, t, 3 * r)),
                  pl.BlockSpec((None, l, w), lambda i, r, t: (i, 0, 3 * r + 1)),
                  pl.BlockSpec((None, l, w), lambda i, r, t: (i, 0, 3 * r + 2))],
        out_specs=[pl.BlockSpec((None, tq, w), lambda i, r, t: (i, t, r)),
                   pl.BlockSpec((None, tq, w), lambda i, r, t: (i, t, r))],
        out_shape=[jax.ShapeDtypeStruct((b, l, dil * w), jnp.bfloat16),
                   jax.ShapeDtypeStruct((b, l, dil * w), jnp.float32)],
        compiler_params=_cparams(("arbitrary", "arbitrary", "arbitrary")),
        name=f"dil_attn_d{dil}",
    )(pv, pv, pv)
    return o, lse


def _layer_norm(y, g, b):
    mu = jnp.mean(y, axis=-1, keepdims=True)
    yc = y - mu
    var = jnp.mean(yc * yc, axis=-1, keepdims=True)
    return yc * lax.rsqrt(var + LN_EPS) * g + b


def _twice_sigmoid_of_double(h):
    return 1.0 + jnp.tanh(h)


def _natural_order(src_ref, dst_ref, a, dil):
    rows = src_ref.shape[0]
    slabs = DIL_OUT_W // LANES
    for r in range(dil):
        for sl in range(slabs):
            c = r * DIL_OUT_W + sl * LANES
            dst_ref[a, sl, pl.ds(r, rows, stride=dil), :] = (
                src_ref[:, c:c + LANES].astype(jnp.float32))
    return jnp.concatenate([dst_ref[a, sl] for sl in range(slabs)], axis=1)


def _merge_kernel(alpha, x_ref, ym_ref, bg_ref, cg_ref, hc_ref, cgp_ref, hcp_ref,
                  g0_ref, g1_ref, g2_ref, o1_ref, o2_ref, o3_ref, l1_ref, l2_ref, l3_ref,
                  wm_ref, wc_ref, wd_ref, wo_ref, wsc_ref, lng_ref, lnb_ref, out_ref, nat_ref):
    f32 = jnp.float32
    tm = x_ref.shape[0]
    ch = cg_ref[...].astype(f32) * hc_ref[...].astype(f32)
    chp = cgp_ref[...].astype(f32) * hcp_ref[...].astype(f32)
    chp = jnp.where(pl.program_id(1) == 0, 0.0, chp)
    hp = chp.shape[0]
    prev1 = chp[hp - 1:hp, :]
    prev2 = chp[hp - 2:hp - 1, :]
    row = lax.broadcasted_iota(jnp.int32, (tm, CONV_WIDTH), 0)
    ch1 = jnp.where(row == 0, prev1, pltpu.roll(ch, 1, axis=0))
    ch2 = pltpu.roll(ch, 2, axis=0)
    ch2 = jnp.where(row == 0, prev2, jnp.where(row == 1, prev1, ch2))
    wsc = wsc_ref[...]
    conv = ch * wsc[0:1, :] + ch1 * wsc[1:2, :] + ch2 * wsc[2:3, :]
    y_conv = (bg_ref[...].astype(f32) * conv).astype(jnp.bfloat16)

    d2, d3 = DIL_GROUPS[1][1], DIL_GROUPS[2][1]
    o1, l1 = o1_ref[...].astype(f32), l1_ref[...]
    o2, l2 = _natural_order(o2_ref, nat_ref, 0, d2), _natural_order(l2_ref, nat_ref, 1, d2)
    o3, l3 = _natural_order(o3_ref, nat_ref, 2, d3), _natural_order(l3_ref, nat_ref, 3, d3)
    lm = jnp.maximum(jnp.maximum(l1, l2), l3)
    e1, e2, e3 = jnp.exp(l1 - lm), jnp.exp(l2 - lm), jnp.exp(l3 - lm)
    y_dil = ((e1 * o1 + e2 * o2 + e3 * o3) / (e1 + e2 + e3)).astype(jnp.bfloat16)

    pm = jnp.dot(ym_ref[...], wm_ref[...], preferred_element_type=f32)
    pc = jnp.dot(y_conv, wc_ref[...], preferred_element_type=f32)
    pd = jnp.dot(y_dil, wd_ref[...], preferred_element_type=f32)
    merged = (_twice_sigmoid_of_double(g0_ref[...].astype(f32)) * pm
              + _twice_sigmoid_of_double(g1_ref[...].astype(f32)) * pc
              + _twice_sigmoid_of_double(g2_ref[...].astype(f32)) * pd)
    mix = jnp.dot(merged.astype(jnp.bfloat16), wo_ref[...], preferred_element_type=f32)
    y = alpha * x_ref[...] + mix
    out_ref[...] = _layer_norm(y, lng_ref[...], lnb_ref[...])


def _merge(alpha, x3, ym, p3, dil_outs, wm, wc, wd, wo, wsc, lng, lnb):
    b, s, d = x3.shape
    tm = MERGE_TM
    hp = 16
    cw = CONV_WIDTH
    cblk = OFF_CONV // cw
    gblk = OFF_GATE // d
    row = lambda w: pl.BlockSpec((None, tm, w), lambda i, t: (i, t, 0))
    pcol = lambda w, c: pl.BlockSpec((None, tm, w), lambda i, t, c=c: (i, t, c))
    prev = lambda c: pl.BlockSpec(
        (None, hp, cw), lambda i, t, c=c: (i, jnp.maximum(t * (tm // hp) - 1, 0), c))
    full = lambda a: pl.BlockSpec(a.shape, lambda i, t: (0,) * a.ndim)
    (o1, l1), (o2, l2), (o3, l3) = dil_outs
    dils = [dil for _, dil in DIL_GROUPS]
    grp = lambda dil: pl.BlockSpec((None, tm // dil, dil * DIL_OUT_W), lambda i, t: (i, t, 0))
    in_specs = [row(d), row(MOBA_W),
                pcol(cw, cblk), pcol(cw, cblk + 1), pcol(cw, cblk + 2),
                prev(cblk + 1), prev(cblk + 2),
                pcol(d, gblk), pcol(d, gblk + 1), pcol(d, gblk + 2),
                grp(dils[0]), grp(dils[1]), grp(dils[2]),
                grp(dils[0]), grp(dils[1]), grp(dils[2]),
                full(wm), full(wc), full(wd), full(wo), full(wsc), full(lng), full(lnb)]
    return pl.pallas_call(
        functools.partial(_merge_kernel, alpha),
        grid=(b, s // tm),
        in_specs=in_specs,
        out_specs=row(d),
        out_shape=jax.ShapeDtypeStruct((b, s, d), jnp.float32),
        scratch_shapes=[pltpu.VMEM((4, DIL_OUT_W // LANES, tm, LANES), jnp.float32)],
        compiler_params=_cparams(("arbitrary", "arbitrary")),
        name="merge",
    )(x3, ym, p3, p3, p3, p3, p3, p3, p3, p3, o1, o2, o3, l1, l2, l3,
      wm, wc, wd, wo, wsc, lng, lnb)


def _ffn_kernel(alpha, x_ref, xp_ref, wup_ref, wdn_ref, wcv_ref, bcv_ref, lng_ref, lnb_ref,
                out_ref, xs_ref, act_ref):
    f32 = jnp.float32
    tm = x_ref.shape[0]
    halo = xp_ref.shape[0]
    xp = jnp.where(pl.program_id(1) == 0, 0.0, xp_ref[...])
    xs_ref[0:halo, :] = xp.astype(jnp.bfloat16)
    xs_ref[halo:halo + tm, :] = x_ref[...].astype(jnp.bfloat16)

    def conv(u, w):
        y = u * w[0:1, :] + pltpu.roll(u, 1, axis=0) * w[1:2, :] + pltpu.roll(u, 2, axis=0) * w[2:3, :]
        return y[halo:, :]

    for c in range(FFN_NCHUNK):
        gc = slice(c * FFN_CHUNK, (c + 1) * FFN_CHUNK)
        vc = slice(D_FF + c * FFN_CHUNK, D_FF + (c + 1) * FFN_CHUNK)
        xs = xs_ref[...]
        ug = jnp.dot(xs, wup_ref[:, gc], preferred_element_type=f32)
        uv = jnp.dot(xs, wup_ref[:, vc], preferred_element_type=f32)
        hg = conv(ug, wcv_ref[:, gc]) + bcv_ref[:, gc]
        yv = conv(uv, wcv_ref[:, vc]) + bcv_ref[:, vc]
        act_ref[:, gc] = (hg * _twice_sigmoid_of_double(hg) * yv).astype(jnp.bfloat16)

    ffn = jnp.dot(act_ref[...], wdn_ref[...], preferred_element_type=f32)
    y = alpha * x_ref[...] + ffn
    out_ref[...] = _layer_norm(y, lng_ref[...], lnb_ref[...])


def _ffn(alpha, x3, wup, wdn, wcv, bcv, lng, lnb):
    b, s, d = x3.shape
    tm = FFN_TM
    halo = FFN_HALO
    full = lambda a: pl.BlockSpec(a.shape, lambda i, t: (0,) * a.ndim)
    return pl.pallas_call(
        functools.partial(_ffn_kernel, alpha),
        grid=(b, s // tm),
        in_specs=[pl.BlockSpec((None, tm, d), lambda i, t: (i, t, 0)),
                  pl.BlockSpec((None, halo, d),
                               lambda i, t: (i, jnp.maximum(t * (tm // halo) - 1, 0), 0)),
                  full(wup), full(wdn), full(wcv), full(bcv), full(lng), full(lnb)],
        out_specs=pl.BlockSpec((None, tm, d), lambda i, t: (i, t, 0)),
        out_shape=jax.ShapeDtypeStruct((b, s, d), jnp.float32),
        scratch_shapes=[pltpu.VMEM((tm + halo, d), jnp.bfloat16),
                        pltpu.VMEM((tm, D_FF), jnp.bfloat16)],
        compiler_params=_cparams(("arbitrary", "arbitrary")),
        name="conv_ffn",
    )(x3, x3, wup, wdn, wcv, bcv, lng, lnb)


def kernel(x, w_in, w_short_conv, w_moba_proj, w_dil_proj, w_conv_proj, w_mix_out,
           ln1_g, ln1_b, w_up, w_ffn_conv, b_ffn_conv, w_down, ln2_g, ln2_b):
    b, s, d = x.shape
    depth = w_in.shape[0]
    alpha = float((2 * depth) ** 0.25)
    bf = jnp.bfloat16
    for l in range(depth):
        x2 = x.reshape(b * s, d)
        p = _in_proj(x2, w_in, l).reshape(b, s, MAIN_COLS)
        pgs = _dil_proj(x2, w_in, l)
        ym = _moba(p)
        dil_outs = [_dil_group(pg, b, dil) for pg, (_, dil) in zip(pgs, DIL_GROUPS)]
        half = lambda w: (0.5 * w).astype(bf)
        x = _merge(alpha, x, ym, p, dil_outs,
                   half(w_moba_proj[l]), half(w_conv_proj[l]), half(w_dil_proj[l]),
                   w_mix_out[l].astype(bf),
                   w_short_conv[l], ln1_g[l].reshape(1, d), ln1_b[l].reshape(1, d))
        gate_half = jnp.where(jnp.arange(2 * D_FF) < D_FF, 0.5, 1.0)
        x = _ffn(alpha, x, w_up[l].astype(bf), w_down[l].astype(bf),
                 w_ffn_conv[l] * gate_half, (b_ffn_conv[l] * gate_half).reshape(1, 2 * D_FF),
                 ln2_g[l].reshape(1, d), ln2_b[l].reshape(1, d))
    return x
```

```python
import functools

import jax
import jax.numpy as jnp
import numpy as np
from jax import lax
from jax.experimental import pallas as pl
from jax.experimental.pallas import tpu as pltpu

D_MODEL = 1024
HEAD_DIM = 64
MOBA_HEADS = 8
MOBA_BLOCK = 256
MOBA_TOPK = 3
MOBA_W = MOBA_HEADS * HEAD_DIM
DIL_GROUPS = ((128, 1), (512, 4), (2048, 16))
DIL_HEADS_PER_GROUP = 4
DIL_W = 3 * DIL_HEADS_PER_GROUP * HEAD_DIM
DIL_OUT_W = DIL_HEADS_PER_GROUP * HEAD_DIM
CONV_WIDTH = 512
N_BRANCH = 3
D_FF = 2816
LN_EPS = 1e-5

OFF_CONV = 3 * MOBA_W
OFF_GATE = OFF_CONV + 3 * CONV_WIDTH
MAIN_COLS = OFF_GATE + N_BRANCH * D_MODEL

LOG2E = float(np.log2(np.e))
LN2 = float(np.log(2.0))
Q_SCALE = HEAD_DIM ** -0.5 * LOG2E

LANES = 128
NEG = -1e30
VMEM_LIMIT = 56 * 1024 * 1024

PROJ_TM = 2048
PROJ_TN = 768
DILP_TM = 1024
MOBA_KTILE = 2 * MOBA_BLOCK
MOBA_UNROLL = 2
MOBA_ONES = 16
DIL_TQ = 1024
DIL_SPAN = 128
MERGE_TM = 512
FFN_TM = 512
FFN_HALO = 16
FFN_CHUNK = 256
FFN_NCHUNK = D_FF // FFN_CHUNK


def _cparams(sem):
    return pltpu.CompilerParams(dimension_semantics=sem, vmem_limit_bytes=VMEM_LIMIT)


def _proj_kernel(x_ref, w_ref, cs_ref, o_ref, xb_ref, wb_ref):
    j = pl.program_id(1)

    @pl.when(j == 0)
    def _():
        xb_ref[...] = x_ref[...].astype(jnp.bfloat16)

    @pl.when(pl.program_id(0) == 0)
    def _():
        wb_ref[j] = w_ref[...].astype(jnp.bfloat16)

    acc = jnp.dot(xb_ref[...], wb_ref[j], preferred_element_type=jnp.float32)
    o_ref[...] = (acc * cs_ref[...]).astype(o_ref.dtype)


def _main_scale_row():
    cs = np.ones((1, MAIN_COLS), np.float32)
    cs[:, :MOBA_W] = Q_SCALE
    cs[:, OFF_GATE:] = 0.5
    return jnp.asarray(cs)


def _in_proj(x2, w_in, layer):
    n, d = x2.shape
    tn = PROJ_TN
    skip_at = 3 * MOBA_W // tn
    skip = 3 * DIL_W // tn
    n_ct = MAIN_COLS // tn
    src = lambda j: jnp.where(j < skip_at, j, j + skip)
    w_tile = lambda i, j: src(jnp.where(i == 0, j, n_ct - 1))
    return pl.pallas_call(
        _proj_kernel,
        grid=(n // PROJ_TM, n_ct),
        in_specs=[pl.BlockSpec((PROJ_TM, d), lambda i, j: (i, 0)),
                  pl.BlockSpec((None, d, tn), lambda i, j: (layer, 0, w_tile(i, j))),
                  pl.BlockSpec((1, tn), lambda i, j: (0, j))],
        out_specs=pl.BlockSpec((PROJ_TM, tn), lambda i, j: (i, j)),
        out_shape=jax.ShapeDtypeStruct((n, MAIN_COLS), jnp.bfloat16),
        scratch_shapes=[pltpu.VMEM((PROJ_TM, d), jnp.bfloat16),
                        pltpu.VMEM((n_ct, d, tn), jnp.bfloat16)],
        compiler_params=_cparams(("arbitrary", "arbitrary")),
        name="in_proj",
    )(x2, w_in, _main_scale_row())


def _dil_proj_kernel(x_ref, wq_ref, wk_ref, wv_ref, o0_ref, o1_ref, o2_ref, wb_ref, res_ref):
    tm = x_ref.shape[0]

    @pl.when(pl.program_id(0) == 0)
    def _():
        for j, w_ref in enumerate((wq_ref, wk_ref, wv_ref)):
            wb_ref[j] = w_ref[...].astype(jnp.bfloat16)

    xb = x_ref[...].astype(jnp.bfloat16)
    per_part = DIL_W // LANES
    for j in range(3):
        res = jnp.dot(xb, wb_ref[j], preferred_element_type=jnp.float32)
        if j == 0:
            res = res * Q_SCALE
        for cb in range(per_part):
            res_ref[j * per_part + cb] = res[:, cb * LANES:(cb + 1) * LANES]
    slabs = DIL_OUT_W // LANES
    for g, o_ref in enumerate((o0_ref, o1_ref, o2_ref)):
        d = DIL_GROUPS[g][1]
        rows = tm // d
        for r in range(d):
            for j in range(3):
                for sl in range(slabs):
                    src_cb = (j * DIL_W + g * DIL_OUT_W) // LANES + sl
                    dst_c = (r * 3 + j) * DIL_OUT_W + sl * LANES
                    if d == 1:
                        src = res_ref[src_cb]
                    else:
                        src = res_ref[src_cb, pl.ds(r, rows, stride=d), :]
                    o_ref[:, dst_c:dst_c + LANES] = src.astype(o_ref.dtype)


def _dil_proj(x2, w_in, layer):
    n, d = x2.shape
    tm = DILP_TM
    dils = [dil for _, dil in DIL_GROUPS]
    first = 3 * MOBA_W // DIL_W
    wspec = lambda j: pl.BlockSpec((None, d, DIL_W), lambda i, j=j: (layer, 0, first + j))
    return pl.pallas_call(
        _dil_proj_kernel,
        grid=(n // tm,),
        in_specs=[pl.BlockSpec((tm, d), lambda i: (i, 0)), wspec(0), wspec(1), wspec(2)],
        out_specs=[pl.BlockSpec((tm // dil, dil * 3 * DIL_OUT_W), lambda i: (i, 0))
                   for dil in dils],
        out_shape=[jax.ShapeDtypeStruct((n // dil, dil * 3 * DIL_OUT_W), jnp.bfloat16)
                   for dil in dils],
        scratch_shapes=[pltpu.VMEM((3, d, DIL_W), jnp.bfloat16),
                        pltpu.VMEM((3 * DIL_W // LANES, tm, LANES), jnp.float32)],
        compiler_params=_cparams(("arbitrary",)),
        name="dil_proj",
    )(x2, w_in, w_in, w_in)


def _moba_kernel(q_ref, k_ref, v_ref, o_ref, kaug_ref, vaug_ref, km_ref, sc_ref, qa_ref):
    s = q_ref.shape[0]
    nb = s // MOBA_BLOCK
    tq = kt = MOBA_KTILE
    f32 = jnp.float32
    bf = jnp.bfloat16
    sub_bf = 16
    lane_r = lax.broadcasted_iota(jnp.int32, (sub_bf, LANES), 1)
    row01 = lambda cond: cond.astype(f32).astype(bf)
    blk01 = lambda cond: jnp.broadcast_to(
        row01(cond)[None], (MOBA_BLOCK // sub_bf, sub_bf, LANES)).reshape(MOBA_BLOCK, LANES)
    head_lanes = (blk01(lane_r < HEAD_DIM), blk01(lane_r >= HEAD_DIM))
    for n in range(nb):
        rows = slice(n * MOBA_BLOCK, (n + 1) * MOBA_BLOCK)
        kblk = k_ref[rows, :]
        kaug_ref[0, rows, :] = kblk * head_lanes[0] + blk01(lane_r == HEAD_DIM + n)
        kaug_ref[1, rows, :] = kblk * head_lanes[1] + blk01(lane_r == n)
    vr = HEAD_DIM + MOBA_ONES
    lane_k = lax.broadcasted_iota(jnp.int32, (kt, LANES), 1)
    for t in range(s // kt):
        vt = v_ref[t * kt:(t + 1) * kt, :].astype(f32)
        vaug_ref[0, t] = jnp.where(lane_k < HEAD_DIM, vt, 1.0).T[:vr].astype(bf)
        vaug_ref[1, t] = jnp.where(lane_k >= HEAD_DIM, vt, 1.0).T[LANES - vr:].astype(bf)

    offs = (HEAD_DIM, 0)
    in_blk = (lax.broadcasted_iota(jnp.int32, (nb, s), 1) // MOBA_BLOCK
              == lax.broadcasted_iota(jnp.int32, (nb, s), 0))
    ksum = jnp.dot(row01(in_blk), k_ref[...], preferred_element_type=f32)
    kmean = ksum * (1.0 / MOBA_BLOCK)
    lane_b = lax.broadcasted_iota(jnp.int32, (nb, LANES), 1)
    km0 = jnp.where(lane_b < HEAD_DIM, kmean, 0.0)
    km1 = jnp.where(lane_b >= HEAD_DIM, kmean, 0.0)
    zgap = jnp.zeros((HEAD_DIM - nb, LANES), f32)
    km_ref[...] = jnp.concatenate([km1, zgap, km0, zgap], axis=0).astype(bf)

    lane_t = lax.broadcasted_iota(jnp.int32, (tq, LANES), 1)
    causal = (lax.broadcasted_iota(jnp.int32, (kt, tq), 0)
              <= lax.broadcasted_iota(jnp.int32, (kt, tq), 1))
    blk_i = lax.broadcasted_iota(jnp.int32, (nb, tq), 0)
    blk_f = blk_i.astype(f32)
    q_sub = lax.broadcasted_iota(jnp.int32, (nb, tq), 1) // MOBA_BLOCK
    zgap_t = jnp.zeros((HEAD_DIM - nb, tq), f32)
    nt = (((1,), (1,)), ((), ()))
    heads = (0, 1)
    q_lanes = (lane_t < HEAD_DIM, lane_t >= HEAD_DIM)

    def augmented_queries(tile):
        q2 = q_ref[pl.ds(pl.multiple_of(tile * tq, tq), tq), :]
        q2f = q2.astype(f32)
        gate_t = lax.dot_general(km_ref[...], q2, nt, preferred_element_type=f32)
        j = tile * (tq // MOBA_BLOCK) + q_sub
        valid = blk_i < j
        pens = []
        for h in heads:
            g = jnp.where(valid, gate_t[offs[h]:offs[h] + nb, :], NEG)
            sel = jnp.zeros((nb, tq), jnp.bool_)
            for _ in range(MOBA_TOPK):
                m = jnp.max(g, axis=0, keepdims=True)
                first = jnp.min(jnp.where(g == m, blk_f, 1e9), axis=0, keepdims=True)
                pick = blk_f == first
                sel = sel | pick
                g = jnp.where(pick, NEG, g)
            keep = (sel & valid) | (blk_i == j)
            pens.append(jnp.where(keep, 0.0, NEG))
        pen = jnp.concatenate([pens[1], zgap_t, pens[0], zgap_t], axis=0).T
        return [jnp.where(q_lanes[h], q2f, pen).astype(bf) for h in heads]

    def prepare(tile):
        r0 = pl.multiple_of(tile * tq, tq)
        for h, qa in enumerate(augmented_queries(tile)):
            sc = lax.dot_general(kaug_ref[h, pl.ds(r0, kt), :], qa, nt,
                                 preferred_element_type=f32)
            sc_ref[h] = jnp.where(causal, sc, NEG)
            qa_ref[h] = qa

    n_tiles = s // tq
    prepare(0)

    def q_tile(t_last, carry):
        r0 = pl.multiple_of(t_last * tq, tq)
        qas = [qa_ref[h] for h in heads]

        def tile_of(visit):
            return jnp.where(visit <= 0, t_last, visit - 1)

        def kv_step(i, mc):
            t_pv = tile_of(i)
            k0 = pl.multiple_of(jnp.clip(i, 0, jnp.maximum(t_last - 1, 0)) * kt, kt)
            out = []
            for h in heads:
                m_old, acc = mc[2 * h], mc[2 * h + 1]
                sc = sc_ref[h]
                sc_next = lax.dot_general(kaug_ref[h, pl.ds(k0, kt), :], qas[h], nt,
                                          preferred_element_type=f32)
                m_new = jnp.maximum(m_old, jnp.max(sc, axis=0, keepdims=True))
                alpha = jnp.exp2(m_old - m_new)
                p = jnp.exp2(sc - m_new).astype(bf)
                pv = jnp.dot(vaug_ref[h, t_pv], p, preferred_element_type=f32)
                sc_ref[h] = sc_next
                out += [m_new, acc * alpha + pv]
            return tuple(out)

        m_init = jnp.full((1, tq), NEG, f32)
        acc_init = jnp.zeros((vr, tq), f32)
        def multi_step(ii, mc):
            for u in range(MOBA_UNROLL):
                mc = kv_step(MOBA_UNROLL * ii + u, mc)
            return mc

        n_steps = t_last + 1
        n_main = n_steps // MOBA_UNROLL
        mc = lax.fori_loop(0, n_main, multi_step, (m_init, acc_init, m_init, acc_init))
        fin = lax.fori_loop(n_main * MOBA_UNROLL, n_steps, kv_step, mc)
        o0 = fin[1][:HEAD_DIM] / fin[1][HEAD_DIM:HEAD_DIM + 1]
        o1 = fin[3][MOBA_ONES:] / fin[3][0:1]
        o_ref[pl.ds(r0, tq), :] = jnp.concatenate([o0, o1], axis=0).T.astype(o_ref.dtype)
        prepare(jnp.minimum(t_last + 1, n_tiles - 1))
        return carry

    lax.fori_loop(0, n_tiles, q_tile, 0)


def _moba(p3):
    b, s, _ = p3.shape
    n_pairs = MOBA_HEADS // 2
    blk = (None, s, LANES)
    return pl.pallas_call(
        _moba_kernel,
        grid=(b, n_pairs),
        in_specs=[pl.BlockSpec(blk, lambda i, j: (i, 0, j)),
                  pl.BlockSpec(blk, lambda i, j: (i, 0, n_pairs + j)),
                  pl.BlockSpec(blk, lambda i, j: (i, 0, 2 * n_pairs + j))],
        out_specs=pl.BlockSpec(blk, lambda i, j: (i, 0, j)),
        out_shape=jax.ShapeDtypeStruct((b, s, MOBA_W), jnp.bfloat16),
        scratch_shapes=[pltpu.VMEM((2, s, LANES), jnp.bfloat16),
                        pltpu.VMEM((2, s // MOBA_KTILE, HEAD_DIM + MOBA_ONES, MOBA_KTILE),
                                   jnp.bfloat16),
                        pltpu.VMEM((LANES, LANES), jnp.bfloat16),
                        pltpu.VMEM((2, MOBA_KTILE, MOBA_KTILE), jnp.float32),
                        pltpu.VMEM((2, MOBA_KTILE, LANES), jnp.bfloat16)],
        compiler_params=_cparams(("arbitrary", "arbitrary")),
        name="moba_attn",
    )(p3, p3, p3)


def _dil_kernel(q_ref, k_ref, v_ref, o_ref, lse_ref):
    tq = q_ref.shape[0]
    sub = DIL_SPAN
    kw = 2 * sub
    base_tile = pl.program_id(2) * tq
    lane = lax.broadcasted_iota(jnp.int32, (sub, LANES), 1)
    first = lane < HEAD_DIM
    diff = (lax.broadcasted_iota(jnp.int32, (sub, kw), 0)
            - lax.broadcasted_iota(jnp.int32, (sub, kw), 1))
    band = lambda dist: jnp.where((dist >= 0) & (dist <= DIL_SPAN), 0.0, NEG)
    bias_in = band(diff + sub)
    bias_first = jnp.where(base_tile == 0, band(diff), bias_in)
    nt = (((1,), (1,)), ((), ()))

    for sb in range(tq // sub):
        q0 = sb * sub
        kstart = pl.multiple_of(jnp.maximum(base_tile + q0 - sub, 0), sub)
        bias = bias_first if sb == 0 else bias_in
        for pr in range(DIL_HEADS_PER_GROUP // 2):
            cs = slice(pr * LANES, (pr + 1) * LANES)
            q2 = q_ref[pl.ds(q0, sub), cs].astype(jnp.float32)
            k2 = k_ref[pl.ds(kstart, kw), cs]
            v2 = v_ref[pl.ds(kstart, kw), cs]
            outs, lses = [], []
            for h in range(2):
                qh = jnp.where(first if h == 0 else jnp.logical_not(first), q2, 0.0)
                qh = qh.astype(jnp.bfloat16)
                sc = lax.dot_general(qh, k2, nt, preferred_element_type=jnp.float32) + bias
                m = jnp.max(sc, axis=1, keepdims=True)
                p = jnp.exp2(sc - m)
                l = jnp.sum(p, axis=1, keepdims=True)
                o = jnp.dot(p.astype(jnp.bfloat16), v2, preferred_element_type=jnp.float32)
                outs.append(o / l)
                lses.append(jnp.broadcast_to(m * LN2 + jnp.log(l), (sub, LANES)))
            o_ref[pl.ds(q0, sub), cs] = jnp.where(first, outs[0], outs[1]).astype(o_ref.dtype)
            lse_ref[pl.ds(q0, sub), cs] = jnp.where(first, lses[0], lses[1])


def _dil_group(pg, b, dil):
    w = DIL_OUT_W
    l = pg.shape[0] // b
    pv = pg.reshape(b, l, dil * 3 * w)
    tq = min(DIL_TQ, l)
    o, lse = pl.pallas_call(
        _dil_kernel,
        grid=(b, dil, l // tq),
        in_specs=[pl.BlockSpec((None, tq, w), lambda i, r, t: (i, t, 3 * r)),
                  pl.BlockSpec((None, l, w), lambda i, r, t: (i, 0, 3 * r + 1)),
                  pl.BlockSpec((None, l, w), lambda i, r, t: (i, 0, 3 * r + 2))],
        out_specs=[pl.BlockSpec((None, tq, w), lambda i, r, t: (i, t, r)),
                   pl.BlockSpec((None, tq, w), lambda i, r, t: (i, t, r))],
        out_shape=[jax.ShapeDtypeStruct((b, l, dil * w), jnp.bfloat16),
                   jax.ShapeDtypeStruct((b, l, dil * w), jnp.float32)],
        compiler_params=_cparams(("arbitrary", "arbitrary", "arbitrary")),
        name=f"dil_attn_d{dil}",
    )(pv, pv, pv)
    return o, lse


def _layer_norm(y, g, b):
    mu = jnp.mean(y, axis=-1, keepdims=True)
    yc = y - mu
    var = jnp.mean(yc * yc, axis=-1, keepdims=True)
    return yc * lax.rsqrt(var + LN_EPS) * g + b


def _twice_sigmoid_of_double(h):
    return 1.0 + jnp.tanh(h)


def _natural_order(src_ref, dst_ref, a, dil):
    rows = src_ref.shape[0]
    slabs = DIL_OUT_W // LANES
    for r in range(dil):
        for sl in range(slabs):
            c = r * DIL_OUT_W + sl * LANES
            dst_ref[a, sl, pl.ds(r, rows, stride=dil), :] = (
                src_ref[:, c:c + LANES].astype(jnp.float32))
    return jnp.concatenate([dst_ref[a, sl] for sl in range(slabs)], axis=1)


def _merge_kernel(alpha, x_ref, ym_ref, bg_ref, cg_ref, hc_ref, cgp_ref, hcp_ref,
                  g0_ref, g1_ref, g2_ref, o1_ref, o2_ref, o3_ref, l1_ref, l2_ref, l3_ref,
                  wm_ref, wc_ref, wd_ref, wo_ref, wsc_ref, lng_ref, lnb_ref, out_ref, nat_ref):
    f32 = jnp.float32
    tm = x_ref.shape[0]
    ch = cg_ref[...].astype(f32) * hc_ref[...].astype(f32)
    chp = cgp_ref[...].astype(f32) * hcp_ref[...].astype(f32)
    chp = jnp.where(pl.program_id(1) == 0, 0.0, chp)
    hp = chp.shape[0]
    prev1 = chp[hp - 1:hp, :]
    prev2 = chp[hp - 2:hp - 1, :]
    row = lax.broadcasted_iota(jnp.int32, (tm, CONV_WIDTH), 0)
    ch1 = jnp.where(row == 0, prev1, pltpu.roll(ch, 1, axis=0))
    ch2 = pltpu.roll(ch, 2, axis=0)
    ch2 = jnp.where(row == 0, prev2, jnp.where(row == 1, prev1, ch2))
    wsc = wsc_ref[...]
    conv = ch * wsc[0:1, :] + ch1 * wsc[1:2, :] + ch2 * wsc[2:3, :]
    y_conv = (bg_ref[...].astype(f32) * conv).astype(jnp.bfloat16)

    d2, d3 = DIL_GROUPS[1][1], DIL_GROUPS[2][1]
    o1, l1 = o1_ref[...].astype(f32), l1_ref[...]
    o2, l2 = _natural_order(o2_ref, nat_ref, 0, d2), _natural_order(l2_ref, nat_ref, 1, d2)
    o3, l3 = _natural_order(o3_ref, nat_ref, 2, d3), _natural_order(l3_ref, nat_ref, 3, d3)
    lm = jnp.maximum(jnp.maximum(l1, l2), l3)
    e1, e2, e3 = jnp.exp(l1 - lm), jnp.exp(l2 - lm), jnp.exp(l3 - lm)
    y_dil = ((e1 * o1 + e2 * o2 + e3 * o3) / (e1 + e2 + e3)).astype(jnp.bfloat16)

    pm = jnp.dot(ym_ref[...], wm_ref[...], preferred_element_type=f32)
    pc = jnp.dot(y_conv, wc_ref[...], preferred_element_type=f32)
    pd = jnp.dot(y_dil, wd_ref[...], preferred_element_type=f32)
    merged = (_twice_sigmoid_of_double(g0_ref[...].astype(f32)) * pm
              + _twice_sigmoid_of_double(g1_ref[...].astype(f32)) * pc
              + _twice_sigmoid_of_double(g2_ref[...].astype(f32)) * pd)
    mix = jnp.dot(merged.astype(jnp.bfloat16), wo_ref[...], preferred_element_type=f32)
    y = alpha * x_ref[...] + mix
    out_ref[...] = _layer_norm(y, lng_ref[...], lnb_ref[...])


def _merge(alpha, x3, ym, p3, dil_outs, wm, wc, wd, wo, wsc, lng, lnb):
    b, s, d = x3.shape
    tm = MERGE_TM
    hp = 16
    cw = CONV_WIDTH
    cblk = OFF_CONV // cw
    gblk = OFF_GATE // d
    row = lambda w: pl.BlockSpec((None, tm, w), lambda i, t: (i, t, 0))
    pcol = lambda w, c: pl.BlockSpec((None, tm, w), lambda i, t, c=c: (i, t, c))
    prev = lambda c: pl.BlockSpec(
        (None, hp, cw), lambda i, t, c=c: (i, jnp.maximum(t * (tm // hp) - 1, 0), c))
    full = lambda a: pl.BlockSpec(a.shape, lambda i, t: (0,) * a.ndim)
    (o1, l1), (o2, l2), (o3, l3) = dil_outs
    dils = [dil for _, dil in DIL_GROUPS]
    grp = lambda dil: pl.BlockSpec((None, tm // dil, dil * DIL_OUT_W), lambda i, t: (i, t, 0))
    in_specs = [row(d), row(MOBA_W),
                pcol(cw, cblk), pcol(cw, cblk + 1), pcol(cw, cblk + 2),
                prev(cblk + 1), prev(cblk + 2),
                pcol(d, gblk), pcol(d, gblk + 1), pcol(d, gblk + 2),
                grp(dils[0]), grp(dils[1]), grp(dils[2]),
                grp(dils[0]), grp(dils[1]), grp(dils[2]),
                full(wm), full(wc), full(wd), full(wo), full(wsc), full(lng), full(lnb)]
    return pl.pallas_call(
        functools.partial(_merge_kernel, alpha),
        grid=(b, s // tm),
        in_specs=in_specs,
        out_specs=row(d),
        out_shape=jax.ShapeDtypeStruct((b, s, d), jnp.float32),
        scratch_shapes=[pltpu.VMEM((4, DIL_OUT_W // LANES, tm, LANES), jnp.float32)],
        compiler_params=_cparams(("arbitrary", "arbitrary")),
        name="merge",
    )(x3, ym, p3, p3, p3, p3, p3, p3, p3, p3, o1, o2, o3, l1, l2, l3,
      wm, wc, wd, wo, wsc, lng, lnb)


def _ffn_kernel(alpha, x_ref, xp_ref, wup_ref, wdn_ref, wcv_ref, bcv_ref, lng_ref, lnb_ref,
                out_ref, xs_ref, act_ref):
    f32 = jnp.float32
    tm = x_ref.shape[0]
    halo = xp_ref.shape[0]
    xp = jnp.where(pl.program_id(1) == 0, 0.0, xp_ref[...])
    xs_ref[0:halo, :] = xp.astype(jnp.bfloat16)
    xs_ref[halo:halo + tm, :] = x_ref[...].astype(jnp.bfloat16)

    def conv(u, w):
        y = u * w[0:1, :] + pltpu.roll(u, 1, axis=0) * w[1:2, :] + pltpu.roll(u, 2, axis=0) * w[2:3, :]
        return y[halo:, :]

    for c in range(FFN_NCHUNK):
        gc = slice(c * FFN_CHUNK, (c + 1) * FFN_CHUNK)
        vc = slice(D_FF + c * FFN_CHUNK, D_FF + (c + 1) * FFN_CHUNK)
        xs = xs_ref[...]
        ug = jnp.dot(xs, wup_ref[:, gc], preferred_element_type=f32)
        uv = jnp.dot(xs, wup_ref[:, vc], preferred_element_type=f32)
        hg = conv(ug, wcv_ref[:, gc]) + bcv_ref[:, gc]
        yv = conv(uv, wcv_ref[:, vc]) + bcv_ref[:, vc]
        act_ref[:, gc] = (hg * _twice_sigmoid_of_double(hg) * yv).astype(jnp.bfloat16)

    ffn = jnp.dot(act_ref[...], wdn_ref[...], preferred_element_type=f32)
    y = alpha * x_ref[...] + ffn
    out_ref[...] = _layer_norm(y, lng_ref[...], lnb_ref[...])


def _ffn(alpha, x3, wup, wdn, wcv, bcv, lng, lnb):
    b, s, d = x3.shape
    tm = FFN_TM
    halo = FFN_HALO
    full = lambda a: pl.BlockSpec(a.shape, lambda i, t: (0,) * a.ndim)
    return pl.pallas_call(
        functools.partial(_ffn_kernel, alpha),
        grid=(b, s // tm),
        in_specs=[pl.BlockSpec((None, tm, d), lambda i, t: (i, t, 0)),
                  pl.BlockSpec((None, halo, d),
                               lambda i, t: (i, jnp.maximum(t * (tm // halo) - 1, 0), 0)),
                  full(wup), full(wdn), full(wcv), full(bcv), full(lng), full(lnb)],
        out_specs=pl.BlockSpec((None, tm, d), lambda i, t: (i, t, 0)),
        out_shape=jax.ShapeDtypeStruct((b, s, d), jnp.float32),
        scratch_shapes=[pltpu.VMEM((tm + halo, d), jnp.bfloat16),
                        pltpu.VMEM((tm, D_FF), jnp.bfloat16)],
        compiler_params=_cparams(("arbitrary", "arbitrary")),
        name="conv_ffn",
    )(x3, x3, wup, wdn, wcv, bcv, lng, lnb)


def kernel(x, w_in, w_short_conv, w_moba_proj, w_dil_proj, w_conv_proj, w_mix_out,
           ln1_g, ln1_b, w_up, w_ffn_conv, b_ffn_conv, w_down, ln2_g, ln2_b):
    b, s, d = x.shape
    depth = w_in.shape[0]
    alpha = float((2 * depth) ** 0.25)
    bf = jnp.bfloat16
    for l in range(depth):
        x2 = x.reshape(b * s, d)
        p = _in_proj(x2, w_in, l).reshape(b, s, MAIN_COLS)
        pgs = _dil_proj(x2, w_in, l)
        ym = _moba(p)
        dil_outs = [_dil_group(pg, b, dil) for pg, (_, dil) in zip(pgs, DIL_GROUPS)]
        half = lambda w: (0.5 * w).astype(bf)
        x = _merge(alpha, x, ym, p, dil_outs,
                   half(w_moba_proj[l]), half(w_conv_proj[l]), half(w_dil_proj[l]),
                   w_mix_out[l].astype(bf),
                   w_short_conv[l], ln1_g[l].reshape(1, d), ln1_b[l].reshape(1, d))
        gate_half = jnp.where(jnp.arange(2 * D_FF) < D_FF, 0.5, 1.0)
        x = _ffn(alpha, x, w_up[l].astype(bf), w_down[l].astype(bf),
                 w_ffn_conv[l] * gate_half, (b_ffn_conv[l] * gate_half).reshape(1, 2 * D_FF),
                 ln2_g[l].reshape(1, d), ln2_b[l].reshape(1, d))
    return x
```

```python
import functools

import jax
import jax.numpy as jnp
import numpy as np
from jax import lax
from jax.experimental import pallas as pl
from jax.experimental.pallas import tpu as pltpu

D_MODEL = 1024
HEAD_DIM = 64
MOBA_HEADS = 8
MOBA_BLOCK = 256
MOBA_TOPK = 3
MOBA_W = MOBA_HEADS * HEAD_DIM
DIL_GROUPS = ((128, 1), (512, 4), (2048, 16))
DIL_HEADS_PER_GROUP = 4
DIL_W = 3 * DIL_HEADS_PER_GROUP * HEAD_DIM
DIL_OUT_W = DIL_HEADS_PER_GROUP * HEAD_DIM
CONV_WIDTH = 512
N_BRANCH = 3
D_FF = 2816
LN_EPS = 1e-5

OFF_CONV = 3 * MOBA_W
OFF_GATE = OFF_CONV + 3 * CONV_WIDTH
MAIN_COLS = OFF_GATE + N_BRANCH * D_MODEL

LOG2E = float(np.log2(np.e))
LN2 = float(np.log(2.0))
Q_SCALE = HEAD_DIM ** -0.5 * LOG2E

LANES = 128
NEG = -1e30
VMEM_LIMIT = 56 * 1024 * 1024

PROJ_TM = 2048
PROJ_TN = 768
DILP_TM = 1024
MOBA_KTILE = 2 * MOBA_BLOCK
MOBA_UNROLL = 2
MOBA_ONES = 16
DIL_TQ = 2048
DIL_SPAN = 128
MERGE_TM = 512
FFN_TM = 512
FFN_HALO = 16
FFN_CHUNK = 256
FFN_NCHUNK = D_FF // FFN_CHUNK


def _cparams(sem):
    return pltpu.CompilerParams(dimension_semantics=sem, vmem_limit_bytes=VMEM_LIMIT)


def _proj_kernel(x_ref, w_ref, cs_ref, o_ref, xb_ref, wb_ref):
    j = pl.program_id(1)

    @pl.when(j == 0)
    def _():
        xb_ref[...] = x_ref[...].astype(jnp.bfloat16)

    @pl.when(pl.program_id(0) == 0)
    def _():
        wb_ref[j] = w_ref[...].astype(jnp.bfloat16)

    acc = jnp.dot(xb_ref[...], wb_ref[j], preferred_element_type=jnp.float32)
    o_ref[...] = (acc * cs_ref[...]).astype(o_ref.dtype)


def _main_scale_row():
    cs = np.ones((1, MAIN_COLS), np.float32)
    cs[:, :MOBA_W] = Q_SCALE
    cs[:, OFF_GATE:] = 0.5
    return jnp.asarray(cs)


def _in_proj(x2, w_in, layer):
    n, d = x2.shape
    tn = PROJ_TN
    skip_at = 3 * MOBA_W // tn
    skip = 3 * DIL_W // tn
    n_ct = MAIN_COLS // tn
    src = lambda j: jnp.where(j < skip_at, j, j + skip)
    w_tile = lambda i, j: src(jnp.where(i == 0, j, n_ct - 1))
    return pl.pallas_call(
        _proj_kernel,
        grid=(n // PROJ_TM, n_ct),
        in_specs=[pl.BlockSpec((PROJ_TM, d), lambda i, j: (i, 0)),
                  pl.BlockSpec((None, d, tn), lambda i, j: (layer, 0, w_tile(i, j))),
                  pl.BlockSpec((1, tn), lambda i, j: (0, j))],
        out_specs=pl.BlockSpec((PROJ_TM, tn), lambda i, j: (i, j)),
        out_shape=jax.ShapeDtypeStruct((n, MAIN_COLS), jnp.bfloat16),
        scratch_shapes=[pltpu.VMEM((PROJ_TM, d), jnp.bfloat16),
                        pltpu.VMEM((n_ct, d, tn), jnp.bfloat16)],
        compiler_params=_cparams(("arbitrary", "arbitrary")),
        name="in_proj",
    )(x2, w_in, _main_scale_row())


def _dil_proj_kernel(x_ref, wq_ref, wk_ref, wv_ref, o0_ref, o1_ref, o2_ref, wb_ref, res_ref):
    tm = x_ref.shape[0]

    @pl.when(pl.program_id(0) == 0)
    def _():
        for j, w_ref in enumerate((wq_ref, wk_ref, wv_ref)):
            wb_ref[j] = w_ref[...].astype(jnp.bfloat16)

    xb = x_ref[...].astype(jnp.bfloat16)
    per_part = DIL_W // LANES
    for j in range(3):
        res = jnp.dot(xb, wb_ref[j], preferred_element_type=jnp.float32)
        if j == 0:
            res = res * Q_SCALE
        for cb in range(per_part):
            res_ref[j * per_part + cb] = res[:, cb * LANES:(cb + 1) * LANES]
    slabs = DIL_OUT_W // LANES
    for g, o_ref in enumerate((o0_ref, o1_ref, o2_ref)):
        d = DIL_GROUPS[g][1]
        rows = tm // d
        for r in range(d):
            for j in range(3):
                for sl in range(slabs):
                    src_cb = (j * DIL_W + g * DIL_OUT_W) // LANES + sl
                    dst_c = (r * 3 + j) * DIL_OUT_W + sl * LANES
                    if d == 1:
                        src = res_ref[src_cb]
                    else:
                        src = res_ref[src_cb, pl.ds(r, rows, stride=d), :]
                    o_ref[:, dst_c:dst_c + LANES] = src.astype(o_ref.dtype)


def _dil_proj(x2, w_in, layer):
    n, d = x2.shape
    tm = DILP_TM
    dils = [dil for _, dil in DIL_GROUPS]
    first = 3 * MOBA_W // DIL_W
    wspec = lambda j: pl.BlockSpec((None, d, DIL_W), lambda i, j=j: (layer, 0, first + j))
    return pl.pallas_call(
        _dil_proj_kernel,
        grid=(n // tm,),
        in_specs=[pl.BlockSpec((tm, d), lambda i: (i, 0)), wspec(0), wspec(1), wspec(2)],
        out_specs=[pl.BlockSpec((tm // dil, dil * 3 * DIL_OUT_W), lambda i: (i, 0))
                   for dil in dils],
        out_shape=[jax.ShapeDtypeStruct((n // dil, dil * 3 * DIL_OUT_W), jnp.bfloat16)
                   for dil in dils],
        scratch_shapes=[pltpu.VMEM((3, d, DIL_W), jnp.bfloat16),
                        pltpu.VMEM((3 * DIL_W // LANES, tm, LANES), jnp.float32)],
        compiler_params=_cparams(("arbitrary",)),
        name="dil_proj",
    )(x2, w_in, w_in, w_in)


def _moba_kernel(q_ref, k_ref, v_ref, o_ref, kaug_ref, vaug_ref, km_ref, sc_ref, qa_ref):
    s = q_ref.shape[0]
    nb = s // MOBA_BLOCK
    tq = kt = MOBA_KTILE
    f32 = jnp.float32
    bf = jnp.bfloat16
    sub_bf = 16
    lane_r = lax.broadcasted_iota(jnp.int32, (sub_bf, LANES), 1)
    row01 = lambda cond: cond.astype(f32).astype(bf)
    blk01 = lambda cond: jnp.broadcast_to(
        row01(cond)[None], (MOBA_BLOCK // sub_bf, sub_bf, LANES)).reshape(MOBA_BLOCK, LANES)
    head_lanes = (blk01(lane_r < HEAD_DIM), blk01(lane_r >= HEAD_DIM))
    for n in range(nb):
        rows = slice(n * MOBA_BLOCK, (n + 1) * MOBA_BLOCK)
        kblk = k_ref[rows, :]
        kaug_ref[0, rows, :] = kblk * head_lanes[0] + blk01(lane_r == HEAD_DIM + n)
        kaug_ref[1, rows, :] = kblk * head_lanes[1] + blk01(lane_r == n)
    vr = HEAD_DIM + MOBA_ONES
    lane_k = lax.broadcasted_iota(jnp.int32, (kt, LANES), 1)
    for t in range(s // kt):
        vt = v_ref[t * kt:(t + 1) * kt, :].astype(f32)
        vaug_ref[0, t] = jnp.where(lane_k < HEAD_DIM, vt, 1.0).T[:vr].astype(bf)
        vaug_ref[1, t] = jnp.where(lane_k >= HEAD_DIM, vt, 1.0).T[LANES - vr:].astype(bf)

    offs = (HEAD_DIM, 0)
    in_blk = (lax.broadcasted_iota(jnp.int32, (nb, s), 1) // MOBA_BLOCK
              == lax.broadcasted_iota(jnp.int32, (nb, s), 0))
    ksum = jnp.dot(row01(in_blk), k_ref[...], preferred_element_type=f32)
    kmean = ksum * (1.0 / MOBA_BLOCK)
    lane_b = lax.broadcasted_iota(jnp.int32, (nb, LANES), 1)
    km0 = jnp.where(lane_b < HEAD_DIM, kmean, 0.0)
    km1 = jnp.where(lane_b >= HEAD_DIM, kmean, 0.0)
    zgap = jnp.zeros((HEAD_DIM - nb, LANES), f32)
    km_ref[...] = jnp.concatenate([km1, zgap, km0, zgap], axis=0).astype(bf)

    lane_t = lax.broadcasted_iota(jnp.int32, (tq, LANES), 1)
    causal = (lax.broadcasted_iota(jnp.int32, (kt, tq), 0)
              <= lax.broadcasted_iota(jnp.int32, (kt, tq), 1))
    blk_i = lax.broadcasted_iota(jnp.int32, (nb, tq), 0)
    blk_f = blk_i.astype(f32)
    q_sub = lax.broadcasted_iota(jnp.int32, (nb, tq), 1) // MOBA_BLOCK
    zgap_t = jnp.zeros((HEAD_DIM - nb, tq), f32)
    nt = (((1,), (1,)), ((), ()))
    heads = (0, 1)
    q_lanes = (lane_t < HEAD_DIM, lane_t >= HEAD_DIM)

    def augmented_queries(tile):
        q2 = q_ref[pl.ds(pl.multiple_of(tile * tq, tq), tq), :]
        q2f = q2.astype(f32)
        gate_t = lax.dot_general(km_ref[...], q2, nt, preferred_element_type=f32)
        j = tile * (tq // MOBA_BLOCK) + q_sub
        valid = blk_i < j
        pens = []
        for h in heads:
            g = jnp.where(valid, gate_t[offs[h]:offs[h] + nb, :], NEG)
            sel = jnp.zeros((nb, tq), jnp.bool_)
            for _ in range(MOBA_TOPK):
                m = jnp.max(g, axis=0, keepdims=True)
                first = jnp.min(jnp.where(g == m, blk_f, 1e9), axis=0, keepdims=True)
                pick = blk_f == first
                sel = sel | pick
                g = jnp.where(pick, NEG, g)
            keep = (sel & valid) | (blk_i == j)
            pens.append(jnp.where(keep, 0.0, NEG))
        pen = jnp.concatenate([pens[1], zgap_t, pens[0], zgap_t], axis=0).T
        return [jnp.where(q_lanes[h], q2f, pen).astype(bf) for h in heads]

    def prepare(tile):
        r0 = pl.multiple_of(tile * tq, tq)
        for h, qa in enumerate(augmented_queries(tile)):
            sc = lax.dot_general(kaug_ref[h, pl.ds(r0, kt), :], qa, nt,
                                 preferred_element_type=f32)
            sc_ref[h] = jnp.where(causal, sc, NEG)
            qa_ref[h] = qa

    n_tiles = s // tq
    prepare(0)

    def q_tile(t_last, carry):
        r0 = pl.multiple_of(t_last * tq, tq)
        qas = [qa_ref[h] for h in heads]

        def tile_of(visit):
            return jnp.where(visit <= 0, t_last, visit - 1)

        def kv_step(i, mc):
            t_pv = tile_of(i)
            k0 = pl.multiple_of(jnp.clip(i, 0, jnp.maximum(t_last - 1, 0)) * kt, kt)
            out = []
            for h in heads:
                m_old, acc = mc[2 * h], mc[2 * h + 1]
                sc = sc_ref[h]
                sc_next = lax.dot_general(kaug_ref[h, pl.ds(k0, kt), :], qas[h], nt,
                                          preferred_element_type=f32)
                m_new = jnp.maximum(m_old, jnp.max(sc, axis=0, keepdims=True))
                alpha = jnp.exp2(m_old - m_new)
                p = jnp.exp2(sc - m_new).astype(bf)
                pv = jnp.dot(vaug_ref[h, t_pv], p, preferred_element_type=f32)
                sc_ref[h] = sc_next
                out += [m_new, acc * alpha + pv]
            return tuple(out)

        m_init = jnp.full((1, tq), NEG, f32)
        acc_init = jnp.zeros((vr, tq), f32)
        def multi_step(ii, mc):
            for u in range(MOBA_UNROLL):
                mc = kv_step(MOBA_UNROLL * ii + u, mc)
            return mc

        n_steps = t_last + 1
        n_main = n_steps // MOBA_UNROLL
        mc = lax.fori_loop(0, n_main, multi_step, (m_init, acc_init, m_init, acc_init))
        fin = lax.fori_loop(n_main * MOBA_UNROLL, n_steps, kv_step, mc)
        o0 = fin[1][:HEAD_DIM] / fin[1][HEAD_DIM:HEAD_DIM + 1]
        o1 = fin[3][MOBA_ONES:] / fin[3][0:1]
        o_ref[pl.ds(r0, tq), :] = jnp.concatenate([o0, o1], axis=0).T.astype(o_ref.dtype)
        prepare(jnp.minimum(t_last + 1, n_tiles - 1))
        return carry

    lax.fori_loop(0, n_tiles, q_tile, 0)


def _moba(p3):
    b, s, _ = p3.shape
    n_pairs = MOBA_HEADS // 2
    blk = (None, s, LANES)
    return pl.pallas_call(
        _moba_kernel,
        grid=(b, n_pairs),
        in_specs=[pl.BlockSpec(blk, lambda i, j: (i, 0, j)),
                  pl.BlockSpec(blk, lambda i, j: (i, 0, n_pairs + j)),
                  pl.BlockSpec(blk, lambda i, j: (i, 0, 2 * n_pairs + j))],
        out_specs=pl.BlockSpec(blk, lambda i, j: (i, 0, j)),
        out_shape=jax.ShapeDtypeStruct((b, s, MOBA_W), jnp.bfloat16),
        scratch_shapes=[pltpu.VMEM((2, s, LANES), jnp.bfloat16),
                        pltpu.VMEM((2, s // MOBA_KTILE, HEAD_DIM + MOBA_ONES, MOBA_KTILE),
                                   jnp.bfloat16),
                        pltpu.VMEM((LANES, LANES), jnp.bfloat16),
                        pltpu.VMEM((2, MOBA_KTILE, MOBA_KTILE), jnp.float32),
                        pltpu.VMEM((2, MOBA_KTILE, LANES), jnp.bfloat16)],
        compiler_params=_cparams(("arbitrary", "arbitrary")),
        name="moba_attn",
    )(p3, p3, p3)


def _dil_kernel(n_res, *refs):
    q_refs, k_refs, v_refs = refs[:n_res], refs[n_res:2 * n_res], refs[2 * n_res:3 * n_res]
    o_ref, lse_ref = refs[3 * n_res:]
    tq = q_refs[0].shape[0]
    sub = DIL_SPAN
    kw = 2 * sub
    base_tile = pl.program_id(2) * tq
    lane = lax.broadcasted_iota(jnp.int32, (sub, LANES), 1)
    first = lane < HEAD_DIM
    diff = (lax.broadcasted_iota(jnp.int32, (sub, kw), 0)
            - lax.broadcasted_iota(jnp.int32, (sub, kw), 1))
    band = lambda dist: jnp.where((dist >= 0) & (dist <= DIL_SPAN), 0.0, NEG)
    bias_in = band(diff + sub)
    bias_first = jnp.where(base_tile == 0, band(diff), bias_in)
    nt = (((1,), (1,)), ((), ()))

    units = [(ri, sb, pr) for ri in range(n_res) for sb in range(tq // sub)
             for pr in range(DIL_HEADS_PER_GROUP // 2)]
    for ri, sb, pr in units:
        q0 = sb * sub
        kstart = pl.multiple_of(jnp.maximum(base_tile + q0 - sub, 0), sub)
        bias = bias_first if sb == 0 else bias_in
        cs = slice(pr * LANES, (pr + 1) * LANES)
        oc = slice(ri * DIL_OUT_W + pr * LANES, ri * DIL_OUT_W + (pr + 1) * LANES)
        q2 = q_refs[ri][pl.ds(q0, sub), cs].astype(jnp.float32)
        k2 = k_refs[ri][pl.ds(kstart, kw), cs]
        v2 = v_refs[ri][pl.ds(kstart, kw), cs]
        outs, lses = [], []
        for h in range(2):
            qh = jnp.where(first if h == 0 else jnp.logical_not(first), q2, 0.0)
            qh = qh.astype(jnp.bfloat16)
            sc = lax.dot_general(qh, k2, nt, preferred_element_type=jnp.float32) + bias
            m = jnp.max(sc, axis=1, keepdims=True)
            p = jnp.exp2(sc - m)
            l = jnp.sum(p, axis=1, keepdims=True)
            o = jnp.dot(p.astype(jnp.bfloat16), v2, preferred_element_type=jnp.float32)
            outs.append(o / l)
            lses.append(jnp.broadcast_to(m * LN2 + jnp.log(l), (sub, LANES)))
        o_ref[pl.ds(q0, sub), oc] = jnp.where(first, outs[0], outs[1]).astype(o_ref.dtype)
        lse_ref[pl.ds(q0, sub), oc] = jnp.where(first, lses[0], lses[1])


def _dil_group(pg, b, dil):
    w = DIL_OUT_W
    l = pg.shape[0] // b
    pv = pg.reshape(b, l, dil * 3 * w)
    tq = min(DIL_TQ, l)
    n_res = min(dil, DIL_TQ // tq)
    qkv = lambda part, width, rows: [
        pl.BlockSpec((None, rows, width),
                     lambda i, r, t, ri=ri: (i, t if part == 0 else 0, 3 * (r * n_res + ri) + part))
        for ri in range(n_res)]
    out = pl.BlockSpec((None, tq, n_res * w), lambda i, r, t: (i, t, r))
    o, lse = pl.pallas_call(
        functools.partial(_dil_kernel, n_res),
        grid=(b, dil // n_res, l // tq),
        in_specs=qkv(0, w, tq) + qkv(1, w, l) + qkv(2, w, l),
        out_specs=[out, out],
        out_shape=[jax.ShapeDtypeStruct((b, l, dil * w), jnp.bfloat16),
                   jax.ShapeDtypeStruct((b, l, dil * w), jnp.float32)],
        compiler_params=_cparams(("arbitrary", "arbitrary", "arbitrary")),
        name=f"dil_attn_d{dil}",
    )(*([pv] * (3 * n_res)))
    return o, lse


def _layer_norm(y, g, b):
    mu = jnp.mean(y, axis=-1, keepdims=True)
    yc = y - mu
    var = jnp.mean(yc * yc, axis=-1, keepdims=True)
    return yc * lax.rsqrt(var + LN_EPS) * g + b


def _twice_sigmoid_of_double(h):
    return 1.0 + jnp.tanh(h)


def _natural_order(src_ref, dst_ref, a, dil):
    rows = src_ref.shape[0]
    slabs = DIL_OUT_W // LANES
    for r in range(dil):
        for sl in range(slabs):
            c = r * DIL_OUT_W + sl * LANES
            dst_ref[a, sl, pl.ds(r, rows, stride=dil), :] = (
                src_ref[:, c:c + LANES].astype(jnp.float32))
    return jnp.concatenate([dst_ref[a, sl] for sl in range(slabs)], axis=1)


def _merge_kernel(alpha, x_ref, ym_ref, bg_ref, cg_ref, hc_ref, cgp_ref, hcp_ref,
                  g0_ref, g1_ref, g2_ref, o1_ref, o2_ref, o3_ref, l1_ref, l2_ref, l3_ref,
                  wm_ref, wc_ref, wd_ref, wo_ref, wsc_ref, lng_ref, lnb_ref, out_ref, nat_ref):
    f32 = jnp.float32
    tm = x_ref.shape[0]
    ch = cg_ref[...].astype(f32) * hc_ref[...].astype(f32)
    chp = cgp_ref[...].astype(f32) * hcp_ref[...].astype(f32)
    chp = jnp.where(pl.program_id(1) == 0, 0.0, chp)
    hp = chp.shape[0]
    prev1 = chp[hp - 1:hp, :]
    prev2 = chp[hp - 2:hp - 1, :]
    row = lax.broadcasted_iota(jnp.int32, (tm, CONV_WIDTH), 0)
    ch1 = jnp.where(row == 0, prev1, pltpu.roll(ch, 1, axis=0))
    ch2 = pltpu.roll(ch, 2, axis=0)
    ch2 = jnp.where(row == 0, prev2, jnp.where(row == 1, prev1, ch2))
    wsc = wsc_ref[...]
    conv = ch * wsc[0:1, :] + ch1 * wsc[1:2, :] + ch2 * wsc[2:3, :]
    y_conv = (bg_ref[...].astype(f32) * conv).astype(jnp.bfloat16)

    d2, d3 = DIL_GROUPS[1][1], DIL_GROUPS[2][1]
    o1, l1 = o1_ref[...].astype(f32), l1_ref[...]
    o2, l2 = _natural_order(o2_ref, nat_ref, 0, d2), _natural_order(l2_ref, nat_ref, 1, d2)
    o3, l3 = _natural_order(o3_ref, nat_ref, 2, d3), _natural_order(l3_ref, nat_ref, 3, d3)
    lm = jnp.maximum(jnp.maximum(l1, l2), l3)
    e1, e2, e3 = jnp.exp(l1 - lm), jnp.exp(l2 - lm), jnp.exp(l3 - lm)
    y_dil = ((e1 * o1 + e2 * o2 + e3 * o3) / (e1 + e2 + e3)).astype(jnp.bfloat16)

    pm = jnp.dot(ym_ref[...], wm_ref[...], preferred_element_type=f32)
    pc = jnp.dot(y_conv, wc_ref[...], preferred_element_type=f32)
    pd = jnp.dot(y_dil, wd_ref[...], preferred_element_type=f32)
    merged = (_twice_sigmoid_of_double(g0_ref[...].astype(f32)) * pm
              + _twice_sigmoid_of_double(g1_ref[...].astype(f32)) * pc
              + _twice_sigmoid_of_double(g2_ref[...].astype(f32)) * pd)
    mix = jnp.dot(merged.astype(jnp.bfloat16), wo_ref[...], preferred_element_type=f32)
    y = alpha * x_ref[...] + mix
    out_ref[...] = _layer_norm(y, lng_ref[...], lnb_ref[...])


def _merge(alpha, x3, ym, p3, dil_outs, wm, wc, wd, wo, wsc, lng, lnb):
    b, s, d = x3.shape
    tm = MERGE_TM
    hp = 16
    cw = CONV_WIDTH
    cblk = OFF_CONV // cw
    gblk = OFF_GATE // d
    row = lambda w: pl.BlockSpec((None, tm, w), lambda i, t: (i, t, 0))
    pcol = lambda w, c: pl.BlockSpec((None, tm, w), lambda i, t, c=c: (i, t, c))
    prev = lambda c: pl.BlockSpec(
        (None, hp, cw), lambda i, t, c=c: (i, jnp.maximum(t * (tm // hp) - 1, 0), c))
    full = lambda a: pl.BlockSpec(a.shape, lambda i, t: (0,) * a.ndim)
    (o1, l1), (o2, l2), (o3, l3) = dil_outs
    dils = [dil for _, dil in DIL_GROUPS]
    grp = lambda dil: pl.BlockSpec((None, tm // dil, dil * DIL_OUT_W), lambda i, t: (i, t, 0))
    in_specs = [row(d), row(MOBA_W),
                pcol(cw, cblk), pcol(cw, cblk + 1), pcol(cw, cblk + 2),
                prev(cblk + 1), prev(cblk + 2),
                pcol(d, gblk), pcol(d, gblk + 1), pcol(d, gblk + 2),
                grp(dils[0]), grp(dils[1]), grp(dils[2]),
                grp(dils[0]), grp(dils[1]), grp(dils[2]),
                full(wm), full(wc), full(wd), full(wo), full(wsc), full(lng), full(lnb)]
    return pl.pallas_call(
        functools.partial(_merge_kernel, alpha),
        grid=(b, s // tm),
        in_specs=in_specs,
        out_specs=row(d),
        out_shape=jax.ShapeDtypeStruct((b, s, d), jnp.float32),
        scratch_shapes=[pltpu.VMEM((4, DIL_OUT_W // LANES, tm, LANES), jnp.float32)],
        compiler_params=_cparams(("arbitrary", "arbitrary")),
        name="merge",
    )(x3, ym, p3, p3, p3, p3, p3, p3, p3, p3, o1, o2, o3, l1, l2, l3,
      wm, wc, wd, wo, wsc, lng, lnb)


def _ffn_kernel(alpha, x_ref, xp_ref, wup_ref, wdn_ref, wcv_ref, bcv_ref, lng_ref, lnb_ref,
                out_ref, xs_ref, act_ref):
    f32 = jnp.float32
    tm = x_ref.shape[0]
    halo = xp_ref.shape[0]
    xp = jnp.where(pl.program_id(1) == 0, 0.0, xp_ref[...])
    xs_ref[0:halo, :] = xp.astype(jnp.bfloat16)
    xs_ref[halo:halo + tm, :] = x_ref[...].astype(jnp.bfloat16)

    def conv(u, w):
        y = u * w[0:1, :] + pltpu.roll(u, 1, axis=0) * w[1:2, :] + pltpu.roll(u, 2, axis=0) * w[2:3, :]
        return y[halo:, :]

    for c in range(FFN_NCHUNK):
        gc = slice(c * FFN_CHUNK, (c + 1) * FFN_CHUNK)
        vc = slice(D_FF + c * FFN_CHUNK, D_FF + (c + 1) * FFN_CHUNK)
        xs = xs_ref[...]
        ug = jnp.dot(xs, wup_ref[:, gc], preferred_element_type=f32)
        uv = jnp.dot(xs, wup_ref[:, vc], preferred_element_type=f32)
        hg = conv(ug, wcv_ref[:, gc]) + bcv_ref[:, gc]
        yv = conv(uv, wcv_ref[:, vc]) + bcv_ref[:, vc]
        act_ref[:, gc] = (hg * _twice_sigmoid_of_double(hg) * yv).astype(jnp.bfloat16)

    ffn = jnp.dot(act_ref[...], wdn_ref[...], preferred_element_type=f32)
    y = alpha * x_ref[...] + ffn
    out_ref[...] = _layer_norm(y, lng_ref[...], lnb_ref[...])


def _ffn(alpha, x3, wup, wdn, wcv, bcv, lng, lnb):
    b, s, d = x3.shape
    tm = FFN_TM
    halo = FFN_HALO
    full = lambda a: pl.BlockSpec(a.shape, lambda i, t: (0,) * a.ndim)
    return pl.pallas_call(
        functools.partial(_ffn_kernel, alpha),
        grid=(b, s // tm),
        in_specs=[pl.BlockSpec((None, tm, d), lambda i, t: (i, t, 0)),
                  pl.BlockSpec((None, halo, d),
                               lambda i, t: (i, jnp.maximum(t * (tm // halo) - 1, 0), 0)),
                  full(wup), full(wdn), full(wcv), full(bcv), full(lng), full(lnb)],
        out_specs=pl.BlockSpec((None, tm, d), lambda i, t: (i, t, 0)),
        out_shape=jax.ShapeDtypeStruct((b, s, d), jnp.float32),
        scratch_shapes=[pltpu.VMEM((tm + halo, d), jnp.bfloat16),
                        pltpu.VMEM((tm, D_FF), jnp.bfloat16)],
        compiler_params=_cparams(("arbitrary", "arbitrary")),
        name="conv_ffn",
    )(x3, x3, wup, wdn, wcv, bcv, lng, lnb)


def kernel(x, w_in, w_short_conv, w_moba_proj, w_dil_proj, w_conv_proj, w_mix_out,
           ln1_g, ln1_b, w_up, w_ffn_conv, b_ffn_conv, w_down, ln2_g, ln2_b):
    b, s, d = x.shape
    depth = w_in.shape[0]
    alpha = float((2 * depth) ** 0.25)
    bf = jnp.bfloat16
    for l in range(depth):
        x2 = x.reshape(b * s, d)
        p = _in_proj(x2, w_in, l).reshape(b, s, MAIN_COLS)
        pgs = _dil_proj(x2, w_in, l)
        ym = _moba(p)
        dil_outs = [_dil_group(pg, b, dil) for pg, (_, dil) in zip(pgs, DIL_GROUPS)]
        half = lambda w: (0.5 * w).astype(bf)
        x = _merge(alpha, x, ym, p, dil_outs,
                   half(w_moba_proj[l]), half(w_conv_proj[l]), half(w_dil_proj[l]),
                   w_mix_out[l].astype(bf),
                   w_short_conv[l], ln1_g[l].reshape(1, d), ln1_b[l].reshape(1, d))
        gate_half = jnp.where(jnp.arange(2 * D_FF) < D_FF, 0.5, 1.0)
        x = _ffn(alpha, x, w_up[l].astype(bf), w_down[l].astype(bf),
                 w_ffn_conv[l] * gate_half, (b_ffn_conv[l] * gate_half).reshape(1, 2 * D_FF),
                 ln2_g[l].reshape(1, d), ln2_b[l].reshape(1, d))
    return x
```

```python
import functools

import jax
import jax.numpy as jnp
import numpy as np
from jax import lax
from jax.experimental import pallas as pl
from jax.experimental.pallas import tpu as pltpu

D_MODEL = 1024
HEAD_DIM = 64
MOBA_HEADS = 8
MOBA_BLOCK = 256
MOBA_TOPK = 3
MOBA_W = MOBA_HEADS * HEAD_DIM
DIL_GROUPS = ((128, 1), (512, 4), (2048, 16))
DIL_HEADS_PER_GROUP = 4
DIL_W = 3 * DIL_HEADS_PER_GROUP * HEAD_DIM
DIL_OUT_W = DIL_HEADS_PER_GROUP * HEAD_DIM
CONV_WIDTH = 512
N_BRANCH = 3
D_FF = 2816
LN_EPS = 1e-5

OFF_CONV = 3 * MOBA_W
OFF_GATE = OFF_CONV + 3 * CONV_WIDTH
MAIN_COLS = OFF_GATE + N_BRANCH * D_MODEL

LOG2E = float(np.log2(np.e))
LN2 = float(np.log(2.0))
Q_SCALE = HEAD_DIM ** -0.5 * LOG2E

LANES = 128
NEG = -1e30
VMEM_LIMIT = 56 * 1024 * 1024

PROJ_TM = 2048
PROJ_TN = 768
DILP_TM = 1024
MOBA_KTILE = 2 * MOBA_BLOCK
MOBA_UNROLL = 3
MOBA_ONES = 16
DIL_TQ = 2048
DIL_SPAN = 128
MERGE_TM = 512
FFN_TM = 512
FFN_HALO = 16
FFN_CHUNK = 256
FFN_NCHUNK = D_FF // FFN_CHUNK


def _cparams(sem):
    return pltpu.CompilerParams(dimension_semantics=sem, vmem_limit_bytes=VMEM_LIMIT)


def _proj_kernel(x_ref, w_ref, cs_ref, o_ref, xb_ref, wb_ref):
    j = pl.program_id(1)

    @pl.when(j == 0)
    def _():
        xb_ref[...] = x_ref[...].astype(jnp.bfloat16)

    @pl.when(pl.program_id(0) == 0)
    def _():
        wb_ref[j] = w_ref[...].astype(jnp.bfloat16)

    acc = jnp.dot(xb_ref[...], wb_ref[j], preferred_element_type=jnp.float32)
    o_ref[...] = (acc * cs_ref[...]).astype(o_ref.dtype)


def _main_scale_row():
    cs = np.ones((1, MAIN_COLS), np.float32)
    cs[:, :MOBA_W] = Q_SCALE
    cs[:, OFF_GATE:] = 0.5
    return jnp.asarray(cs)


def _in_proj(x2, w_in, layer):
    n, d = x2.shape
    tn = PROJ_TN
    skip_at = 3 * MOBA_W // tn
    skip = 3 * DIL_W // tn
    n_ct = MAIN_COLS // tn
    src = lambda j: jnp.where(j < skip_at, j, j + skip)
    w_tile = lambda i, j: src(jnp.where(i == 0, j, n_ct - 1))
    return pl.pallas_call(
        _proj_kernel,
        grid=(n // PROJ_TM, n_ct),
        in_specs=[pl.BlockSpec((PROJ_TM, d), lambda i, j: (i, 0)),
                  pl.BlockSpec((None, d, tn), lambda i, j: (layer, 0, w_tile(i, j))),
                  pl.BlockSpec((1, tn), lambda i, j: (0, j))],
        out_specs=pl.BlockSpec((PROJ_TM, tn), lambda i, j: (i, j)),
        out_shape=jax.ShapeDtypeStruct((n, MAIN_COLS), jnp.bfloat16),
        scratch_shapes=[pltpu.VMEM((PROJ_TM, d), jnp.bfloat16),
                        pltpu.VMEM((n_ct, d, tn), jnp.bfloat16)],
        compiler_params=_cparams(("arbitrary", "arbitrary")),
        name="in_proj",
    )(x2, w_in, _main_scale_row())


def _dil_proj_kernel(x_ref, wq_ref, wk_ref, wv_ref, o0_ref, o1_ref, o2_ref, wb_ref, res_ref):
    tm = x_ref.shape[0]

    @pl.when(pl.program_id(0) == 0)
    def _():
        for j, w_ref in enumerate((wq_ref, wk_ref, wv_ref)):
            wb_ref[j] = w_ref[...].astype(jnp.bfloat16)

    xb = x_ref[...].astype(jnp.bfloat16)
    per_part = DIL_W // LANES
    for j in range(3):
        res = jnp.dot(xb, wb_ref[j], preferred_element_type=jnp.float32)
        if j == 0:
            res = res * Q_SCALE
        for cb in range(per_part):
            res_ref[j * per_part + cb] = res[:, cb * LANES:(cb + 1) * LANES]
    slabs = DIL_OUT_W // LANES
    for g, o_ref in enumerate((o0_ref, o1_ref, o2_ref)):
        d = DIL_GROUPS[g][1]
        rows = tm // d
        for r in range(d):
            for j in range(3):
                for sl in range(slabs):
                    src_cb = (j * DIL_W + g * DIL_OUT_W) // LANES + sl
                    dst_c = (r * 3 + j) * DIL_OUT_W + sl * LANES
                    if d == 1:
                        src = res_ref[src_cb]
                    else:
                        src = res_ref[src_cb, pl.ds(r, rows, stride=d), :]
                    o_ref[:, dst_c:dst_c + LANES] = src.astype(o_ref.dtype)


def _dil_proj(x2, w_in, layer):
    n, d = x2.shape
    tm = DILP_TM
    dils = [dil for _, dil in DIL_GROUPS]
    first = 3 * MOBA_W // DIL_W
    wspec = lambda j: pl.BlockSpec((None, d, DIL_W), lambda i, j=j: (layer, 0, first + j))
    return pl.pallas_call(
        _dil_proj_kernel,
        grid=(n // tm,),
        in_specs=[pl.BlockSpec((tm, d), lambda i: (i, 0)), wspec(0), wspec(1), wspec(2)],
        out_specs=[pl.BlockSpec((tm // dil, dil * 3 * DIL_OUT_W), lambda i: (i, 0))
                   for dil in dils],
        out_shape=[jax.ShapeDtypeStruct((n // dil, dil * 3 * DIL_OUT_W), jnp.bfloat16)
                   for dil in dils],
        scratch_shapes=[pltpu.VMEM((3, d, DIL_W), jnp.bfloat16),
                        pltpu.VMEM((3 * DIL_W // LANES, tm, LANES), jnp.float32)],
        compiler_params=_cparams(("arbitrary",)),
        name="dil_proj",
    )(x2, w_in, w_in, w_in)


def _moba_kernel(q_ref, k_ref, v_ref, o_ref, kaug_ref, vaug_ref, km_ref, sc_ref, qa_ref):
    s = q_ref.shape[0]
    nb = s // MOBA_BLOCK
    tq = kt = MOBA_KTILE
    f32 = jnp.float32
    bf = jnp.bfloat16
    sub_bf = 16
    lane_r = lax.broadcasted_iota(jnp.int32, (sub_bf, LANES), 1)
    row01 = lambda cond: cond.astype(f32).astype(bf)
    blk01 = lambda cond: jnp.broadcast_to(
        row01(cond)[None], (MOBA_BLOCK // sub_bf, sub_bf, LANES)).reshape(MOBA_BLOCK, LANES)
    head_lanes = (blk01(lane_r < HEAD_DIM), blk01(lane_r >= HEAD_DIM))
    for n in range(nb):
        rows = slice(n * MOBA_BLOCK, (n + 1) * MOBA_BLOCK)
        kblk = k_ref[rows, :]
        kaug_ref[0, rows, :] = kblk * head_lanes[0] + blk01(lane_r == HEAD_DIM + n)
        kaug_ref[1, rows, :] = kblk * head_lanes[1] + blk01(lane_r == n)
    vr = HEAD_DIM + MOBA_ONES
    lane_k = lax.broadcasted_iota(jnp.int32, (kt, LANES), 1)
    for t in range(s // kt):
        vt = v_ref[t * kt:(t + 1) * kt, :].astype(f32)
        vaug_ref[0, t] = jnp.where(lane_k < HEAD_DIM, vt, 1.0).T[:vr].astype(bf)
        vaug_ref[1, t] = jnp.where(lane_k >= HEAD_DIM, vt, 1.0).T[LANES - vr:].astype(bf)

    offs = (HEAD_DIM, 0)
    in_blk = (lax.broadcasted_iota(jnp.int32, (nb, s), 1) // MOBA_BLOCK
              == lax.broadcasted_iota(jnp.int32, (nb, s), 0))
    ksum = jnp.dot(row01(in_blk), k_ref[...], preferred_element_type=f32)
    kmean = ksum * (1.0 / MOBA_BLOCK)
    lane_b = lax.broadcasted_iota(jnp.int32, (nb, LANES), 1)
    km0 = jnp.where(lane_b < HEAD_DIM, kmean, 0.0)
    km1 = jnp.where(lane_b >= HEAD_DIM, kmean, 0.0)
    zgap = jnp.zeros((HEAD_DIM - nb, LANES), f32)
    km_ref[...] = jnp.concatenate([km1, zgap, km0, zgap], axis=0).astype(bf)

    lane_t = lax.broadcasted_iota(jnp.int32, (tq, LANES), 1)
    causal = (lax.broadcasted_iota(jnp.int32, (kt, tq), 0)
              <= lax.broadcasted_iota(jnp.int32, (kt, tq), 1))
    blk_i = lax.broadcasted_iota(jnp.int32, (nb, tq), 0)
    blk_f = blk_i.astype(f32)
    q_sub = lax.broadcasted_iota(jnp.int32, (nb, tq), 1) // MOBA_BLOCK
    zgap_t = jnp.zeros((HEAD_DIM - nb, tq), f32)
    nt = (((1,), (1,)), ((), ()))
    heads = (0, 1)
    q_lanes = (lane_t < HEAD_DIM, lane_t >= HEAD_DIM)

    def augmented_queries(tile):
        q2 = q_ref[pl.ds(pl.multiple_of(tile * tq, tq), tq), :]
        q2f = q2.astype(f32)
        gate_t = lax.dot_general(km_ref[...], q2, nt, preferred_element_type=f32)
        j = tile * (tq // MOBA_BLOCK) + q_sub
        valid = blk_i < j
        pens = []
        for h in heads:
            g = jnp.where(valid, gate_t[offs[h]:offs[h] + nb, :], NEG)
            sel = jnp.zeros((nb, tq), jnp.bool_)
            for _ in range(MOBA_TOPK):
                m = jnp.max(g, axis=0, keepdims=True)
                first = jnp.min(jnp.where(g == m, blk_f, 1e9), axis=0, keepdims=True)
                pick = blk_f == first
                sel = sel | pick
                g = jnp.where(pick, NEG, g)
            keep = (sel & valid) | (blk_i == j)
            pens.append(jnp.where(keep, 0.0, NEG))
        pen = jnp.concatenate([pens[1], zgap_t, pens[0], zgap_t], axis=0).T
        return [jnp.where(q_lanes[h], q2f, pen).astype(bf) for h in heads]

    def prepare(tile):
        r0 = pl.multiple_of(tile * tq, tq)
        for h, qa in enumerate(augmented_queries(tile)):
            sc = lax.dot_general(kaug_ref[h, pl.ds(r0, kt), :], qa, nt,
                                 preferred_element_type=f32)
            sc_ref[h] = jnp.where(causal, sc, NEG)
            qa_ref[h] = qa

    n_tiles = s // tq
    prepare(0)

    def q_tile(t_last, carry):
        r0 = pl.multiple_of(t_last * tq, tq)
        qas = [qa_ref[h] for h in heads]

        def tile_of(visit):
            return jnp.where(visit <= 0, t_last, visit - 1)

        def kv_step(i, mc):
            t_pv = tile_of(i)
            k0 = pl.multiple_of(jnp.clip(i, 0, jnp.maximum(t_last - 1, 0)) * kt, kt)
            out = []
            for h in heads:
                m_old, acc = mc[2 * h], mc[2 * h + 1]
                sc = sc_ref[h]
                sc_next = lax.dot_general(kaug_ref[h, pl.ds(k0, kt), :], qas[h], nt,
                                          preferred_element_type=f32)
                m_new = jnp.maximum(m_old, jnp.max(sc, axis=0, keepdims=True))
                alpha = jnp.exp2(m_old - m_new)
                p = jnp.exp2(sc - m_new).astype(bf)
                pv = jnp.dot(vaug_ref[h, t_pv], p, preferred_element_type=f32)
                sc_ref[h] = sc_next
                out += [m_new, acc * alpha + pv]
            return tuple(out)

        m_init = jnp.full((1, tq), NEG, f32)
        acc_init = jnp.zeros((vr, tq), f32)
        def multi_step(ii, mc):
            for u in range(MOBA_UNROLL):
                mc = kv_step(MOBA_UNROLL * ii + u, mc)
            return mc

        n_steps = t_last + 1
        n_main = n_steps // MOBA_UNROLL
        mc = lax.fori_loop(0, n_main, multi_step, (m_init, acc_init, m_init, acc_init))
        fin = lax.fori_loop(n_main * MOBA_UNROLL, n_steps, kv_step, mc)
        o0 = fin[1][:HEAD_DIM] / fin[1][HEAD_DIM:HEAD_DIM + 1]
        o1 = fin[3][MOBA_ONES:] / fin[3][0:1]
        o_ref[pl.ds(r0, tq), :] = jnp.concatenate([o0, o1], axis=0).T.astype(o_ref.dtype)
        prepare(jnp.minimum(t_last + 1, n_tiles - 1))
        return carry

    lax.fori_loop(0, n_tiles, q_tile, 0)


def _moba(p3):
    b, s, _ = p3.shape
    n_pairs = MOBA_HEADS // 2
    blk = (None, s, LANES)
    return pl.pallas_call(
        _moba_kernel,
        grid=(b, n_pairs),
        in_specs=[pl.BlockSpec(blk, lambda i, j: (i, 0, j)),
                  pl.BlockSpec(blk, lambda i, j: (i, 0, n_pairs + j)),
                  pl.BlockSpec(blk, lambda i, j: (i, 0, 2 * n_pairs + j))],
        out_specs=pl.BlockSpec(blk, lambda i, j: (i, 0, j)),
        out_shape=jax.ShapeDtypeStruct((b, s, MOBA_W), jnp.bfloat16),
        scratch_shapes=[pltpu.VMEM((2, s, LANES), jnp.bfloat16),
                        pltpu.VMEM((2, s // MOBA_KTILE, HEAD_DIM + MOBA_ONES, MOBA_KTILE),
                                   jnp.bfloat16),
                        pltpu.VMEM((LANES, LANES), jnp.bfloat16),
                        pltpu.VMEM((2, MOBA_KTILE, MOBA_KTILE), jnp.float32),
                        pltpu.VMEM((2, MOBA_KTILE, LANES), jnp.bfloat16)],
        compiler_params=_cparams(("arbitrary", "arbitrary")),
        name="moba_attn",
    )(p3, p3, p3)


def _dil_kernel(n_res, *refs):
    q_refs, k_refs, v_refs = refs[:n_res], refs[n_res:2 * n_res], refs[2 * n_res:3 * n_res]
    o_ref, lse_ref = refs[3 * n_res:]
    tq = q_refs[0].shape[0]
    sub = DIL_SPAN
    kw = 2 * sub
    base_tile = pl.program_id(2) * tq
    lane = lax.broadcasted_iota(jnp.int32, (sub, LANES), 1)
    first = lane < HEAD_DIM
    diff = (lax.broadcasted_iota(jnp.int32, (sub, kw), 0)
            - lax.broadcasted_iota(jnp.int32, (sub, kw), 1))
    band = lambda dist: jnp.where((dist >= 0) & (dist <= DIL_SPAN), 0.0, NEG)
    bias_in = band(diff + sub)
    bias_first = jnp.where(base_tile == 0, band(diff), bias_in)
    nt = (((1,), (1,)), ((), ()))

    units = [(ri, sb, pr) for ri in range(n_res) for sb in range(tq // sub)
             for pr in range(DIL_HEADS_PER_GROUP // 2)]
    for ri, sb, pr in units:
        q0 = sb * sub
        kstart = pl.multiple_of(jnp.maximum(base_tile + q0 - sub, 0), sub)
        bias = bias_first if sb == 0 else bias_in
        cs = slice(pr * LANES, (pr + 1) * LANES)
        oc = slice(ri * DIL_OUT_W + pr * LANES, ri * DIL_OUT_W + (pr + 1) * LANES)
        q2 = q_refs[ri][pl.ds(q0, sub), cs].astype(jnp.float32)
        k2 = k_refs[ri][pl.ds(kstart, kw), cs]
        v2 = v_refs[ri][pl.ds(kstart, kw), cs]
        outs, lses = [], []
        for h in range(2):
            qh = jnp.where(first if h == 0 else jnp.logical_not(first), q2, 0.0)
            qh = qh.astype(jnp.bfloat16)
            sc = lax.dot_general(qh, k2, nt, preferred_element_type=jnp.float32) + bias
            m = jnp.max(sc, axis=1, keepdims=True)
            p = jnp.exp2(sc - m)
            l = jnp.sum(p, axis=1, keepdims=True)
            o = jnp.dot(p.astype(jnp.bfloat16), v2, preferred_element_type=jnp.float32)
            outs.append(o / l)
            lses.append(jnp.broadcast_to(m * LN2 + jnp.log(l), (sub, LANES)))
        o_ref[pl.ds(q0, sub), oc] = jnp.where(first, outs[0], outs[1]).astype(o_ref.dtype)
        lse_ref[pl.ds(q0, sub), oc] = jnp.where(first, lses[0], lses[1])


def _dil_group(pg, b, dil):
    w = DIL_OUT_W
    l = pg.shape[0] // b
    pv = pg.reshape(b, l, dil * 3 * w)
    tq = min(DIL_TQ, l)
    n_res = min(dil, DIL_TQ // tq)
    qkv = lambda part, width, rows: [
        pl.BlockSpec((None, rows, width),
                     lambda i, r, t, ri=ri: (i, t if part == 0 else 0, 3 * (r * n_res + ri) + part))
        for ri in range(n_res)]
    out = pl.BlockSpec((None, tq, n_res * w), lambda i, r, t: (i, t, r))
    o, lse = pl.pallas_call(
        functools.partial(_dil_kernel, n_res),
        grid=(b, dil // n_res, l // tq),
        in_specs=qkv(0, w, tq) + qkv(1, w, l) + qkv(2, w, l),
        out_specs=[out, out],
        out_shape=[jax.ShapeDtypeStruct((b, l, dil * w), jnp.bfloat16),
                   jax.ShapeDtypeStruct((b, l, dil * w), jnp.float32)],
        compiler_params=_cparams(("arbitrary", "arbitrary", "arbitrary")),
        name=f"dil_attn_d{dil}",
    )(*([pv] * (3 * n_res)))
    return o, lse


def _layer_norm(y, g, b):
    mu = jnp.mean(y, axis=-1, keepdims=True)
    yc = y - mu
    var = jnp.mean(yc * yc, axis=-1, keepdims=True)
    return yc * lax.rsqrt(var + LN_EPS) * g + b


def _twice_sigmoid_of_double(h):
    return 1.0 + jnp.tanh(h)


def _natural_order(src_ref, dst_ref, a, dil):
    rows = src_ref.shape[0]
    slabs = DIL_OUT_W // LANES
    for r in range(dil):
        for sl in range(slabs):
            c = r * DIL_OUT_W + sl * LANES
            dst_ref[a, sl, pl.ds(r, rows, stride=dil), :] = (
                src_ref[:, c:c + LANES].astype(jnp.float32))
    return jnp.concatenate([dst_ref[a, sl] for sl in range(slabs)], axis=1)


def _merge_kernel(alpha, x_ref, ym_ref, bg_ref, cg_ref, hc_ref, cgp_ref, hcp_ref,
                  g0_ref, g1_ref, g2_ref, o1_ref, o2_ref, o3_ref, l1_ref, l2_ref, l3_ref,
                  wm_ref, wc_ref, wd_ref, wo_ref, wsc_ref, lng_ref, lnb_ref, out_ref, nat_ref):
    f32 = jnp.float32
    tm = x_ref.shape[0]
    ch = cg_ref[...].astype(f32) * hc_ref[...].astype(f32)
    chp = cgp_ref[...].astype(f32) * hcp_ref[...].astype(f32)
    chp = jnp.where(pl.program_id(1) == 0, 0.0, chp)
    hp = chp.shape[0]
    prev1 = chp[hp - 1:hp, :]
    prev2 = chp[hp - 2:hp - 1, :]
    row = lax.broadcasted_iota(jnp.int32, (tm, CONV_WIDTH), 0)
    ch1 = jnp.where(row == 0, prev1, pltpu.roll(ch, 1, axis=0))
    ch2 = pltpu.roll(ch, 2, axis=0)
    ch2 = jnp.where(row == 0, prev2, jnp.where(row == 1, prev1, ch2))
    wsc = wsc_ref[...]
    conv = ch * wsc[0:1, :] + ch1 * wsc[1:2, :] + ch2 * wsc[2:3, :]
    y_conv = (bg_ref[...].astype(f32) * conv).astype(jnp.bfloat16)

    d2, d3 = DIL_GROUPS[1][1], DIL_GROUPS[2][1]
    o1, l1 = o1_ref[...].astype(f32), l1_ref[...]
    o2, l2 = _natural_order(o2_ref, nat_ref, 0, d2), _natural_order(l2_ref, nat_ref, 1, d2)
    o3, l3 = _natural_order(o3_ref, nat_ref, 2, d3), _natural_order(l3_ref, nat_ref, 3, d3)
    lm = jnp.maximum(jnp.maximum(l1, l2), l3)
    e1, e2, e3 = jnp.exp(l1 - lm), jnp.exp(l2 - lm), jnp.exp(l3 - lm)
    y_dil = ((e1 * o1 + e2 * o2 + e3 * o3) / (e1 + e2 + e3)).astype(jnp.bfloat16)

    pm = jnp.dot(ym_ref[...], wm_ref[...], preferred_element_type=f32)
    pc = jnp.dot(y_conv, wc_ref[...], preferred_element_type=f32)
    pd = jnp.dot(y_dil, wd_ref[...], preferred_element_type=f32)
    merged = (_twice_sigmoid_of_double(g0_ref[...].astype(f32)) * pm
              + _twice_sigmoid_of_double(g1_ref[...].astype(f32)) * pc
              + _twice_sigmoid_of_double(g2_ref[...].astype(f32)) * pd)
    mix = jnp.dot(merged.astype(jnp.bfloat16), wo_ref[...], preferred_element_type=f32)
    y = alpha * x_ref[...] + mix
    out_ref[...] = _layer_norm(y, lng_ref[...], lnb_ref[...])


def _merge(alpha, x3, ym, p3, dil_outs, wm, wc, wd, wo, wsc, lng, lnb):
    b, s, d = x3.shape
    tm = MERGE_TM
    hp = 16
    cw = CONV_WIDTH
    cblk = OFF_CONV // cw
    gblk = OFF_GATE // d
    row = lambda w: pl.BlockSpec((None, tm, w), lambda i, t: (i, t, 0))
    pcol = lambda w, c: pl.BlockSpec((None, tm, w), lambda i, t, c=c: (i, t, c))
    prev = lambda c: pl.BlockSpec(
        (None, hp, cw), lambda i, t, c=c: (i, jnp.maximum(t * (tm // hp) - 1, 0), c))
    full = lambda a: pl.BlockSpec(a.shape, lambda i, t: (0,) * a.ndim)
    (o1, l1), (o2, l2), (o3, l3) = dil_outs
    dils = [dil for _, dil in DIL_GROUPS]
    grp = lambda dil: pl.BlockSpec((None, tm // dil, dil * DIL_OUT_W), lambda i, t: (i, t, 0))
    in_specs = [row(d), row(MOBA_W),
                pcol(cw, cblk), pcol(cw, cblk + 1), pcol(cw, cblk + 2),
                prev(cblk + 1), prev(cblk + 2),
                pcol(d, gblk), pcol(d, gblk + 1), pcol(d, gblk + 2),
                grp(dils[0]), grp(dils[1]), grp(dils[2]),
                grp(dils[0]), grp(dils[1]), grp(dils[2]),
                full(wm), full(wc), full(wd), full(wo), full(wsc), full(lng), full(lnb)]
    return pl.pallas_call(
        functools.partial(_merge_kernel, alpha),
        grid=(b, s // tm),
        in_specs=in_specs,
        out_specs=row(d),
        out_shape=jax.ShapeDtypeStruct((b, s, d), jnp.float32),
        scratch_shapes=[pltpu.VMEM((4, DIL_OUT_W // LANES, tm, LANES), jnp.float32)],
        compiler_params=_cparams(("arbitrary", "arbitrary")),
        name="merge",
    )(x3, ym, p3, p3, p3, p3, p3, p3, p3, p3, o1, o2, o3, l1, l2, l3,
      wm, wc, wd, wo, wsc, lng, lnb)


def _ffn_kernel(alpha, x_ref, xp_ref, wup_ref, wdn_ref, wcv_ref, bcv_ref, lng_ref, lnb_ref,
                out_ref, xs_ref, act_ref):
    f32 = jnp.float32
    tm = x_ref.shape[0]
    halo = xp_ref.shape[0]
    xp = jnp.where(pl.program_id(1) == 0, 0.0, xp_ref[...])
    xs_ref[0:halo, :] = xp.astype(jnp.bfloat16)
    xs_ref[halo:halo + tm, :] = x_ref[...].astype(jnp.bfloat16)

    def conv(u, w):
        y = u * w[0:1, :] + pltpu.roll(u, 1, axis=0) * w[1:2, :] + pltpu.roll(u, 2, axis=0) * w[2:3, :]
        return y[halo:, :]

    for c in range(FFN_NCHUNK):
        gc = slice(c * FFN_CHUNK, (c + 1) * FFN_CHUNK)
        vc = slice(D_FF + c * FFN_CHUNK, D_FF + (c + 1) * FFN_CHUNK)
        xs = xs_ref[...]
        ug = jnp.dot(xs, wup_ref[:, gc], preferred_element_type=f32)
        uv = jnp.dot(xs, wup_ref[:, vc], preferred_element_type=f32)
        hg = conv(ug, wcv_ref[:, gc]) + bcv_ref[:, gc]
        yv = conv(uv, wcv_ref[:, vc]) + bcv_ref[:, vc]
        act_ref[:, gc] = (hg * _twice_sigmoid_of_double(hg) * yv).astype(jnp.bfloat16)

    ffn = jnp.dot(act_ref[...], wdn_ref[...], preferred_element_type=f32)
    y = alpha * x_ref[...] + ffn
    out_ref[...] = _layer_norm(y, lng_ref[...], lnb_ref[...])


def _ffn(alpha, x3, wup, wdn, wcv, bcv, lng, lnb):
    b, s, d = x3.shape
    tm = FFN_TM
    halo = FFN_HALO
    full = lambda a: pl.BlockSpec(a.shape, lambda i, t: (0,) * a.ndim)
    return pl.pallas_call(
        functools.partial(_ffn_kernel, alpha),
        grid=(b, s // tm),
        in_specs=[pl.BlockSpec((None, tm, d), lambda i, t: (i, t, 0)),
                  pl.BlockSpec((None, halo, d),
                               lambda i, t: (i, jnp.maximum(t * (tm // halo) - 1, 0), 0)),
                  full(wup), full(wdn), full(wcv), full(bcv), full(lng), full(lnb)],
        out_specs=pl.BlockSpec((None, tm, d), lambda i, t: (i, t, 0)),
        out_shape=jax.ShapeDtypeStruct((b, s, d), jnp.float32),
        scratch_shapes=[pltpu.VMEM((tm + halo, d), jnp.bfloat16),
                        pltpu.VMEM((tm, D_FF), jnp.bfloat16)],
        compiler_params=_cparams(("arbitrary", "arbitrary")),
        name="conv_ffn",
    )(x3, x3, wup, wdn, wcv, bcv, lng, lnb)


def kernel(x, w_in, w_short_conv, w_moba_proj, w_dil_proj, w_conv_proj, w_mix_out,
           ln1_g, ln1_b, w_up, w_ffn_conv, b_ffn_conv, w_down, ln2_g, ln2_b):
    b, s, d = x.shape
    depth = w_in.shape[0]
    alpha = float((2 * depth) ** 0.25)
    bf = jnp.bfloat16
    for l in range(depth):
        x2 = x.reshape(b * s, d)
        p = _in_proj(x2, w_in, l).reshape(b, s, MAIN_COLS)
        pgs = _dil_proj(x2, w_in, l)
        ym = _moba(p)
        dil_outs = [_dil_group(pg, b, dil) for pg, (_, dil) in zip(pgs, DIL_GROUPS)]
        half = lambda w: (0.5 * w).astype(bf)
        x = _merge(alpha, x, ym, p, dil_outs,
                   half(w_moba_proj[l]), half(w_conv_proj[l]), half(w_dil_proj[l]),
                   w_mix_out[l].astype(bf),
                   w_short_conv[l], ln1_g[l].reshape(1, d), ln1_b[l].reshape(1, d))
        gate_half = jnp.where(jnp.arange(2 * D_FF) < D_FF, 0.5, 1.0)
        x = _ffn(alpha, x, w_up[l].astype(bf), w_down[l].astype(bf),
                 w_ffn_conv[l] * gate_half, (b_ffn_conv[l] * gate_half).reshape(1, 2 * D_FF),
                 ln2_g[l].reshape(1, d), ln2_b[l].reshape(1, d))
    return x
```

```python
import functools

import jax
import jax.numpy as jnp
import numpy as np
from jax import lax
from jax.experimental import pallas as pl
from jax.experimental.pallas import tpu as pltpu

D_MODEL = 1024
HEAD_DIM = 64
MOBA_HEADS = 8
MOBA_BLOCK = 256
MOBA_TOPK = 3
MOBA_W = MOBA_HEADS * HEAD_DIM
DIL_GROUPS = ((128, 1), (512, 4), (2048, 16))
DIL_HEADS_PER_GROUP = 4
DIL_W = 3 * DIL_HEADS_PER_GROUP * HEAD_DIM
DIL_OUT_W = DIL_HEADS_PER_GROUP * HEAD_DIM
CONV_WIDTH = 512
N_BRANCH = 3
D_FF = 2816
LN_EPS = 1e-5

OFF_CONV = 3 * MOBA_W
OFF_GATE = OFF_CONV + 3 * CONV_WIDTH
MAIN_COLS = OFF_GATE + N_BRANCH * D_MODEL

LOG2E = float(np.log2(np.e))
LN2 = float(np.log(2.0))
Q_SCALE = HEAD_DIM ** -0.5 * LOG2E

LANES = 128
NEG = -1e30
VMEM_LIMIT = 56 * 1024 * 1024

PROJ_TM = 2048
PROJ_TN = 768
DILP_TM = 1024
MOBA_KTILE = 2 * MOBA_BLOCK
MOBA_UNROLL = 4
MOBA_ONES = 16
DIL_TQ = 2048
DIL_SPAN = 128
MERGE_TM = 512
FFN_TM = 512
FFN_HALO = 16
FFN_CHUNK = 256
FFN_NCHUNK = D_FF // FFN_CHUNK


def _cparams(sem):
    return pltpu.CompilerParams(dimension_semantics=sem, vmem_limit_bytes=VMEM_LIMIT)


def _proj_kernel(x_ref, w_ref, cs_ref, o_ref, xb_ref, wb_ref):
    j = pl.program_id(1)

    @pl.when(j == 0)
    def _():
        xb_ref[...] = x_ref[...].astype(jnp.bfloat16)

    @pl.when(pl.program_id(0) == 0)
    def _():
        wb_ref[j] = w_ref[...].astype(jnp.bfloat16)

    acc = jnp.dot(xb_ref[...], wb_ref[j], preferred_element_type=jnp.float32)
    o_ref[...] = (acc * cs_ref[...]).astype(o_ref.dtype)


def _main_scale_row():
    cs = np.ones((1, MAIN_COLS), np.float32)
    cs[:, :MOBA_W] = Q_SCALE
    cs[:, OFF_GATE:] = 0.5
    return jnp.asarray(cs)


def _in_proj(x2, w_in, layer):
    n, d = x2.shape
    tn = PROJ_TN
    skip_at = 3 * MOBA_W // tn
    skip = 3 * DIL_W // tn
    n_ct = MAIN_COLS // tn
    src = lambda j: jnp.where(j < skip_at, j, j + skip)
    w_tile = lambda i, j: src(jnp.where(i == 0, j, n_ct - 1))
    return pl.pallas_call(
        _proj_kernel,
        grid=(n // PROJ_TM, n_ct),
        in_specs=[pl.BlockSpec((PROJ_TM, d), lambda i, j: (i, 0)),
                  pl.BlockSpec((None, d, tn), lambda i, j: (layer, 0, w_tile(i, j))),
                  pl.BlockSpec((1, tn), lambda i, j: (0, j))],
        out_specs=pl.BlockSpec((PROJ_TM, tn), lambda i, j: (i, j)),
        out_shape=jax.ShapeDtypeStruct((n, MAIN_COLS), jnp.bfloat16),
        scratch_shapes=[pltpu.VMEM((PROJ_TM, d), jnp.bfloat16),
                        pltpu.VMEM((n_ct, d, tn), jnp.bfloat16)],
        compiler_params=_cparams(("arbitrary", "arbitrary")),
        name="in_proj",
    )(x2, w_in, _main_scale_row())


def _dil_proj_kernel(x_ref, wq_ref, wk_ref, wv_ref, o0_ref, o1_ref, o2_ref, wb_ref, res_ref):
    tm = x_ref.shape[0]

    @pl.when(pl.program_id(0) == 0)
    def _():
        for j, w_ref in enumerate((wq_ref, wk_ref, wv_ref)):
            wb_ref[j] = w_ref[...].astype(jnp.bfloat16)

    xb = x_ref[...].astype(jnp.bfloat16)
    per_part = DIL_W // LANES
    for j in range(3):
        res = jnp.dot(xb, wb_ref[j], preferred_element_type=jnp.float32)
        if j == 0:
            res = res * Q_SCALE
        for cb in range(per_part):
            res_ref[j * per_part + cb] = res[:, cb * LANES:(cb + 1) * LANES]
    slabs = DIL_OUT_W // LANES
    for g, o_ref in enumerate((o0_ref, o1_ref, o2_ref)):
        d = DIL_GROUPS[g][1]
        rows = tm // d
        for r in range(d):
            for j in range(3):
                for sl in range(slabs):
                    src_cb = (j * DIL_W + g * DIL_OUT_W) // LANES + sl
                    dst_c = (r * 3 + j) * DIL_OUT_W + sl * LANES
                    if d == 1:
                        src = res_ref[src_cb]
                    else:
                        src = res_ref[src_cb, pl.ds(r, rows, stride=d), :]
                    o_ref[:, dst_c:dst_c + LANES] = src.astype(o_ref.dtype)


def _dil_proj(x2, w_in, layer):
    n, d = x2.shape
    tm = DILP_TM
    dils = [dil for _, dil in DIL_GROUPS]
    first = 3 * MOBA_W // DIL_W
    wspec = lambda j: pl.BlockSpec((None, d, DIL_W), lambda i, j=j: (layer, 0, first + j))
    return pl.pallas_call(
        _dil_proj_kernel,
        grid=(n // tm,),
        in_specs=[pl.BlockSpec((tm, d), lambda i: (i, 0)), wspec(0), wspec(1), wspec(2)],
        out_specs=[pl.BlockSpec((tm // dil, dil * 3 * DIL_OUT_W), lambda i: (i, 0))
                   for dil in dils],
        out_shape=[jax.ShapeDtypeStruct((n // dil, dil * 3 * DIL_OUT_W), jnp.bfloat16)
                   for dil in dils],
        scratch_shapes=[pltpu.VMEM((3, d, DIL_W), jnp.bfloat16),
                        pltpu.VMEM((3 * DIL_W // LANES, tm, LANES), jnp.float32)],
        compiler_params=_cparams(("arbitrary",)),
        name="dil_proj",
    )(x2, w_in, w_in, w_in)


def _moba_kernel(q_ref, k_ref, v_ref, o_ref, kaug_ref, vaug_ref, km_ref, sc_ref, qa_ref):
    s = q_ref.shape[0]
    nb = s // MOBA_BLOCK
    tq = kt = MOBA_KTILE
    f32 = jnp.float32
    bf = jnp.bfloat16
    sub_bf = 16
    lane_r = lax.broadcasted_iota(jnp.int32, (sub_bf, LANES), 1)
    row01 = lambda cond: cond.astype(f32).astype(bf)
    blk01 = lambda cond: jnp.broadcast_to(
        row01(cond)[None], (MOBA_BLOCK // sub_bf, sub_bf, LANES)).reshape(MOBA_BLOCK, LANES)
    head_lanes = (blk01(lane_r < HEAD_DIM), blk01(lane_r >= HEAD_DIM))
    for n in range(nb):
        rows = slice(n * MOBA_BLOCK, (n + 1) * MOBA_BLOCK)
        kblk = k_ref[rows, :]
        kaug_ref[0, rows, :] = kblk * head_lanes[0] + blk01(lane_r == HEAD_DIM + n)
        kaug_ref[1, rows, :] = kblk * head_lanes[1] + blk01(lane_r == n)
    vr = HEAD_DIM + MOBA_ONES
    lane_k = lax.broadcasted_iota(jnp.int32, (kt, LANES), 1)
    for t in range(s // kt):
        vt = v_ref[t * kt:(t + 1) * kt, :].astype(f32)
        vaug_ref[0, t] = jnp.where(lane_k < HEAD_DIM, vt, 1.0).T[:vr].astype(bf)
        vaug_ref[1, t] = jnp.where(lane_k >= HEAD_DIM, vt, 1.0).T[LANES - vr:].astype(bf)

    offs = (HEAD_DIM, 0)
    in_blk = (lax.broadcasted_iota(jnp.int32, (nb, s), 1) // MOBA_BLOCK
              == lax.broadcasted_iota(jnp.int32, (nb, s), 0))
    ksum = jnp.dot(row01(in_blk), k_ref[...], preferred_element_type=f32)
    kmean = ksum * (1.0 / MOBA_BLOCK)
    lane_b = lax.broadcasted_iota(jnp.int32, (nb, LANES), 1)
    km0 = jnp.where(lane_b < HEAD_DIM, kmean, 0.0)
    km1 = jnp.where(lane_b >= HEAD_DIM, kmean, 0.0)
    zgap = jnp.zeros((HEAD_DIM - nb, LANES), f32)
    km_ref[...] = jnp.concatenate([km1, zgap, km0, zgap], axis=0).astype(bf)

    lane_t = lax.broadcasted_iota(jnp.int32, (tq, LANES), 1)
    causal = (lax.broadcasted_iota(jnp.int32, (kt, tq), 0)
              <= lax.broadcasted_iota(jnp.int32, (kt, tq), 1))
    blk_i = lax.broadcasted_iota(jnp.int32, (nb, tq), 0)
    blk_f = blk_i.astype(f32)
    q_sub = lax.broadcasted_iota(jnp.int32, (nb, tq), 1) // MOBA_BLOCK
    zgap_t = jnp.zeros((HEAD_DIM - nb, tq), f32)
    nt = (((1,), (1,)), ((), ()))
    heads = (0, 1)
    q_lanes = (lane_t < HEAD_DIM, lane_t >= HEAD_DIM)

    def augmented_queries(tile):
        q2 = q_ref[pl.ds(pl.multiple_of(tile * tq, tq), tq), :]
        q2f = q2.astype(f32)
        gate_t = lax.dot_general(km_ref[...], q2, nt, preferred_element_type=f32)
        j = tile * (tq // MOBA_BLOCK) + q_sub
        valid = blk_i < j
        pens = []
        for h in heads:
            g = jnp.where(valid, gate_t[offs[h]:offs[h] + nb, :], NEG)
            sel = jnp.zeros((nb, tq), jnp.bool_)
            for _ in range(MOBA_TOPK):
                m = jnp.max(g, axis=0, keepdims=True)
                first = jnp.min(jnp.where(g == m, blk_f, 1e9), axis=0, keepdims=True)
                pick = blk_f == first
                sel = sel | pick
                g = jnp.where(pick, NEG, g)
            keep = (sel & valid) | (blk_i == j)
            pens.append(jnp.where(keep, 0.0, NEG))
        pen = jnp.concatenate([pens[1], zgap_t, pens[0], zgap_t], axis=0).T
        return [jnp.where(q_lanes[h], q2f, pen).astype(bf) for h in heads]

    def prepare(tile):
        r0 = pl.multiple_of(tile * tq, tq)
        for h, qa in enumerate(augmented_queries(tile)):
            sc = lax.dot_general(kaug_ref[h, pl.ds(r0, kt), :], qa, nt,
                                 preferred_element_type=f32)
            sc_ref[h] = jnp.where(causal, sc, NEG)
            qa_ref[h] = qa

    n_tiles = s // tq
    prepare(0)

    def q_tile(t_last, carry):
        r0 = pl.multiple_of(t_last * tq, tq)
        qas = [qa_ref[h] for h in heads]

        def tile_of(visit):
            return jnp.where(visit <= 0, t_last, visit - 1)

        def kv_step(i, mc):
            t_pv = tile_of(i)
            k0 = pl.multiple_of(jnp.clip(i, 0, jnp.maximum(t_last - 1, 0)) * kt, kt)
            out = []
            for h in heads:
                m_old, acc = mc[2 * h], mc[2 * h + 1]
                sc = sc_ref[h]
                sc_next = lax.dot_general(kaug_ref[h, pl.ds(k0, kt), :], qas[h], nt,
                                          preferred_element_type=f32)
                m_new = jnp.maximum(m_old, jnp.max(sc, axis=0, keepdims=True))
                alpha = jnp.exp2(m_old - m_new)
                p = jnp.exp2(sc - m_new).astype(bf)
                pv = jnp.dot(vaug_ref[h, t_pv], p, preferred_element_type=f32)
                sc_ref[h] = sc_next
                out += [m_new, acc * alpha + pv]
            return tuple(out)

        m_init = jnp.full((1, tq), NEG, f32)
        acc_init = jnp.zeros((vr, tq), f32)
        def multi_step(ii, mc):
            for u in range(MOBA_UNROLL):
                mc = kv_step(MOBA_UNROLL * ii + u, mc)
            return mc

        n_steps = t_last + 1
        n_main = n_steps // MOBA_UNROLL
        mc = lax.fori_loop(0, n_main, multi_step, (m_init, acc_init, m_init, acc_init))
        fin = lax.fori_loop(n_main * MOBA_UNROLL, n_steps, kv_step, mc)
        o0 = fin[1][:HEAD_DIM] / fin[1][HEAD_DIM:HEAD_DIM + 1]
        o1 = fin[3][MOBA_ONES:] / fin[3][0:1]
        o_ref[pl.ds(r0, tq), :] = jnp.concatenate([o0, o1], axis=0).T.astype(o_ref.dtype)
        prepare(jnp.minimum(t_last + 1, n_tiles - 1))
        return carry

    lax.fori_loop(0, n_tiles, q_tile, 0)


def _moba(p3):
    b, s, _ = p3.shape
    n_pairs = MOBA_HEADS // 2
    blk = (None, s, LANES)
    return pl.pallas_call(
        _moba_kernel,
        grid=(b, n_pairs),
        in_specs=[pl.BlockSpec(blk, lambda i, j: (i, 0, j)),
                  pl.BlockSpec(blk, lambda i, j: (i, 0, n_pairs + j)),
                  pl.BlockSpec(blk, lambda i, j: (i, 0, 2 * n_pairs + j))],
        out_specs=pl.BlockSpec(blk, lambda i, j: (i, 0, j)),
        out_shape=jax.ShapeDtypeStruct((b, s, MOBA_W), jnp.bfloat16),
        scratch_shapes=[pltpu.VMEM((2, s, LANES), jnp.bfloat16),
                        pltpu.VMEM((2, s // MOBA_KTILE, HEAD_DIM + MOBA_ONES, MOBA_KTILE),
                                   jnp.bfloat16),
                        pltpu.VMEM((LANES, LANES), jnp.bfloat16),
                        pltpu.VMEM((2, MOBA_KTILE, MOBA_KTILE), jnp.float32),
                        pltpu.VMEM((2, MOBA_KTILE, LANES), jnp.bfloat16)],
        compiler_params=_cparams(("arbitrary", "arbitrary")),
        name="moba_attn",
    )(p3, p3, p3)


def _dil_kernel(n_res, *refs):
    q_refs, k_refs, v_refs = refs[:n_res], refs[n_res:2 * n_res], refs[2 * n_res:3 * n_res]
    o_ref, lse_ref = refs[3 * n_res:]
    tq = q_refs[0].shape[0]
    sub = DIL_SPAN
    kw = 2 * sub
    base_tile = pl.program_id(2) * tq
    lane = lax.broadcasted_iota(jnp.int32, (sub, LANES), 1)
    first = lane < HEAD_DIM
    diff = (lax.broadcasted_iota(jnp.int32, (sub, kw), 0)
            - lax.broadcasted_iota(jnp.int32, (sub, kw), 1))
    band = lambda dist: jnp.where((dist >= 0) & (dist <= DIL_SPAN), 0.0, NEG)
    bias_in = band(diff + sub)
    bias_first = jnp.where(base_tile == 0, band(diff), bias_in)
    nt = (((1,), (1,)), ((), ()))

    units = [(ri, sb, pr) for ri in range(n_res) for sb in range(tq // sub)
             for pr in range(DIL_HEADS_PER_GROUP // 2)]
    for ri, sb, pr in units:
        q0 = sb * sub
        kstart = pl.multiple_of(jnp.maximum(base_tile + q0 - sub, 0), sub)
        bias = bias_first if sb == 0 else bias_in
        cs = slice(pr * LANES, (pr + 1) * LANES)
        oc = slice(ri * DIL_OUT_W + pr * LANES, ri * DIL_OUT_W + (pr + 1) * LANES)
        q2 = q_refs[ri][pl.ds(q0, sub), cs].astype(jnp.float32)
        k2 = k_refs[ri][pl.ds(kstart, kw), cs]
        v2 = v_refs[ri][pl.ds(kstart, kw), cs]
        outs, lses = [], []
        for h in range(2):
            qh = jnp.where(first if h == 0 else jnp.logical_not(first), q2, 0.0)
            qh = qh.astype(jnp.bfloat16)
            sc = lax.dot_general(qh, k2, nt, preferred_element_type=jnp.float32) + bias
            m = jnp.max(sc, axis=1, keepdims=True)
            p = jnp.exp2(sc - m)
            l = jnp.sum(p, axis=1, keepdims=True)
            o = jnp.dot(p.astype(jnp.bfloat16), v2, preferred_element_type=jnp.float32)
            outs.append(o / l)
            lses.append(jnp.broadcast_to(m * LN2 + jnp.log(l), (sub, LANES)))
        o_ref[pl.ds(q0, sub), oc] = jnp.where(first, outs[0], outs[1]).astype(o_ref.dtype)
        lse_ref[pl.ds(q0, sub), oc] = jnp.where(first, lses[0], lses[1])


def _dil_group(pg, b, dil):
    w = DIL_OUT_W
    l = pg.shape[0] // b
    pv = pg.reshape(b, l, dil * 3 * w)
    tq = min(DIL_TQ, l)
    n_res = min(dil, DIL_TQ // tq)
    qkv = lambda part, width, rows: [
        pl.BlockSpec((None, rows, width),
                     lambda i, r, t, ri=ri: (i, t if part == 0 else 0, 3 * (r * n_res + ri) + part))
        for ri in range(n_res)]
    out = pl.BlockSpec((None, tq, n_res * w), lambda i, r, t: (i, t, r))
    o, lse = pl.pallas_call(
        functools.partial(_dil_kernel, n_res),
        grid=(b, dil // n_res, l // tq),
        in_specs=qkv(0, w, tq) + qkv(1, w, l) + qkv(2, w, l),
        out_specs=[out, out],
        out_shape=[jax.ShapeDtypeStruct((b, l, dil * w), jnp.bfloat16),
                   jax.ShapeDtypeStruct((b, l, dil * w), jnp.float32)],
        compiler_params=_cparams(("arbitrary", "arbitrary", "arbitrary")),
        name=f"dil_attn_d{dil}",
    )(*([pv] * (3 * n_res)))
    return o, lse


def _layer_norm(y, g, b):
    mu = jnp.mean(y, axis=-1, keepdims=True)
    yc = y - mu
    var = jnp.mean(yc * yc, axis=-1, keepdims=True)
    return yc * lax.rsqrt(var + LN_EPS) * g + b


def _twice_sigmoid_of_double(h):
    return 1.0 + jnp.tanh(h)


def _natural_order(src_ref, dst_ref, a, dil):
    rows = src_ref.shape[0]
    slabs = DIL_OUT_W // LANES
    for r in range(dil):
        for sl in range(slabs):
            c = r * DIL_OUT_W + sl * LANES
            dst_ref[a, sl, pl.ds(r, rows, stride=dil), :] = (
                src_ref[:, c:c + LANES].astype(jnp.float32))
    return jnp.concatenate([dst_ref[a, sl] for sl in range(slabs)], axis=1)


def _merge_kernel(alpha, x_ref, ym_ref, bg_ref, cg_ref, hc_ref, cgp_ref, hcp_ref,
                  g0_ref, g1_ref, g2_ref, o1_ref, o2_ref, o3_ref, l1_ref, l2_ref, l3_ref,
                  wm_ref, wc_ref, wd_ref, wo_ref, wsc_ref, lng_ref, lnb_ref, out_ref, nat_ref):
    f32 = jnp.float32
    tm = x_ref.shape[0]
    ch = cg_ref[...].astype(f32) * hc_ref[...].astype(f32)
    chp = cgp_ref[...].astype(f32) * hcp_ref[...].astype(f32)
    chp = jnp.where(pl.program_id(1) == 0, 0.0, chp)
    hp = chp.shape[0]
    prev1 = chp[hp - 1:hp, :]
    prev2 = chp[hp - 2:hp - 1, :]
    row = lax.broadcasted_iota(jnp.int32, (tm, CONV_WIDTH), 0)
    ch1 = jnp.where(row == 0, prev1, pltpu.roll(ch, 1, axis=0))
    ch2 = pltpu.roll(ch, 2, axis=0)
    ch2 = jnp.where(row == 0, prev2, jnp.where(row == 1, prev1, ch2))
    wsc = wsc_ref[...]
    conv = ch * wsc[0:1, :] + ch1 * wsc[1:2, :] + ch2 * wsc[2:3, :]
    y_conv = (bg_ref[...].astype(f32) * conv).astype(jnp.bfloat16)

    d2, d3 = DIL_GROUPS[1][1], DIL_GROUPS[2][1]
    o1, l1 = o1_ref[...].astype(f32), l1_ref[...]
    o2, l2 = _natural_order(o2_ref, nat_ref, 0, d2), _natural_order(l2_ref, nat_ref, 1, d2)
    o3, l3 = _natural_order(o3_ref, nat_ref, 2, d3), _natural_order(l3_ref, nat_ref, 3, d3)
    lm = jnp.maximum(jnp.maximum(l1, l2), l3)
    e1, e2, e3 = jnp.exp(l1 - lm), jnp.exp(l2 - lm), jnp.exp(l3 - lm)
    y_dil = ((e1 * o1 + e2 * o2 + e3 * o3) / (e1 + e2 + e3)).astype(jnp.bfloat16)

    pm = jnp.dot(ym_ref[...], wm_ref[...], preferred_element_type=f32)
    pc = jnp.dot(y_conv, wc_ref[...], preferred_element_type=f32)
    pd = jnp.dot(y_dil, wd_ref[...], preferred_element_type=f32)
    merged = (_twice_sigmoid_of_double(g0_ref[...].astype(f32)) * pm
              + _twice_sigmoid_of_double(g1_ref[...].astype(f32)) * pc
              + _twice_sigmoid_of_double(g2_ref[...].astype(f32)) * pd)
    mix = jnp.dot(merged.astype(jnp.bfloat16), wo_ref[...], preferred_element_type=f32)
    y = alpha * x_ref[...] + mix
    out_ref[...] = _layer_norm(y, lng_ref[...], lnb_ref[...])


def _merge(alpha, x3, ym, p3, dil_outs, wm, wc, wd, wo, wsc, lng, lnb):
    b, s, d = x3.shape
    tm = MERGE_TM
    hp = 16
    cw = CONV_WIDTH
    cblk = OFF_CONV // cw
    gblk = OFF_GATE // d
    row = lambda w: pl.BlockSpec((None, tm, w), lambda i, t: (i, t, 0))
    pcol = lambda w, c: pl.BlockSpec((None, tm, w), lambda i, t, c=c: (i, t, c))
    prev = lambda c: pl.BlockSpec(
        (None, hp, cw), lambda i, t, c=c: (i, jnp.maximum(t * (tm // hp) - 1, 0), c))
    full = lambda a: pl.BlockSpec(a.shape, lambda i, t: (0,) * a.ndim)
    (o1, l1), (o2, l2), (o3, l3) = dil_outs
    dils = [dil for _, dil in DIL_GROUPS]
    grp = lambda dil: pl.BlockSpec((None, tm // dil, dil * DIL_OUT_W), lambda i, t: (i, t, 0))
    in_specs = [row(d), row(MOBA_W),
                pcol(cw, cblk), pcol(cw, cblk + 1), pcol(cw, cblk + 2),
                prev(cblk + 1), prev(cblk + 2),
                pcol(d, gblk), pcol(d, gblk + 1), pcol(d, gblk + 2),
                grp(dils[0]), grp(dils[1]), grp(dils[2]),
                grp(dils[0]), grp(dils[1]), grp(dils[2]),
                full(wm), full(wc), full(wd), full(wo), full(wsc), full(lng), full(lnb)]
    return pl.pallas_call(
        functools.partial(_merge_kernel, alpha),
        grid=(b, s // tm),
        in_specs=in_specs,
        out_specs=row(d),
        out_shape=jax.ShapeDtypeStruct((b, s, d), jnp.float32),
        scratch_shapes=[pltpu.VMEM((4, DIL_OUT_W // LANES, tm, LANES), jnp.float32)],
        compiler_params=_cparams(("arbitrary", "arbitrary")),
        name="merge",
    )(x3, ym, p3, p3, p3, p3, p3, p3, p3, p3, o1, o2, o3, l1, l2, l3,
      wm, wc, wd, wo, wsc, lng, lnb)


def _ffn_kernel(alpha, x_ref, xp_ref, wup_ref, wdn_ref, wcv_ref, bcv_ref, lng_ref, lnb_ref,
                out_ref, xs_ref, act_ref):
    f32 = jnp.float32
    tm = x_ref.shape[0]
    halo = xp_ref.shape[0]
    xp = jnp.where(pl.program_id(1) == 0, 0.0, xp_ref[...])
    xs_ref[0:halo, :] = xp.astype(jnp.bfloat16)
    xs_ref[halo:halo + tm, :] = x_ref[...].astype(jnp.bfloat16)

    def conv(u, w):
        y = u * w[0:1, :] + pltpu.roll(u, 1, axis=0) * w[1:2, :] + pltpu.roll(u, 2, axis=0) * w[2:3, :]
        return y[halo:, :]

    for c in range(FFN_NCHUNK):
        gc = slice(c * FFN_CHUNK, (c + 1) * FFN_CHUNK)
        vc = slice(D_FF + c * FFN_CHUNK, D_FF + (c + 1) * FFN_CHUNK)
        xs = xs_ref[...]
        ug = jnp.dot(xs, wup_ref[:, gc], preferred_element_type=f32)
        uv = jnp.dot(xs, wup_ref[:, vc], preferred_element_type=f32)
        hg = conv(ug, wcv_ref[:, gc]) + bcv_ref[:, gc]
        yv = conv(uv, wcv_ref[:, vc]) + bcv_ref[:, vc]
        act_ref[:, gc] = (hg * _twice_sigmoid_of_double(hg) * yv).astype(jnp.bfloat16)

    ffn = jnp.dot(act_ref[...], wdn_ref[...], preferred_element_type=f32)
    y = alpha * x_ref[...] + ffn
    out_ref[...] = _layer_norm(y, lng_ref[...], lnb_ref[...])


def _ffn(alpha, x3, wup, wdn, wcv, bcv, lng, lnb):
    b, s, d = x3.shape
    tm = FFN_TM
    halo = FFN_HALO
    full = lambda a: pl.BlockSpec(a.shape, lambda i, t: (0,) * a.ndim)
    return pl.pallas_call(
        functools.partial(_ffn_kernel, alpha),
        grid=(b, s // tm),
        in_specs=[pl.BlockSpec((None, tm, d), lambda i, t: (i, t, 0)),
                  pl.BlockSpec((None, halo, d),
                               lambda i, t: (i, jnp.maximum(t * (tm // halo) - 1, 0), 0)),
                  full(wup), full(wdn), full(wcv), full(bcv), full(lng), full(lnb)],
        out_specs=pl.BlockSpec((None, tm, d), lambda i, t: (i, t, 0)),
        out_shape=jax.ShapeDtypeStruct((b, s, d), jnp.float32),
        scratch_shapes=[pltpu.VMEM((tm + halo, d), jnp.bfloat16),
                        pltpu.VMEM((tm, D_FF), jnp.bfloat16)],
        compiler_params=_cparams(("arbitrary", "arbitrary")),
        name="conv_ffn",
    )(x3, x3, wup, wdn, wcv, bcv, lng, lnb)


def kernel(x, w_in, w_short_conv, w_moba_proj, w_dil_proj, w_conv_proj, w_mix_out,
           ln1_g, ln1_b, w_up, w_ffn_conv, b_ffn_conv, w_down, ln2_g, ln2_b):
    b, s, d = x.shape
    depth = w_in.shape[0]
    alpha = float((2 * depth) ** 0.25)
    bf = jnp.bfloat16
    for l in range(depth):
        x2 = x.reshape(b * s, d)
        p = _in_proj(x2, w_in, l).reshape(b, s, MAIN_COLS)
        pgs = _dil_proj(x2, w_in, l)
        ym = _moba(p)
        dil_outs = [_dil_group(pg, b, dil) for pg, (_, dil) in zip(pgs, DIL_GROUPS)]
        half = lambda w: (0.5 * w).astype(bf)
        x = _merge(alpha, x, ym, p, dil_outs,
                   half(w_moba_proj[l]), half(w_conv_proj[l]), half(w_dil_proj[l]),
                   w_mix_out[l].astype(bf),
                   w_short_conv[l], ln1_g[l].reshape(1, d), ln1_b[l].reshape(1, d))
        gate_half = jnp.where(jnp.arange(2 * D_FF) < D_FF, 0.5, 1.0)
        x = _ffn(alpha, x, w_up[l].astype(bf), w_down[l].astype(bf),
                 w_ffn_conv[l] * gate_half, (b_ffn_conv[l] * gate_half).reshape(1, 2 * D_FF),
                 ln2_g[l].reshape(1, d), ln2_b[l].reshape(1, d))
    return x
```

```python
import functools

import jax
import jax.numpy as jnp
import numpy as np
from jax import lax
from jax.experimental import pallas as pl
from jax.experimental.pallas import tpu as pltpu

D_MODEL = 1024
HEAD_DIM = 64
MOBA_HEADS = 8
MOBA_BLOCK = 256
MOBA_TOPK = 3
MOBA_W = MOBA_HEADS * HEAD_DIM
DIL_GROUPS = ((128, 1), (512, 4), (2048, 16))
DIL_HEADS_PER_GROUP = 4
DIL_W = 3 * DIL_HEADS_PER_GROUP * HEAD_DIM
DIL_OUT_W = DIL_HEADS_PER_GROUP * HEAD_DIM
CONV_WIDTH = 512
N_BRANCH = 3
D_FF = 2816
LN_EPS = 1e-5

OFF_CONV = 3 * MOBA_W
OFF_GATE = OFF_CONV + 3 * CONV_WIDTH
MAIN_COLS = OFF_GATE
GATE_COL0 = 3 * MOBA_W + 3 * DIL_W + 3 * CONV_WIDTH

LOG2E = float(np.log2(np.e))
LN2 = float(np.log(2.0))
Q_SCALE = HEAD_DIM ** -0.5 * LOG2E

LANES = 128
NEG = -1e30
VMEM_LIMIT = 56 * 1024 * 1024

PROJ_TM = 2048
PROJ_TN = 768
DILP_TM = 1024
MOBA_KTILE = 2 * MOBA_BLOCK
MOBA_UNROLL = 3
MOBA_ONES = 16
DIL_TQ = 2048
DIL_SPAN = 128
MERGE_TM = 512
FFN_TM = 512
FFN_HALO = 16
FFN_CHUNK = 256
FFN_NCHUNK = D_FF // FFN_CHUNK


def _cparams(sem):
    return pltpu.CompilerParams(dimension_semantics=sem, vmem_limit_bytes=VMEM_LIMIT)


def _proj_kernel(x_ref, w_ref, cs_ref, o_ref, xb_ref, wb_ref):
    j = pl.program_id(1)

    @pl.when(j == 0)
    def _():
        xb_ref[...] = x_ref[...].astype(jnp.bfloat16)

    @pl.when(pl.program_id(0) == 0)
    def _():
        wb_ref[j] = w_ref[...].astype(jnp.bfloat16)

    acc = jnp.dot(xb_ref[...], wb_ref[j], preferred_element_type=jnp.float32)
    o_ref[...] = (acc * cs_ref[...]).astype(o_ref.dtype)


def _main_scale_row():
    cs = np.ones((1, MAIN_COLS), np.float32)
    cs[:, :MOBA_W] = Q_SCALE
    return jnp.asarray(cs)


def _in_proj(x2, w_in, layer):
    n, d = x2.shape
    tn = PROJ_TN
    skip_at = 3 * MOBA_W // tn
    skip = 3 * DIL_W // tn
    n_ct = MAIN_COLS // tn
    src = lambda j: jnp.where(j < skip_at, j, j + skip)
    w_tile = lambda i, j: src(jnp.where(i == 0, j, n_ct - 1))
    return pl.pallas_call(
        _proj_kernel,
        grid=(n // PROJ_TM, n_ct),
        in_specs=[pl.BlockSpec((PROJ_TM, d), lambda i, j: (i, 0)),
                  pl.BlockSpec((None, d, tn), lambda i, j: (layer, 0, w_tile(i, j))),
                  pl.BlockSpec((1, tn), lambda i, j: (0, j))],
        out_specs=pl.BlockSpec((PROJ_TM, tn), lambda i, j: (i, j)),
        out_shape=jax.ShapeDtypeStruct((n, MAIN_COLS), jnp.bfloat16),
        scratch_shapes=[pltpu.VMEM((PROJ_TM, d), jnp.bfloat16),
                        pltpu.VMEM((n_ct, d, tn), jnp.bfloat16)],
        compiler_params=_cparams(("arbitrary", "arbitrary")),
        name="in_proj",
    )(x2, w_in, _main_scale_row())


def _dil_proj_kernel(x_ref, wq_ref, wk_ref, wv_ref, o0_ref, o1_ref, o2_ref, wb_ref, res_ref):
    tm = x_ref.shape[0]

    @pl.when(pl.program_id(0) == 0)
    def _():
        for j, w_ref in enumerate((wq_ref, wk_ref, wv_ref)):
            wb_ref[j] = w_ref[...].astype(jnp.bfloat16)

    xb = x_ref[...].astype(jnp.bfloat16)
    per_part = DIL_W // LANES
    for j in range(3):
        res = jnp.dot(xb, wb_ref[j], preferred_element_type=jnp.float32)
        if j == 0:
            res = res * Q_SCALE
        for cb in range(per_part):
            res_ref[j * per_part + cb] = res[:, cb * LANES:(cb + 1) * LANES]
    slabs = DIL_OUT_W // LANES
    for g, o_ref in enumerate((o0_ref, o1_ref, o2_ref)):
        d = DIL_GROUPS[g][1]
        rows = tm // d
        for r in range(d):
            for j in range(3):
                for sl in range(slabs):
                    src_cb = (j * DIL_W + g * DIL_OUT_W) // LANES + sl
                    dst_c = (r * 3 + j) * DIL_OUT_W + sl * LANES
                    if d == 1:
                        src = res_ref[src_cb]
                    else:
                        src = res_ref[src_cb, pl.ds(r, rows, stride=d), :]
                    o_ref[:, dst_c:dst_c + LANES] = src.astype(o_ref.dtype)


def _dil_proj(x2, w_in, layer):
    n, d = x2.shape
    tm = DILP_TM
    dils = [dil for _, dil in DIL_GROUPS]
    first = 3 * MOBA_W // DIL_W
    wspec = lambda j: pl.BlockSpec((None, d, DIL_W), lambda i, j=j: (layer, 0, first + j))
    return pl.pallas_call(
        _dil_proj_kernel,
        grid=(n // tm,),
        in_specs=[pl.BlockSpec((tm, d), lambda i: (i, 0)), wspec(0), wspec(1), wspec(2)],
        out_specs=[pl.BlockSpec((tm // dil, dil * 3 * DIL_OUT_W), lambda i: (i, 0))
                   for dil in dils],
        out_shape=[jax.ShapeDtypeStruct((n // dil, dil * 3 * DIL_OUT_W), jnp.bfloat16)
                   for dil in dils],
        scratch_shapes=[pltpu.VMEM((3, d, DIL_W), jnp.bfloat16),
                        pltpu.VMEM((3 * DIL_W // LANES, tm, LANES), jnp.float32)],
        compiler_params=_cparams(("arbitrary",)),
        name="dil_proj",
    )(x2, w_in, w_in, w_in)


def _moba_kernel(q_ref, k_ref, v_ref, o_ref, kaug_ref, vaug_ref, km_ref, sc_ref, qa_ref):
    s = q_ref.shape[0]
    nb = s // MOBA_BLOCK
    tq = kt = MOBA_KTILE
    f32 = jnp.float32
    bf = jnp.bfloat16
    sub_bf = 16
    lane_r = lax.broadcasted_iota(jnp.int32, (sub_bf, LANES), 1)
    row01 = lambda cond: cond.astype(f32).astype(bf)
    blk01 = lambda cond: jnp.broadcast_to(
        row01(cond)[None], (MOBA_BLOCK // sub_bf, sub_bf, LANES)).reshape(MOBA_BLOCK, LANES)
    head_lanes = (blk01(lane_r < HEAD_DIM), blk01(lane_r >= HEAD_DIM))
    for n in range(nb):
        rows = slice(n * MOBA_BLOCK, (n + 1) * MOBA_BLOCK)
        kblk = k_ref[rows, :]
        kaug_ref[0, rows, :] = kblk * head_lanes[0] + blk01(lane_r == HEAD_DIM + n)
        kaug_ref[1, rows, :] = kblk * head_lanes[1] + blk01(lane_r == n)
    vr = HEAD_DIM + MOBA_ONES
    lane_k = lax.broadcasted_iota(jnp.int32, (kt, LANES), 1)
    for t in range(s // kt):
        vt = v_ref[t * kt:(t + 1) * kt, :].astype(f32)
        vaug_ref[0, t] = jnp.where(lane_k < HEAD_DIM, vt, 1.0).T[:vr].astype(bf)
        vaug_ref[1, t] = jnp.where(lane_k >= HEAD_DIM, vt, 1.0).T[LANES - vr:].astype(bf)

    offs = (HEAD_DIM, 0)
    in_blk = (lax.broadcasted_iota(jnp.int32, (nb, s), 1) // MOBA_BLOCK
              == lax.broadcasted_iota(jnp.int32, (nb, s), 0))
    ksum = jnp.dot(row01(in_blk), k_ref[...], preferred_element_type=f32)
    kmean = ksum * (1.0 / MOBA_BLOCK)
    lane_b = lax.broadcasted_iota(jnp.int32, (nb, LANES), 1)
    km0 = jnp.where(lane_b < HEAD_DIM, kmean, 0.0)
    km1 = jnp.where(lane_b >= HEAD_DIM, kmean, 0.0)
    zgap = jnp.zeros((HEAD_DIM - nb, LANES), f32)
    km_ref[...] = jnp.concatenate([km1, zgap, km0, zgap], axis=0).astype(bf)

    lane_t = lax.broadcasted_iota(jnp.int32, (tq, LANES), 1)
    causal = (lax.broadcasted_iota(jnp.int32, (kt, tq), 0)
              <= lax.broadcasted_iota(jnp.int32, (kt, tq), 1))
    blk_i = lax.broadcasted_iota(jnp.int32, (nb, tq), 0)
    blk_f = blk_i.astype(f32)
    q_sub = lax.broadcasted_iota(jnp.int32, (nb, tq), 1) // MOBA_BLOCK
    zgap_t = jnp.zeros((HEAD_DIM - nb, tq), f32)
    nt = (((1,), (1,)), ((), ()))
    heads = (0, 1)
    q_lanes = (lane_t < HEAD_DIM, lane_t >= HEAD_DIM)

    def augmented_queries(tile):
        q2 = q_ref[pl.ds(pl.multiple_of(tile * tq, tq), tq), :]
        q2f = q2.astype(f32)
        gate_t = lax.dot_general(km_ref[...], q2, nt, preferred_element_type=f32)
        j = tile * (tq // MOBA_BLOCK) + q_sub
        valid = blk_i < j
        pens = []
        for h in heads:
            g = jnp.where(valid, gate_t[offs[h]:offs[h] + nb, :], NEG)
            sel = jnp.zeros((nb, tq), jnp.bool_)
            for _ in range(MOBA_TOPK):
                m = jnp.max(g, axis=0, keepdims=True)
                first = jnp.min(jnp.where(g == m, blk_f, 1e9), axis=0, keepdims=True)
                pick = blk_f == first
                sel = sel | pick
                g = jnp.where(pick, NEG, g)
            keep = (sel & valid) | (blk_i == j)
            pens.append(jnp.where(keep, 0.0, NEG))
        pen = jnp.concatenate([pens[1], zgap_t, pens[0], zgap_t], axis=0).T
        return [jnp.where(q_lanes[h], q2f, pen).astype(bf) for h in heads]

    def prepare(tile):
        r0 = pl.multiple_of(tile * tq, tq)
        for h, qa in enumerate(augmented_queries(tile)):
            sc = lax.dot_general(kaug_ref[h, pl.ds(r0, kt), :], qa, nt,
                                 preferred_element_type=f32)
            sc_ref[h] = jnp.where(causal, sc, NEG)
            qa_ref[h] = qa

    n_tiles = s // tq
    prepare(0)

    def q_tile(t_last, carry):
        r0 = pl.multiple_of(t_last * tq, tq)
        qas = [qa_ref[h] for h in heads]

        def tile_of(visit):
            return jnp.where(visit <= 0, t_last, visit - 1)

        def kv_step(i, mc):
            t_pv = tile_of(i)
            k0 = pl.multiple_of(jnp.clip(i, 0, jnp.maximum(t_last - 1, 0)) * kt, kt)
            out = []
            for h in heads:
                m_old, acc = mc[2 * h], mc[2 * h + 1]
                sc = sc_ref[h]
                sc_next = lax.dot_general(kaug_ref[h, pl.ds(k0, kt), :], qas[h], nt,
                                          preferred_element_type=f32)
                m_new = jnp.maximum(m_old, jnp.max(sc, axis=0, keepdims=True))
                alpha = jnp.exp2(m_old - m_new)
                p = jnp.exp2(sc - m_new).astype(bf)
                pv = jnp.dot(vaug_ref[h, t_pv], p, preferred_element_type=f32)
                sc_ref[h] = sc_next
                out += [m_new, acc * alpha + pv]
            return tuple(out)

        m_init = jnp.full((1, tq), NEG, f32)
        acc_init = jnp.zeros((vr, tq), f32)
        def multi_step(ii, mc):
            for u in range(MOBA_UNROLL):
                mc = kv_step(MOBA_UNROLL * ii + u, mc)
            return mc

        n_steps = t_last + 1
        n_main = n_steps // MOBA_UNROLL
        mc = lax.fori_loop(0, n_main, multi_step, (m_init, acc_init, m_init, acc_init))
        fin = lax.fori_loop(n_main * MOBA_UNROLL, n_steps, kv_step, mc)
        o0 = fin[1][:HEAD_DIM] / fin[1][HEAD_DIM:HEAD_DIM + 1]
        o1 = fin[3][MOBA_ONES:] / fin[3][0:1]
        o_ref[pl.ds(r0, tq), :] = jnp.concatenate([o0, o1], axis=0).T.astype(o_ref.dtype)
        prepare(jnp.minimum(t_last + 1, n_tiles - 1))
        return carry

    lax.fori_loop(0, n_tiles, q_tile, 0)


def _moba(p3):
    b, s, _ = p3.shape
    n_pairs = MOBA_HEADS // 2
    blk = (None, s, LANES)
    return pl.pallas_call(
        _moba_kernel,
        grid=(b, n_pairs),
        in_specs=[pl.BlockSpec(blk, lambda i, j: (i, 0, j)),
                  pl.BlockSpec(blk, lambda i, j: (i, 0, n_pairs + j)),
                  pl.BlockSpec(blk, lambda i, j: (i, 0, 2 * n_pairs + j))],
        out_specs=pl.BlockSpec(blk, lambda i, j: (i, 0, j)),
        out_shape=jax.ShapeDtypeStruct((b, s, MOBA_W), jnp.bfloat16),
        scratch_shapes=[pltpu.VMEM((2, s, LANES), jnp.bfloat16),
                        pltpu.VMEM((2, s // MOBA_KTILE, HEAD_DIM + MOBA_ONES, MOBA_KTILE),
                                   jnp.bfloat16),
                        pltpu.VMEM((LANES, LANES), jnp.bfloat16),
                        pltpu.VMEM((2, MOBA_KTILE, MOBA_KTILE), jnp.float32),
                        pltpu.VMEM((2, MOBA_KTILE, LANES), jnp.bfloat16)],
        compiler_params=_cparams(("arbitrary", "arbitrary")),
        name="moba_attn",
    )(p3, p3, p3)


def _dil_kernel(n_res, *refs):
    q_refs, k_refs, v_refs = refs[:n_res], refs[n_res:2 * n_res], refs[2 * n_res:3 * n_res]
    o_ref, lse_ref = refs[3 * n_res:]
    tq = q_refs[0].shape[0]
    sub = DIL_SPAN
    kw = 2 * sub
    base_tile = pl.program_id(2) * tq
    lane = lax.broadcasted_iota(jnp.int32, (sub, LANES), 1)
    first = lane < HEAD_DIM
    diff = (lax.broadcasted_iota(jnp.int32, (sub, kw), 0)
            - lax.broadcasted_iota(jnp.int32, (sub, kw), 1))
    band = lambda dist: jnp.where((dist >= 0) & (dist <= DIL_SPAN), 0.0, NEG)
    bias_in = band(diff + sub)
    bias_first = jnp.where(base_tile == 0, band(diff), bias_in)
    nt = (((1,), (1,)), ((), ()))

    units = [(ri, sb, pr) for ri in range(n_res) for sb in range(tq // sub)
             for pr in range(DIL_HEADS_PER_GROUP // 2)]
    for ri, sb, pr in units:
        q0 = sb * sub
        kstart = pl.multiple_of(jnp.maximum(base_tile + q0 - sub, 0), sub)
        bias = bias_first if sb == 0 else bias_in
        cs = slice(pr * LANES, (pr + 1) * LANES)
        oc = slice(ri * DIL_OUT_W + pr * LANES, ri * DIL_OUT_W + (pr + 1) * LANES)
        q2 = q_refs[ri][pl.ds(q0, sub), cs].astype(jnp.float32)
        k2 = k_refs[ri][pl.ds(kstart, kw), cs]
        v2 = v_refs[ri][pl.ds(kstart, kw), cs]
        outs, lses = [], []
        for h in range(2):
            qh = jnp.where(first if h == 0 else jnp.logical_not(first), q2, 0.0)
            qh = qh.astype(jnp.bfloat16)
            sc = lax.dot_general(qh, k2, nt, preferred_element_type=jnp.float32) + bias
            m = jnp.max(sc, axis=1, keepdims=True)
            p = jnp.exp2(sc - m)
            l = jnp.sum(p, axis=1, keepdims=True)
            o = jnp.dot(p.astype(jnp.bfloat16), v2, preferred_element_type=jnp.float32)
            outs.append(o / l)
            lses.append(jnp.broadcast_to(m * LN2 + jnp.log(l), (sub, LANES)))
        o_ref[pl.ds(q0, sub), oc] = jnp.where(first, outs[0], outs[1]).astype(o_ref.dtype)
        lse_ref[pl.ds(q0, sub), oc] = jnp.where(first, lses[0], lses[1])


def _dil_group(pg, b, dil):
    w = DIL_OUT_W
    l = pg.shape[0] // b
    pv = pg.reshape(b, l, dil * 3 * w)
    tq = min(DIL_TQ, l)
    n_res = min(dil, DIL_TQ // tq)
    qkv = lambda part, width, rows: [
        pl.BlockSpec((None, rows, width),
                     lambda i, r, t, ri=ri: (i, t if part == 0 else 0, 3 * (r * n_res + ri) + part))
        for ri in range(n_res)]
    out = pl.BlockSpec((None, tq, n_res * w), lambda i, r, t: (i, t, r))
    o, lse = pl.pallas_call(
        functools.partial(_dil_kernel, n_res),
        grid=(b, dil // n_res, l // tq),
        in_specs=qkv(0, w, tq) + qkv(1, w, l) + qkv(2, w, l),
        out_specs=[out, out],
        out_shape=[jax.ShapeDtypeStruct((b, l, dil * w), jnp.bfloat16),
                   jax.ShapeDtypeStruct((b, l, dil * w), jnp.float32)],
        compiler_params=_cparams(("arbitrary", "arbitrary", "arbitrary")),
        name=f"dil_attn_d{dil}",
    )(*([pv] * (3 * n_res)))
    return o, lse


def _layer_norm(y, g, b):
    mu = jnp.mean(y, axis=-1, keepdims=True)
    yc = y - mu
    var = jnp.mean(yc * yc, axis=-1, keepdims=True)
    return yc * lax.rsqrt(var + LN_EPS) * g + b


def _twice_sigmoid_of_double(h):
    return 1.0 + jnp.tanh(h)


def _natural_order(src_ref, dst_ref, a, dil):
    rows = src_ref.shape[0]
    slabs = DIL_OUT_W // LANES
    for r in range(dil):
        for sl in range(slabs):
            c = r * DIL_OUT_W + sl * LANES
            dst_ref[a, sl, pl.ds(r, rows, stride=dil), :] = (
                src_ref[:, c:c + LANES].astype(jnp.float32))
    return jnp.concatenate([dst_ref[a, sl] for sl in range(slabs)], axis=1)


def _merge_kernel(alpha, x_ref, ym_ref, bg_ref, cg_ref, hc_ref, cgp_ref, hcp_ref,
                  wg_ref, o1_ref, o2_ref, o3_ref, l1_ref, l2_ref, l3_ref,
                  wm_ref, wc_ref, wd_ref, wo_ref, wsc_ref, lng_ref, lnb_ref, out_ref, nat_ref):
    f32 = jnp.float32
    tm = x_ref.shape[0]
    ch = cg_ref[...].astype(f32) * hc_ref[...].astype(f32)
    chp = cgp_ref[...].astype(f32) * hcp_ref[...].astype(f32)
    chp = jnp.where(pl.program_id(1) == 0, 0.0, chp)
    hp = chp.shape[0]
    prev1 = chp[hp - 1:hp, :]
    prev2 = chp[hp - 2:hp - 1, :]
    row = lax.broadcasted_iota(jnp.int32, (tm, CONV_WIDTH), 0)
    ch1 = jnp.where(row == 0, prev1, pltpu.roll(ch, 1, axis=0))
    ch2 = pltpu.roll(ch, 2, axis=0)
    ch2 = jnp.where(row == 0, prev2, jnp.where(row == 1, prev1, ch2))
    wsc = wsc_ref[...]
    conv = ch * wsc[0:1, :] + ch1 * wsc[1:2, :] + ch2 * wsc[2:3, :]
    y_conv = (bg_ref[...].astype(f32) * conv).astype(jnp.bfloat16)

    d2, d3 = DIL_GROUPS[1][1], DIL_GROUPS[2][1]
    o1, l1 = o1_ref[...].astype(f32), l1_ref[...]
    o2, l2 = _natural_order(o2_ref, nat_ref, 0, d2), _natural_order(l2_ref, nat_ref, 1, d2)
    o3, l3 = _natural_order(o3_ref, nat_ref, 2, d3), _natural_order(l3_ref, nat_ref, 3, d3)
    lm = jnp.maximum(jnp.maximum(l1, l2), l3)
    e1, e2, e3 = jnp.exp(l1 - lm), jnp.exp(l2 - lm), jnp.exp(l3 - lm)
    y_dil = ((e1 * o1 + e2 * o2 + e3 * o3) / (e1 + e2 + e3)).astype(jnp.bfloat16)

    pm = jnp.dot(ym_ref[...], wm_ref[...], preferred_element_type=f32)
    pc = jnp.dot(y_conv, wc_ref[...], preferred_element_type=f32)
    pd = jnp.dot(y_dil, wd_ref[...], preferred_element_type=f32)
    xb = x_ref[...].astype(jnp.bfloat16)
    dm = D_MODEL
    gate = lambda g: _twice_sigmoid_of_double(
        jnp.dot(xb, wg_ref[:, g * dm:(g + 1) * dm], preferred_element_type=f32))
    merged = gate(0) * pm + gate(1) * pc + gate(2) * pd
    mix = jnp.dot(merged.astype(jnp.bfloat16), wo_ref[...], preferred_element_type=f32)
    y = alpha * x_ref[...] + mix
    out_ref[...] = _layer_norm(y, lng_ref[...], lnb_ref[...])


def _merge(alpha, x3, ym, p3, dil_outs, wg, wm, wc, wd, wo, wsc, lng, lnb):
    b, s, d = x3.shape
    tm = MERGE_TM
    hp = 16
    cw = CONV_WIDTH
    cblk = OFF_CONV // cw
    row = lambda w: pl.BlockSpec((None, tm, w), lambda i, t: (i, t, 0))
    pcol = lambda w, c: pl.BlockSpec((None, tm, w), lambda i, t, c=c: (i, t, c))
    prev = lambda c: pl.BlockSpec(
        (None, hp, cw), lambda i, t, c=c: (i, jnp.maximum(t * (tm // hp) - 1, 0), c))
    full = lambda a: pl.BlockSpec(a.shape, lambda i, t: (0,) * a.ndim)
    (o1, l1), (o2, l2), (o3, l3) = dil_outs
    dils = [dil for _, dil in DIL_GROUPS]
    grp = lambda dil: pl.BlockSpec((None, tm // dil, dil * DIL_OUT_W), lambda i, t: (i, t, 0))
    in_specs = [row(d), row(MOBA_W),
                pcol(cw, cblk), pcol(cw, cblk + 1), pcol(cw, cblk + 2),
                prev(cblk + 1), prev(cblk + 2),
                full(wg),
                grp(dils[0]), grp(dils[1]), grp(dils[2]),
                grp(dils[0]), grp(dils[1]), grp(dils[2]),
                full(wm), full(wc), full(wd), full(wo), full(wsc), full(lng), full(lnb)]
    return pl.pallas_call(
        functools.partial(_merge_kernel, alpha),
        grid=(b, s // tm),
        in_specs=in_specs,
        out_specs=row(d),
        out_shape=jax.ShapeDtypeStruct((b, s, d), jnp.float32),
        scratch_shapes=[pltpu.VMEM((4, DIL_OUT_W // LANES, tm, LANES), jnp.float32)],
        compiler_params=_cparams(("arbitrary", "arbitrary")),
        name="merge",
    )(x3, ym, p3, p3, p3, p3, p3, wg, o1, o2, o3, l1, l2, l3,
      wm, wc, wd, wo, wsc, lng, lnb)


def _ffn_kernel(alpha, x_ref, xp_ref, wup_ref, wdn_ref, wcv_ref, bcv_ref, lng_ref, lnb_ref,
                out_ref, xs_ref, act_ref):
    f32 = jnp.float32
    tm = x_ref.shape[0]
    halo = xp_ref.shape[0]
    xp = jnp.where(pl.program_id(1) == 0, 0.0, xp_ref[...])
    xs_ref[0:halo, :] = xp.astype(jnp.bfloat16)
    xs_ref[halo:halo + tm, :] = x_ref[...].astype(jnp.bfloat16)

    def conv(u, w):
        y = u * w[0:1, :] + pltpu.roll(u, 1, axis=0) * w[1:2, :] + pltpu.roll(u, 2, axis=0) * w[2:3, :]
        return y[halo:, :]

    for c in range(FFN_NCHUNK):
        gc = slice(c * FFN_CHUNK, (c + 1) * FFN_CHUNK)
        vc = slice(D_FF + c * FFN_CHUNK, D_FF + (c + 1) * FFN_CHUNK)
        xs = xs_ref[...]
        ug = jnp.dot(xs, wup_ref[:, gc], preferred_element_type=f32)
        uv = jnp.dot(xs, wup_ref[:, vc], preferred_element_type=f32)
        hg = conv(ug, wcv_ref[:, gc]) + bcv_ref[:, gc]
        yv = conv(uv, wcv_ref[:, vc]) + bcv_ref[:, vc]
        act_ref[:, gc] = (hg * _twice_sigmoid_of_double(hg) * yv).astype(jnp.bfloat16)

    ffn = jnp.dot(act_ref[...], wdn_ref[...], preferred_element_type=f32)
    y = alpha * x_ref[...] + ffn
    out_ref[...] = _layer_norm(y, lng_ref[...], lnb_ref[...])


def _ffn(alpha, x3, wup, wdn, wcv, bcv, lng, lnb):
    b, s, d = x3.shape
    tm = FFN_TM
    halo = FFN_HALO
    full = lambda a: pl.BlockSpec(a.shape, lambda i, t: (0,) * a.ndim)
    return pl.pallas_call(
        functools.partial(_ffn_kernel, alpha),
        grid=(b, s // tm),
        in_specs=[pl.BlockSpec((None, tm, d), lambda i, t: (i, t, 0)),
                  pl.BlockSpec((None, halo, d),
                               lambda i, t: (i, jnp.maximum(t * (tm // halo) - 1, 0), 0)),
                  full(wup), full(wdn), full(wcv), full(bcv), full(lng), full(lnb)],
        out_specs=pl.BlockSpec((None, tm, d), lambda i, t: (i, t, 0)),
        out_shape=jax.ShapeDtypeStruct((b, s, d), jnp.float32),
        scratch_shapes=[pltpu.VMEM((tm + halo, d), jnp.bfloat16),
                        pltpu.VMEM((tm, D_FF), jnp.bfloat16)],
        compiler_params=_cparams(("arbitrary", "arbitrary")),
        name="conv_ffn",
    )(x3, x3, wup, wdn, wcv, bcv, lng, lnb)


def kernel(x, w_in, w_short_conv, w_moba_proj, w_dil_proj, w_conv_proj, w_mix_out,
           ln1_g, ln1_b, w_up, w_ffn_conv, b_ffn_conv, w_down, ln2_g, ln2_b):
    b, s, d = x.shape
    depth = w_in.shape[0]
    alpha = float((2 * depth) ** 0.25)
    bf = jnp.bfloat16
    for l in range(depth):
        x2 = x.reshape(b * s, d)
        p = _in_proj(x2, w_in, l).reshape(b, s, MAIN_COLS)
        pgs = _dil_proj(x2, w_in, l)
        ym = _moba(p)
        dil_outs = [_dil_group(pg, b, dil) for pg, (_, dil) in zip(pgs, DIL_GROUPS)]
        half = lambda w: (0.5 * w).astype(bf)
        x = _merge(alpha, x, ym, p, dil_outs, half(w_in[l][:, GATE_COL0:]),
                   half(w_moba_proj[l]), half(w_conv_proj[l]), half(w_dil_proj[l]),
                   w_mix_out[l].astype(bf),
                   w_short_conv[l], ln1_g[l].reshape(1, d), ln1_b[l].reshape(1, d))
        gate_half = jnp.where(jnp.arange(2 * D_FF) < D_FF, 0.5, 1.0)
        x = _ffn(alpha, x, w_up[l].astype(bf), w_down[l].astype(bf),
                 w_ffn_conv[l] * gate_half, (b_ffn_conv[l] * gate_half).reshape(1, 2 * D_FF),
                 ln2_g[l].reshape(1, d), ln2_b[l].reshape(1, d))
    return x
```
